```python
import jax, jax.numpy as jnp
from jax import lax
import numpy as np

D_MODEL = 1024
BATCH = 8
SEQ = 2048
DEPTH = 2
DEC_BATCH = 128
DEC_SEQ = 8
PAST_LEN = 16384
PAGE_SIZE = 128

H_A = 6
DK_A = 64
DV_A = 64
QK_A = H_A * DK_A
W_A = H_A * DV_A
CONV_A = 4
W_B = 256
LRU_BLOCKS = 4
LRU_BLOCK = W_B // LRU_BLOCKS
CONV_B = 4
LRU_C = 8.0
H_C = 6
DK_C = 32
DV_C = 64
QK_C = H_C * DK_C
W_C = H_C * DV_C
GLA_RANK = 16
GLA_TAU = 16.0
D_MIX = W_A + W_B + W_C
CHUNK = 64
D_FF = 2816
CONV_F = 3
ALPHA = (2.0 * DEPTH) ** 0.25
BETA_INIT = (8.0 * DEPTH) ** -0.25
EPS = 1e-6

SPLIT_SIZES = (QK_A, QK_A, W_A, W_A, H_A, H_A,
               W_B, W_B,
               QK_C, QK_C, W_C, W_C, GLA_RANK)
D_IN = sum(SPLIT_SIZES)
SPLIT_POINTS = tuple(int(s) for s in np.cumsum(SPLIT_SIZES)[:-1])
CONV_A_WIDTH = 2 * QK_A + W_A

kernel_name = 'hybrid_deltanet_rglru_gla_step'


def layer_norm(x, g, b):
    xf = x.astype(jnp.float32)
    mu = xf.mean(-1, keepdims=True)
    var = jnp.square(xf - mu).mean(-1, keepdims=True)
    return ((xf - mu) * lax.rsqrt(var + EPS) * g.astype(jnp.float32) + b.astype(jnp.float32)).astype(x.dtype)


def head_rms_norm(o, g):
    return o * lax.rsqrt(jnp.mean(jnp.square(o), -1, keepdims=True) + EPS) * g.astype(jnp.float32)


def l2norm(x):
    xf = x.astype(jnp.float32)
    return xf * lax.rsqrt(jnp.sum(jnp.square(xf), -1, keepdims=True) + EPS)


def causal_dwconv(x, buf, w):
    width = w.shape[0]
    T = x.shape[1]
    xp = jnp.concatenate([buf.astype(x.dtype), x], axis=1)
    y = xp[:, 0:T] * w[0]
    for i in range(1, width):
        y = y + xp[:, i:i + T] * w[i]
    return y, xp[:, -(width - 1):]


def chunk_size(T):
    return CHUNK if T % CHUNK == 0 else T


def to_chunks(x, c):
    B, T, H, D = x.shape
    return x.reshape(B, T // c, c, H, D).transpose(1, 0, 3, 2, 4)


def from_chunks(x):
    n, B, H, c, D = x.shape
    return x.transpose(1, 0, 3, 2, 4).reshape(B, n * c, H, D)


def gated_delta_chunked(q, k, v, beta, g, s0):
    f32 = jnp.float32
    T = q.shape[1]
    dv = v.shape[-1]
    c = chunk_size(T)
    qc, kc, vc = (to_chunks(t.astype(f32), c) for t in (q, k, v))
    bc = to_chunks(beta.astype(f32)[..., None], c)[..., 0]
    gc = to_chunks(g.astype(f32)[..., None], c)[..., 0]
    incl = jnp.tril(jnp.ones((c, c), bool))
    strict = jnp.tril(jnp.ones((c, c), bool), -1)
    eye = jnp.eye(c, dtype=f32)

    def step(S, inp):
        qi, ki, vi, bi, gi = inp
        gcum = jnp.cumsum(gi, axis=-1)
        diff = gcum[..., :, None] - gcum[..., None, :]
        dec_incl = jnp.exp(jnp.where(incl, diff, -jnp.inf))
        dec_strict = jnp.where(strict, dec_incl, 0.0)
        kk = jnp.einsum('bhid,bhjd->bhij', ki, ki)
        lhs = eye + bi[..., :, None] * kk * dec_strict
        rhs = jnp.concatenate([bi[..., None] * vi, (bi * jnp.exp(gcum))[..., None] * ki], axis=-1)
        sol = lax.linalg.triangular_solve(lhs, rhs, left_side=True, lower=True, unit_diagonal=True)
        u, wk = sol[..., :dv], sol[..., dv:]
        w = u - jnp.einsum('bhik,bhkv->bhiv', wk, S)
        qk = jnp.einsum('bhid,bhjd->bhij', qi, ki) * dec_incl
        o = jnp.exp(gcum)[..., None] * jnp.einsum('bhik,bhkv->bhiv', qi, S) + jnp.einsum('bhij,bhjv->bhiv', qk, w)
        g_last = gcum[..., -1:]
        k_dec = ki * jnp.exp(g_last - gcum)[..., None]
        S_new = jnp.exp(g_last)[..., None] * S + jnp.einsum('bhjk,bhjv->bhkv', k_dec, w)
        return S_new, o

    S, o = lax.scan(step, s0.astype(f32), (qc, kc, vc, bc, gc))
    return from_chunks(o), S


def gla_chunked(q, k, v, logf, s0):
    f32 = jnp.float32
    T = q.shape[1]
    c = chunk_size(T)
    qc, kc, vc, fc = (to_chunks(t.astype(f32), c) for t in (q, k, v, logf))
    incl = jnp.tril(jnp.ones((c, c), bool))[..., None]

    def step(S, inp):
        qi, ki, vi, fi = inp
        b = jnp.cumsum(fi, axis=-2)
        diff = b[..., :, None, :] - b[..., None, :, :]
        dec = jnp.exp(jnp.where(incl, diff, -jnp.inf))
        att = jnp.einsum('bhik,bhjk,bhijk->bhij', qi, ki, dec)
        o = jnp.einsum('bhik,bhkv->bhiv', qi * jnp.exp(b), S) + jnp.einsum('bhij,bhjv->bhiv', att, vi)
        b_last = b[..., -1:, :]
        S_new = jnp.exp(b_last[..., 0, :])[..., None] * S + jnp.einsum('bhjk,bhjv->bhkv', ki * jnp.exp(b_last - b), vi)
        return S_new, o

    S, o = lax.scan(step, s0.astype(f32), (qc, kc, vc, fc))
    return from_chunks(o), S


def rg_lru(xc, h0, w_r, b_r, w_i, b_i, lam):
    f32 = jnp.float32
    B, T, _ = xc.shape
    xf = xc.astype(f32)
    xb = xf.reshape(B, T, LRU_BLOCKS, LRU_BLOCK)
    r = jax.nn.sigmoid(jnp.einsum('btnd,nde->btne', xb, w_r.astype(f32)).reshape(B, T, W_B) + b_r.astype(f32))
    i = jax.nn.sigmoid(jnp.einsum('btnd,nde->btne', xb, w_i.astype(f32)).reshape(B, T, W_B) + b_i.astype(f32))
    log_a = -LRU_C * r * jax.nn.softplus(-lam.astype(f32))
    a = jnp.exp(log_a)
    bx = jnp.sqrt(-jnp.expm1(2.0 * log_a)) * (i * xf)
    bx = bx.at[:, 0].add(a[:, 0] * h0.astype(f32))

    def combine(left, right):
        a1, b1 = left
        a2, b2 = right
        return a1 * a2, a2 * b1 + b2

    _, h = lax.associative_scan(combine, (a, bx), axis=1)
    return h, h[:, -1]


def conv_ffn(x, buf, w_up, conv_w, conv_b, w_down):
    up = x @ w_up
    gate, val = jnp.split(up, [D_FF], axis=-1)
    gate, buf_new = causal_dwconv(gate, buf, conv_w)
    h = jax.nn.gelu(gate + conv_b) * val
    return h @ w_down, buf_new


def _layer(x, st_dconv, st_delta, st_lconv, st_lru, st_gla, st_fconv,
           w_in, conv_a_w, a_log, dt_bias, norm_a_w, conv_b_w, conv_b_b,
           lru_w_r, lru_b_r, lru_w_i, lru_b_i, lru_lambda, gla_w2, gla_b2, norm_c_w,
           w_out, ln1_g, ln1_b, ffn_w_up, ffn_conv_w, ffn_conv_b, ffn_w_down, ln2_g, ln2_b):
    f32 = jnp.float32
    B, T, _ = x.shape
    proj = x @ w_in
    (qa, ka, va, za, ba, aa, xb, gb, qc, kc, vc, zc, lc) = jnp.split(proj, SPLIT_POINTS, axis=-1)

    qkv, dconv_new = causal_dwconv(jnp.concatenate([qa, ka, va], axis=-1), st_dconv, conv_a_w)
    qkv = jax.nn.silu(qkv)
    qa, ka, va = jnp.split(qkv, [QK_A, 2 * QK_A], axis=-1)
    qa = l2norm(qa.reshape(B, T, H_A, DK_A)) * DK_A ** -0.5
    ka = l2norm(ka.reshape(B, T, H_A, DK_A))
    va = va.reshape(B, T, H_A, DV_A)
    beta = jax.nn.sigmoid(ba.astype(f32))
    g = -jnp.exp(a_log.astype(f32)) * jax.nn.softplus(aa.astype(f32) + dt_bias.astype(f32))
    oa, delta_new = gated_delta_chunked(qa, ka, va, beta, g, st_delta)
    oa = head_rms_norm(oa, norm_a_w) * jax.nn.silu(za.reshape(B, T, H_A, DV_A).astype(f32))

    xb, lconv_new = causal_dwconv(xb, st_lconv, conv_b_w)
    ob, lru_new = rg_lru(xb + conv_b_b, st_lru, lru_w_r, lru_b_r, lru_w_i, lru_b_i, lru_lambda)
    ob = ob * jax.nn.gelu(gb.astype(f32))

    qc = qc.reshape(B, T, H_C, DK_C).astype(f32) * DK_C ** -0.5
    kc = kc.reshape(B, T, H_C, DK_C)
    vc = vc.reshape(B, T, H_C, DV_C)
    logf = jax.nn.log_sigmoid((lc @ gla_w2 + gla_b2).astype(f32)) / GLA_TAU
    oc, gla_new = gla_chunked(qc, kc, vc, logf.reshape(B, T, H_C, DK_C), st_gla)
    oc = head_rms_norm(oc, norm_c_w) * jax.nn.silu(zc.reshape(B, T, H_C, DV_C).astype(f32))

    heads = jnp.concatenate([oa.reshape(B, T, W_A), ob, oc.reshape(B, T, W_C)], axis=-1).astype(x.dtype)
    x = layer_norm(ALPHA * x + heads @ w_out, ln1_g, ln1_b)
    f, fconv_new = conv_ffn(x, st_fconv, ffn_w_up, ffn_conv_w, ffn_conv_b, ffn_w_down)
    x = layer_norm(ALPHA * x + f, ln2_g, ln2_b)
    return x, (dconv_new, delta_new, lconv_new, lru_new, gla_new, fconv_new)


def _trunk(x, states, weights):
    per_layer = []
    for l in range(DEPTH):
        x, st = _layer(x, *(s[l] for s in states), *(w[l] for w in weights))
        per_layer.append(st)
    stacked = tuple(jnp.stack([st[i] for st in per_layer]).astype(x.dtype) for i in range(len(states)))
    return x, stacked


def setup_inputs(seed: int = 0) -> dict:
    key = jax.random.key(seed)
    ks = iter(jax.random.split(key, 40))
    nrm = lambda shape, s: jax.random.normal(next(ks), shape, jnp.float32) * s
    L = DEPTH
    u_decay = jax.random.uniform(next(ks), (L, H_A), jnp.float32, 1.0, 16.0)
    dt = jnp.exp(jax.random.uniform(next(ks), (L, H_A), jnp.float32, np.log(1e-3), np.log(1e-1)))
    a8 = jax.random.uniform(next(ks), (L, W_B), jnp.float32, 0.9, 0.999)
    a1 = a8 ** (1.0 / LRU_C)
    return {
        'x_prompt': nrm((BATCH, SEQ, D_MODEL), 1.0),
        'x_sample': nrm((DEC_BATCH, DEC_SEQ, D_MODEL), 1.0),
        'state_delta_conv': nrm((L, DEC_BATCH, CONV_A - 1, CONV_A_WIDTH), 1.0),
        'state_delta': nrm((L, DEC_BATCH, H_A, DK_A, DV_A), 0.1),
        'state_lru_conv': nrm((L, DEC_BATCH, CONV_B - 1, W_B), 1.0),
        'state_lru': nrm((L, DEC_BATCH, W_B), 0.5),
        'state_gla': nrm((L, DEC_BATCH, H_C, DK_C, DV_C), 0.5),
        'state_ffn_conv': nrm((L, DEC_BATCH, CONV_F - 1, D_FF), 1.0),
        'w_in': nrm((L, D_MODEL, D_IN), D_MODEL ** -0.5),
        'conv_a_w': nrm((L, CONV_A, CONV_A_WIDTH), 0.5),
        'a_log': jnp.log(u_decay),
        'dt_bias': dt + jnp.log(-jnp.expm1(-dt)),
        'norm_a_w': 1.0 + nrm((L, DV_A), 0.1),
        'conv_b_w': nrm((L, CONV_B, W_B), 0.5),
        'conv_b_b': nrm((L, W_B), 0.02),
        'lru_w_r': nrm((L, LRU_BLOCKS, LRU_BLOCK, LRU_BLOCK), LRU_BLOCK ** -0.5),
        'lru_b_r': nrm((L, W_B), 0.02),
        'lru_w_i': nrm((L, LRU_BLOCKS, LRU_BLOCK, LRU_BLOCK), LRU_BLOCK ** -0.5),
        'lru_b_i': nrm((L, W_B), 0.02),
        'lru_lambda': jnp.log(a1) - jnp.log1p(-a1),
        'gla_w2': nrm((L, GLA_RANK, QK_C), GLA_RANK ** -0.5),
        'gla_b2': nrm((L, QK_C), 0.1),
        'norm_c_w': 1.0 + nrm((L, DV_C), 0.1),
        'w_out': nrm((L, D_MIX, D_MODEL), D_MIX ** -0.5 * BETA_INIT),
        'ln1_g': 1.0 + nrm((L, D_MODEL), 0.05),
        'ln1_b': nrm((L, D_MODEL), 0.02),
        'ffn_w_up': nrm((L, D_MODEL, 2 * D_FF), D_MODEL ** -0.5),
        'ffn_conv_w': nrm((L, CONV_F, D_FF), 0.5),
        'ffn_conv_b': nrm((L, D_FF), 0.02),
        'ffn_w_down': nrm((L, D_FF, D_MODEL), D_FF ** -0.5 * BETA_INIT),
        'ln2_g': 1.0 + nrm((L, D_MODEL), 0.05),
        'ln2_b': nrm((L, D_MODEL), 0.02),
    }


def reference(x_prompt, x_sample, state_delta_conv, state_delta, state_lru_conv, state_lru, state_gla, state_ffn_conv,
              w_in, conv_a_w, a_log, dt_bias, norm_a_w, conv_b_w, conv_b_b,
              lru_w_r, lru_b_r, lru_w_i, lru_b_i, lru_lambda, gla_w2, gla_b2, norm_c_w,
              w_out, ln1_g, ln1_b, ffn_w_up, ffn_conv_w, ffn_conv_b, ffn_w_down, ln2_g, ln2_b):
    weights = (w_in, conv_a_w, a_log, dt_bias, norm_a_w, conv_b_w, conv_b_b,
               lru_w_r, lru_b_r, lru_w_i, lru_b_i, lru_lambda, gla_w2, gla_b2, norm_c_w,
               w_out, ln1_g, ln1_b, ffn_w_up, ffn_conv_w, ffn_conv_b, ffn_w_down, ln2_g, ln2_b)
    sample_states = (state_delta_conv, state_delta, state_lru_conv, state_lru, state_gla, state_ffn_conv)
    prompt_states = tuple(jnp.zeros((DEPTH, BATCH) + s.shape[2:], x_prompt.dtype) for s in sample_states)
    y_prompt, (p_dconv, p_delta, p_lconv, p_lru, p_gla, p_fconv) = _trunk(x_prompt, prompt_states, weights)
    y_sample, (s_dconv, s_delta, s_lconv, s_lru, s_gla, s_fconv) = _trunk(x_sample, sample_states, weights)
    return (y_prompt, y_sample,
            p_dconv, p_delta, p_lconv, p_lru, p_gla, p_fconv,
            s_dconv, s_delta, s_lconv, s_lru, s_gla, s_fconv)
```

```python
import functools
import math

import jax
import jax.numpy as jnp
from jax import lax
from jax.experimental import pallas as pl
from jax.experimental.pallas import tpu as pltpu

F32 = jnp.float32
BF16 = jnp.bfloat16

D_MODEL = 1024
H_A, DK_A, DV_A = 6, 64, 64
QK_A = H_A * DK_A
W_A = H_A * DV_A
CONV_A = 4
W_B = 256
LRU_BLOCKS = 4
LRU_BLOCK = W_B // LRU_BLOCKS
CONV_B = 4
LRU_C = 8.0
H_C, DK_C, DV_C = 6, 32, 64
QK_C = H_C * DK_C
W_C = H_C * DV_C
GLA_RANK = 16
GLA_TAU = 16.0
D_MIX = W_A + W_B + W_C
CHUNK = 64
D_FF = 2816
CONV_F = 3
EPS = 1e-6
CONV_A_WIDTH = 2 * QK_A + W_A
DEPTH = 2
ALPHA = (2.0 * DEPTH) ** 0.25

LANES = 128
SUBLANES = 8
HALF = LANES // 2
GROUP_ROWS = 64
STACK = 2 * GROUP_ROWS
N_PAIRS = 3
HIST = SUBLANES
FF_CHUNK = 256
N_FF_CHUNKS = D_FF // FF_CHUNK
VMEM_LIMIT_BYTES = 56 * 1024 * 1024

OFF_QA = 0
OFF_KA = OFF_QA + QK_A
OFF_VA = OFF_KA + QK_A
OFF_ZA = OFF_VA + W_A
OFF_XB = OFF_ZA + W_A
OFF_GB = OFF_XB + W_B
OFF_QC = OFF_GB + W_B
QKC_PAD = 2 * LANES
OFF_KC = OFF_QC + QKC_PAD
OFF_VC = OFF_KC + QKC_PAD
OFF_ZC = OFF_VC + W_C
OFF_SMALL = OFF_ZC + W_C
D_IN_PAD = OFF_SMALL + LANES
LANE_BETA = 0
LANE_DECAY = 8
LANE_LOWRANK = 16


def _mm(a, b):
    return jnp.dot(a.astype(BF16), b.astype(BF16), preferred_element_type=F32)


def _mm_nt(a, b):
    return lax.dot_general(a.astype(BF16), b.astype(BF16), (((1,), (1,)), ((), ())), preferred_element_type=F32)


def _split(a, n):
    parts = []
    r = a
    for i in range(n):
        p = r.astype(BF16)
        parts.append(p)
        if i + 1 < n:
            r = r - p.astype(F32)
    return parts


def _mm_mask_lhs(mask_bf16, a, n=3):
    out = None
    for p in _split(a, n):
        t = jnp.dot(mask_bf16, p, preferred_element_type=F32)
        out = t if out is None else out + t
    return out


def _mm_mask_rhs(a, mask_bf16, n=2):
    out = None
    for p in _split(a, n):
        t = jnp.dot(p, mask_bf16, preferred_element_type=F32)
        out = t if out is None else out + t
    return out


def _mm_hi(a, b):
    ah, al = _split(a, 2)
    bh, bl = _split(b, 2)
    return (jnp.dot(ah, bh, preferred_element_type=F32) + jnp.dot(ah, bl, preferred_element_type=F32)
            + jnp.dot(al, bh, preferred_element_type=F32))


def _softplus(x):
    return jnp.maximum(x, 0.0) + jnp.log1p(jnp.exp(-jnp.abs(x)))


def _sigmoid(x):
    return 1.0 / (1.0 + jnp.exp(-x))


def _silu(x):
    return x * _sigmoid(x)


def _gelu_tanh(x):
    return 0.5 * x * (1.0 + jnp.tanh(math.sqrt(2.0 / math.pi) * (x + 0.044715 * (x * x * x))))


def _layer_norm(x, g, b):
    mu = jnp.mean(x, axis=-1, keepdims=True)
    xc = x - mu
    var = jnp.mean(xc * xc, axis=-1, keepdims=True)
    return xc * lax.rsqrt(var + EPS) * g + b


def _iota(shape, axis):
    return lax.broadcasted_iota(jnp.int32, shape, axis)


def _log2(n):
    l = int(math.log2(n))
    assert (1 << l) == n
    return l


def _causal_conv(pre, hist, w_ref, width, tt):
    rows = pre.shape[0]
    y = pre * w_ref[width - 1:width, :]
    for d in range(1, width):
        rolled = pltpu.roll(pre, d, axis=0)
        if tt == HIST:
            i_in = _iota(pre.shape, 0) & (HIST - 1)
            from_hist = pltpu.roll(hist, (d - HIST) % rows, axis=0)
            shifted = jnp.where(i_in < d, from_hist, rolled)
        else:
            i_in = _iota((HIST, pre.shape[1]), 0)
            first = jnp.where(i_in < d, pltpu.roll(hist, d, axis=0), rolled[0:HIST])
            shifted = jnp.concatenate([first, rolled[HIST:]], axis=0)
        y = y + shifted * w_ref[width - 1 - d:width - d, :]
    return y


def _stack_rows(x, n_sb, c, pieces):
    del x
    if n_sb == 1:
        return jnp.concatenate([pieces[0], pieces[1]], axis=0)
    out = []
    for s in range(n_sb):
        for a in range(2):
            out.append(pieces[a][s * c:(s + 1) * c])
    return jnp.concatenate(out, axis=0)


def _unstack_rows(o, n_sb, c):
    lo = _iota((c, LANES), 1) < HALF
    out = []
    for s in range(n_sb):
        base = s * 2 * c
        out.append(jnp.where(lo, o[base:base + c], o[base + c:base + 2 * c]))
    return out[0] if n_sb == 1 else jnp.concatenate(out, axis=0)


def _col_bcast_stack(vals, lane0, n_sb, c):
    pieces = [jnp.broadcast_to(vals[:, lane0 + a:lane0 + a + 1], (GROUP_ROWS, LANES)) for a in range(2)]
    return _stack_rows(None, n_sb, c, pieces)


def _dup_values(v):
    lo = _iota(v.shape, 1) < HALF
    vr = pltpu.roll(v, HALF, axis=1)
    return jnp.where(lo, v, vr), jnp.where(lo, vr, v)


def _mixer_kernel(
        x_ref, hista_ref, sd_ref, histb_ref, h0_ref, sg_ref,
        w_in_ref, conv_a_w_ref, alog_ref, dtb_ref, norm_a_ref, conv_b_w_ref, conv_b_b_ref,
        wr_ref, br_ref, wi_ref, bi_ref, lam_ref, w2_ref, b2_ref, norm_c_ref, w_out_ref, ln_g_ref, ln_b_ref,
        x1_ref, taila_ref, sd_out_ref, tailb_ref, h_out_ref, sg_out_ref,
        sd_s, sg_s, hista_s, histb_s, h_s, heads_s,
        qa_s, ka_s, va_s, beta_s, gcum_s, gtot_s,
        qc_s, kc_s, vc_s, qb_s, kdec_s, btot_s, qt_s, kt_s,
        *, n_seq, tt, c):
    t_idx = pl.program_id(1)
    n_t = pl.num_programs(1)
    rows = n_seq * tt
    n_groups = rows // GROUP_ROWS
    n_sb = GROUP_ROWS // c
    lc = _log2(c)
    n_levels = lc

    @pl.when(t_idx == 0)
    def _():
        for s in range(n_seq):
            for p in range(N_PAIRS):
                sd = sd_ref[s, p]
                sd_s[s, p] = jnp.concatenate([sd, sd], axis=-1)
                r0 = GROUP_ROWS * (p % 2)
                sg = sg_ref[s, 2 * DK_C * p:2 * DK_C * (p + 1), :]
                sg = jnp.concatenate([sg, sg], axis=-1)
                zeros = jnp.zeros((GROUP_ROWS, LANES), F32)
                sg_s[s, p] = jnp.concatenate([sg, zeros] if r0 == 0 else [zeros, sg], axis=0)
        hista_s[...] = hista_ref[...].reshape(n_seq * HIST, CONV_A_WIDTH)
        histb_s[...] = histb_ref[...].reshape(n_seq * HIST, W_B)
        h_s[...] = h0_ref[...].reshape(n_seq, W_B)

    x = x_ref[...].reshape(rows, D_MODEL)
    proj = _mm(x, w_in_ref[...])

    r_i = _iota((rows, rows), 0)
    r_j = _iota((rows, rows), 1)
    same_chunk = (r_i >> lc) == (r_j >> lc)
    m_cum = jnp.where(same_chunk & (r_j <= r_i), 1.0, 0.0).astype(BF16)
    m_tot = jnp.where(same_chunk, 1.0, 0.0).astype(BF16)

    pre_a = proj[:, OFF_QA:OFF_QA + CONV_A_WIDTH]
    qkv = _silu(_causal_conv(pre_a, hista_s[...], conv_a_w_ref, CONV_A, tt))
    if tt == HIST:
        hista_s[...] = pre_a
    else:
        hista_s[...] = pre_a[rows - HIST:rows]
    e_i = _iota((QK_A, QK_A), 0) >> _log2(DK_A)
    e_j = _iota((QK_A, QK_A), 1) >> _log2(DK_A)
    head_ones = jnp.where(e_i == e_j, 1.0, 0.0).astype(BF16)
    qa = qkv[:, 0:QK_A]
    ka = qkv[:, QK_A:2 * QK_A]
    qa = qa * lax.rsqrt(_mm_mask_rhs(qa * qa, head_ones) + EPS) * (DK_A ** -0.5)
    ka = ka * lax.rsqrt(_mm_mask_rhs(ka * ka, head_ones) + EPS)
    qa_s[...] = qa
    ka_s[...] = ka
    va_s[...] = qkv[:, 2 * QK_A:]

    small = proj[:, OFF_SMALL:OFF_SMALL + LANES]
    beta_s[...] = _sigmoid(small)
    g_full = -jnp.exp(alog_ref[...]) * _softplus(small + dtb_ref[...])
    gcum_s[...] = _mm_mask_lhs(m_cum, g_full)
    gtot_s[...] = _mm_mask_lhs(m_tot, g_full)

    pre_b = proj[:, OFF_XB:OFF_XB + W_B]
    xc = _causal_conv(pre_b, histb_s[...], conv_b_w_ref, CONV_B, tt) + conv_b_b_ref[...]
    if tt == HIST:
        histb_s[...] = pre_b
    else:
        histb_s[...] = pre_b[rows - HIST:rows]
    gate_r = _sigmoid(_mm(xc, wr_ref[...]) + br_ref[...])
    gate_i = _sigmoid(_mm(xc, wi_ref[...]) + bi_ref[...])
    log_a = -LRU_C * gate_r * _softplus(-lam_ref[...])
    a_t = jnp.exp(log_a)
    b_t = jnp.sqrt(-jnp.tanh(log_a) * (a_t * a_t + 1.0)) * (gate_i * xc)
    i_seq = _iota((rows, W_B), 0) & (tt - 1)
    if n_seq == 1:
        h_prev = jnp.broadcast_to(h_s[...], (rows, W_B))
    else:
        h_prev = jnp.concatenate([jnp.broadcast_to(h_s[s:s + 1, :], (tt, W_B)) for s in range(n_seq)], axis=0)
    b_t = b_t + jnp.where(i_seq == 0, a_t * h_prev, 0.0)
    d = 1
    while d < tt:
        a_sh = pltpu.roll(a_t, d, axis=0)
        b_sh = pltpu.roll(b_t, d, axis=0)
        ok = i_seq >= d
        b_t = jnp.where(ok, a_t * b_sh + b_t, b_t)
        a_t = jnp.where(ok, a_t * a_sh, a_t)
        d *= 2
    if n_seq == 1:
        h_s[...] = b_t[rows - 1:rows]
    else:
        h_s[...] = jnp.concatenate([b_t[(s + 1) * tt - 1:(s + 1) * tt] for s in range(n_seq)], axis=0)
    heads_s[:, W_A:W_A + W_B] = b_t * _gelu_tanh(proj[:, OFF_GB:OFF_GB + W_B])

    qc = proj[:, OFF_QC:OFF_QC + QKC_PAD] * (DK_C ** -0.5)
    kc = proj[:, OFF_KC:OFF_KC + QKC_PAD]
    logf = -_softplus(-(_mm(small, w2_ref[...]) + b2_ref[...])) * (1.0 / GLA_TAU)
    b_cum = _mm_mask_lhs(m_cum, logf)
    b_tot = _mm_mask_lhs(m_tot, logf)
    qc_s[...] = qc
    kc_s[...] = kc
    vc_s[...] = proj[:, OFF_VC:OFF_VC + W_C]
    qb_s[...] = qc * jnp.exp(b_cum)
    kdec_s[...] = kc * jnp.exp(b_tot - b_cum)
    btot_s[...] = b_tot
    for lvl in range(n_levels):
        lh = lc - 1 - lvl
        hi_i = (r_i >> lh) & 1
        start_i = (r_i >> lh) << lh
        next_i = ((r_i >> lh) + 1) << lh
        m_q = jnp.where((hi_i == 1) & (r_j >= start_i) & (r_j <= r_i), 1.0, 0.0).astype(BF16)
        m_k = jnp.where((hi_i == 0) & (r_j > r_i) & (r_j < next_i), 1.0, 0.0).astype(BF16)
        qt_s[lvl] = qc * jnp.exp(_mm_mask_lhs(m_q, logf))
        kt_s[lvl] = kc * jnp.exp(_mm_mask_lhs(m_k, logf))

    s_i = _iota((STACK, STACK), 0)
    s_j = _iota((STACK, STACK), 1)
    same_blk = (s_i >> lc) == (s_j >> lc)
    m_incl = same_blk & (s_j <= s_i)
    m_strict = same_blk & (s_j < s_i)
    eye = jnp.where(s_i == s_j, 1.0, 0.0)
    lane = _iota((GROUP_ROWS, LANES), 1)

    def group_body(gi, carry):
        r0 = pl.multiple_of(gi * GROUP_ROWS, GROUP_ROWS)
        rs = pl.ds(r0, GROUP_ROWS)
        beta_g = beta_s[rs, :]
        gcum_g = gcum_s[rs, :]
        gtot_g = gtot_s[rs, :]
        for p in range(N_PAIRS):
            sl = pl.ds(p * LANES, LANES)
            m0 = lane < HALF
            q_p = qa_s[rs, sl]
            k_p = ka_s[rs, sl]
            v0, v1 = _dup_values(va_s[rs, sl])
            q_st = _stack_rows(None, n_sb, c, [jnp.where(m0, q_p, 0.0), jnp.where(m0, 0.0, q_p)])
            k_st = _stack_rows(None, n_sb, c, [jnp.where(m0, k_p, 0.0), jnp.where(m0, 0.0, k_p)])
            v_st = _stack_rows(None, n_sb, c, [v0, v1])
            beta_c = _col_bcast_stack(beta_g, LANE_BETA + 2 * p, n_sb, c)
            g_c = _col_bcast_stack(gcum_g, LANE_DECAY + 2 * p, n_sb, c)
            gt_c = _col_bcast_stack(gtot_g, LANE_DECAY + 2 * p, n_sb, c)
            dec = jnp.exp(jnp.where(m_incl, g_c - g_c.T, -1e30))
            kk = _mm_nt(k_st, k_st)
            qk = _mm_nt(q_st, k_st)
            n_mat = -(beta_c * kk * jnp.where(m_strict, dec, 0.0))
            t_mat = eye + n_mat
            pw = n_mat
            for _ in range(lc - 1):
                pw = _mm_hi(pw, pw)
                t_mat = t_mat + _mm_hi(t_mat, pw)
            eg = jnp.exp(g_c)
            u = _mm_hi(t_mat, beta_c * v_st)
            wk = _mm_hi(t_mat, beta_c * eg * k_st)
            qkd = qk * dec
            k_dec = k_st * jnp.exp(gt_c - g_c)
            w_parts, qs_parts = [], []
            for sb in range(n_sb):
                seq = 0 if n_seq == 1 else gi * n_sb + sb
                blk = slice(sb * 2 * c, (sb + 1) * 2 * c)
                s_old = sd_s[seq, p]
                w_sb = u[blk] - _mm(wk[blk], s_old)
                w_parts.append(w_sb)
                qs_parts.append(_mm(q_st[blk], s_old))
                decay_rows = jnp.concatenate(
                    [jnp.broadcast_to(gt_c[sb * 2 * c + a * c:sb * 2 * c + a * c + 1, :], (GROUP_ROWS, LANES))
                     for a in range(2)], axis=0)
                sd_s[seq, p] = jnp.exp(decay_rows) * s_old + _mm(k_dec[blk].T, w_sb)
            w_all = w_parts[0] if n_sb == 1 else jnp.concatenate(w_parts, axis=0)
            qs_all = qs_parts[0] if n_sb == 1 else jnp.concatenate(qs_parts, axis=0)
            o_st = eg * qs_all + _mm(qkd, w_all)
            o_st = o_st * lax.rsqrt(jnp.mean(o_st * o_st, axis=-1, keepdims=True) + EPS) * norm_a_ref[...]
            heads_s[rs, sl] = _unstack_rows(o_st, n_sb, c)

            slab = pl.ds((p // 2) * LANES, LANES)
            l0 = 2 * DK_C * (p % 2)
            mh0 = (lane >= l0) & (lane < l0 + DK_C)
            mh1 = (lane >= l0 + DK_C) & (lane < l0 + 2 * DK_C)

            def stack_c(ref, idx=None):
                val = ref[rs, slab] if idx is None else ref[idx, rs, slab]
                return _stack_rows(None, n_sb, c, [jnp.where(mh0, val, 0.0), jnp.where(mh1, val, 0.0)])

            att = jnp.where(s_i == s_j, _mm_nt(stack_c(qc_s), stack_c(kc_s)), 0.0)
            for lvl in range(n_levels):
                lh = lc - 1 - lvl
                valid = (((s_i >> (lh + 1)) == (s_j >> (lh + 1)))
                         & (((s_i >> lh) & 1) == 1) & (((s_j >> lh) & 1) == 0))
                att = att + jnp.where(valid, _mm_nt(stack_c(qt_s, lvl), stack_c(kt_s, lvl)), 0.0)
            vc0, vc1 = _dup_values(vc_s[rs, sl])
            vc_st = _stack_rows(None, n_sb, c, [vc0, vc1])
            qb_st = stack_c(qb_s)
            kd_st = stack_c(kdec_s)
            btot_g = btot_s[rs, slab]
            oi_parts = []
            for sb in range(n_sb):
                seq = 0 if n_seq == 1 else gi * n_sb + sb
                blk = slice(sb * 2 * c, (sb + 1) * 2 * c)
                s_old = sg_s[seq, p]
                oi_parts.append(_mm(qb_st[blk], s_old))
                decay_col = jnp.broadcast_to(btot_g[sb * c:sb * c + 1, :], (LANES, LANES)).T
                sg_s[seq, p] = jnp.exp(decay_col) * s_old + _mm(kd_st[blk].T, vc_st[blk])
            oi_all = oi_parts[0] if n_sb == 1 else jnp.concatenate(oi_parts, axis=0)
            oc_st = oi_all + _mm(att, vc_st)
            oc_st = oc_st * lax.rsqrt(jnp.mean(oc_st * oc_st, axis=-1, keepdims=True) + EPS) * norm_c_ref[...]
            heads_s[rs, pl.ds(W_A + W_B + p * LANES, LANES)] = _unstack_rows(oc_st, n_sb, c)
        return carry

    if n_groups == 1:
        group_body(0, 0)
    else:
        lax.fori_loop(0, n_groups, group_body, 0)

    za = proj[:, OFF_ZA:OFF_ZA + W_A]
    zc = proj[:, OFF_ZC:OFF_ZC + W_C]
    heads_s[:, 0:W_A] = heads_s[:, 0:W_A] * _silu(za)
    heads_s[:, W_A + W_B:] = heads_s[:, W_A + W_B:] * _silu(zc)
    y = ALPHA * x + _mm(heads_s[...], w_out_ref[...])
    x1_ref[...] = _layer_norm(y, ln_g_ref[...], ln_b_ref[...]).reshape(n_seq, tt, D_MODEL)

    @pl.when(t_idx == n_t - 1)
    def _():
        taila_ref[...] = hista_s[...].reshape(n_seq, HIST, CONV_A_WIDTH)
        tailb_ref[...] = histb_s[...].reshape(n_seq, HIST, W_B)
        h_out_ref[...] = h_s[...].reshape(n_seq, 1, W_B)
        for s in range(n_seq):
            for p in range(N_PAIRS):
                sd_out_ref[s, p] = sd_s[s, p][:, 0:HALF]
                r0 = GROUP_ROWS * (p % 2)
                sg_out_ref[s, 2 * DK_C * p:2 * DK_C * (p + 1), :] = sg_s[s, p][r0:r0 + GROUP_ROWS, 0:HALF]


def _const_spec(shape):
    nd = len(shape)
    return pl.BlockSpec(shape, lambda b, t: (0,) * nd)


def _mixer_call(x, hista, sd, histb, h0, sg, w, *, n_seq, tt, c, depth):
    nb, t_total, _ = x.shape
    assert nb % n_seq == 0 and t_total % tt == 0
    assert (n_seq == 1 and tt % GROUP_ROWS == 0 and c == GROUP_ROWS) or (tt == HIST and c == HIST)
    assert depth == DEPTH
    rows = n_seq * tt
    assert rows % GROUP_ROWS == 0
    grid = (nb // n_seq, t_total // tt)
    n_levels = _log2(c)

    def seq_spec(shape_tail):
        nd = len(shape_tail)
        return pl.BlockSpec((n_seq,) + shape_tail, lambda b, t: (b,) + (0,) * nd)

    weights = (w['w_in'], w['conv_a_w'], w['alog'], w['dtb'], w['norm_a'], w['conv_b_w'], w['conv_b_b'],
               w['wr'], w['br'], w['wi'], w['bi'], w['lam'], w['w2'], w['b2'], w['norm_c'], w['w_out'],
               w['ln1_g'], w['ln1_b'])
    in_specs = [
        pl.BlockSpec((n_seq, tt, D_MODEL), lambda b, t: (b, t, 0)),
        seq_spec((HIST, CONV_A_WIDTH)),
        seq_spec((N_PAIRS, 2 * DK_A, DV_A)),
        seq_spec((HIST, W_B)),
        seq_spec((1, W_B)),
        seq_spec((QK_C, DV_C)),
    ] + [_const_spec(a.shape) for a in weights]
    out_shape = (
        jax.ShapeDtypeStruct((nb, t_total, D_MODEL), F32),
        jax.ShapeDtypeStruct((nb, HIST, CONV_A_WIDTH), F32),
        jax.ShapeDtypeStruct((nb, N_PAIRS, 2 * DK_A, DV_A), F32),
        jax.ShapeDtypeStruct((nb, HIST, W_B), F32),
        jax.ShapeDtypeStruct((nb, 1, W_B), F32),
        jax.ShapeDtypeStruct((nb, QK_C, DV_C), F32),
    )
    out_specs = (
        pl.BlockSpec((n_seq, tt, D_MODEL), lambda b, t: (b, t, 0)),
        seq_spec((HIST, CONV_A_WIDTH)),
        seq_spec((N_PAIRS, 2 * DK_A, DV_A)),
        seq_spec((HIST, W_B)),
        seq_spec((1, W_B)),
        seq_spec((QK_C, DV_C)),
    )
    scratch = [
        pltpu.VMEM((n_seq, N_PAIRS, STACK, LANES), F32),
        pltpu.VMEM((n_seq, N_PAIRS, LANES, LANES), F32),
        pltpu.VMEM((n_seq * HIST, CONV_A_WIDTH), F32),
        pltpu.VMEM((n_seq * HIST, W_B), F32),
        pltpu.VMEM((n_seq, W_B), F32),
        pltpu.VMEM((rows, D_MIX), F32),
        pltpu.VMEM((rows, QK_A), F32),
        pltpu.VMEM((rows, QK_A), F32),
        pltpu.VMEM((rows, W_A), F32),
        pltpu.VMEM((rows, LANES), F32),
        pltpu.VMEM((rows, LANES), F32),
        pltpu.VMEM((rows, LANES), F32),
        pltpu.VMEM((rows, QKC_PAD), F32),
        pltpu.VMEM((rows, QKC_PAD), F32),
        pltpu.VMEM((rows, W_C), F32),
        pltpu.VMEM((rows, QKC_PAD), F32),
        pltpu.VMEM((rows, QKC_PAD), F32),
        pltpu.VMEM((rows, QKC_PAD), F32),
        pltpu.VMEM((n_levels, rows, QKC_PAD), F32),
        pltpu.VMEM((n_levels, rows, QKC_PAD), F32),
    ]
    kern = functools.partial(_mixer_kernel, n_seq=n_seq, tt=tt, c=c)
    return pl.pallas_call(
        kern,
        grid=grid,
        in_specs=in_specs,
        out_specs=out_specs,
        out_shape=out_shape,
        scratch_shapes=scratch,
        compiler_params=pltpu.CompilerParams(
            dimension_semantics=("arbitrary", "arbitrary"), vmem_limit_bytes=VMEM_LIMIT_BYTES),
        name=f"mixer_nseq{n_seq}_tt{tt}",
    )(x, hista, sd, histb, h0, sg, *weights)


def _ffn_kernel(x_ref, hist_ref, wg_ref, wv_ref, cw_ref, cb_ref, wd_ref, ln_g_ref, ln_b_ref,
                y_ref, tail_ref, hist_s, acc_s, *, n_seq, tt):
    t_idx = pl.program_id(1)
    n_t = pl.num_programs(1)
    rows = n_seq * tt

    @pl.when(t_idx == 0)
    def _():
        hist_s[...] = hist_ref[...]

    x = x_ref[...].reshape(rows, D_MODEL)
    xb = x.astype(BF16)
    acc_s[...] = jnp.zeros((rows, D_MODEL), F32)

    def chunk_body(j, carry):
        gate = jnp.dot(xb, wg_ref[j], preferred_element_type=F32)
        val = jnp.dot(xb, wv_ref[j], preferred_element_type=F32)
        conv = _causal_conv(gate, hist_s[j], cw_ref.at[j], CONV_F, tt) + cb_ref[j]
        if tt == HIST:
            hist_s[j] = gate
        else:
            hist_s[j] = gate[rows - HIST:rows]
        h = _gelu_tanh(conv) * val
        acc_s[...] += _mm(h, wd_ref[j])
        return carry

    lax.fori_loop(0, N_FF_CHUNKS, chunk_body, 0)
    y = ALPHA * x + acc_s[...]
    y_ref[...] = _layer_norm(y, ln_g_ref[...], ln_b_ref[...]).reshape(n_seq, tt, D_MODEL)

    @pl.when(t_idx == n_t - 1)
    def _():
        tail_ref[...] = hist_s[...]


def _ffn_call(x, hist, w, *, n_seq, tt):
    nb, t_total, _ = x.shape
    rows = n_seq * tt
    grid = (nb // n_seq, t_total // tt)
    weights = (w['wg'], w['wv'], w['cw'], w['cb'], w['wd'], w['ln2_g'], w['ln2_b'])
    hist_spec = pl.BlockSpec((N_FF_CHUNKS, n_seq * HIST, FF_CHUNK), lambda b, t: (0, b, 0))
    in_specs = [pl.BlockSpec((n_seq, tt, D_MODEL), lambda b, t: (b, t, 0)), hist_spec] + [
        _const_spec(a.shape) for a in weights]
    out_shape = (jax.ShapeDtypeStruct((nb, t_total, D_MODEL), F32),
                 jax.ShapeDtypeStruct((N_FF_CHUNKS, nb * HIST, FF_CHUNK), F32))
    out_specs = (pl.BlockSpec((n_seq, tt, D_MODEL), lambda b, t: (b, t, 0)), hist_spec)
    scratch = [pltpu.VMEM((N_FF_CHUNKS, n_seq * HIST, FF_CHUNK), F32), pltpu.VMEM((rows, D_MODEL), F32)]
    kern = functools.partial(_ffn_kernel, n_seq=n_seq, tt=tt)
    return pl.pallas_call(
        kern,
        grid=grid,
        in_specs=in_specs,
        out_specs=out_specs,
        out_shape=out_shape,
        scratch_shapes=scratch,
        compiler_params=pltpu.CompilerParams(
            dimension_semantics=("arbitrary", "arbitrary"), vmem_limit_bytes=VMEM_LIMIT_BYTES),
        name=f"ffn_nseq{n_seq}_tt{tt}",
    )(x, hist, *weights)


def _lane_row(vec, lane0, width=LANES):
    out = jnp.zeros((1, width), F32)
    return out.at[0, lane0:lane0 + vec.shape[0]].set(vec.astype(F32))


def _prep_layer_weights(w_in, conv_a_w, a_log, dt_bias, norm_a_w, conv_b_w, conv_b_b, lru_w_r, lru_b_r, lru_w_i,
                        lru_b_i, lru_lambda, gla_w2, gla_b2, norm_c_w, w_out, ln1_g, ln1_b, ffn_w_up, ffn_conv_w,
                        ffn_conv_b, ffn_w_down, ln2_g, ln2_b):
    pts = [0]
    for s in (QK_A, QK_A, W_A, W_A, H_A, H_A, W_B, W_B, QK_C, QK_C, W_C, W_C, GLA_RANK):
        pts.append(pts[-1] + s)
    (qa, ka, va, za, ba, aa, xb, gb, qc, kc, vc, zc, lc) = [w_in[:, pts[i]:pts[i + 1]] for i in range(13)]
    d = w_in.shape[0]
    zpad = jnp.zeros((d, QKC_PAD - QK_C), w_in.dtype)
    small = jnp.zeros((d, LANES), w_in.dtype)
    small = small.at[:, LANE_BETA:LANE_BETA + H_A].set(ba)
    small = small.at[:, LANE_DECAY:LANE_DECAY + H_A].set(aa)
    small = small.at[:, LANE_LOWRANK:LANE_LOWRANK + GLA_RANK].set(lc)
    w_in_p = jnp.concatenate([qa, ka, va, za, xb, gb, qc, zpad, kc, zpad, vc, zc, small], axis=1).astype(BF16)
    assert w_in_p.shape[1] == D_IN_PAD

    def block_diag(wb):
        out = jnp.zeros((W_B, W_B), F32)
        for n in range(LRU_BLOCKS):
            out = out.at[n * LRU_BLOCK:(n + 1) * LRU_BLOCK, n * LRU_BLOCK:(n + 1) * LRU_BLOCK].set(wb[n])
        return out.astype(BF16)

    w2 = jnp.zeros((LANES, QKC_PAD), F32).at[LANE_LOWRANK:LANE_LOWRANK + GLA_RANK, 0:QK_C].set(gla_w2).astype(BF16)
    row = lambda v: v.astype(F32).reshape(1, -1)
    dup = lambda v: jnp.concatenate([v, v]).astype(F32).reshape(1, LANES)
    gate_w = ffn_w_up[:, :D_FF].reshape(d, N_FF_CHUNKS, FF_CHUNK).transpose(1, 0, 2).astype(BF16)
    val_w = ffn_w_up[:, D_FF:].reshape(d, N_FF_CHUNKS, FF_CHUNK).transpose(1, 0, 2).astype(BF16)
    return {
        'w_in': w_in_p, 'conv_a_w': conv_a_w.astype(F32),
        'alog': _lane_row(a_log, LANE_DECAY), 'dtb': _lane_row(dt_bias, LANE_DECAY), 'norm_a': dup(norm_a_w),
        'conv_b_w': conv_b_w.astype(F32), 'conv_b_b': row(conv_b_b),
        'wr': block_diag(lru_w_r), 'br': row(lru_b_r), 'wi': block_diag(lru_w_i), 'bi': row(lru_b_i),
        'lam': row(lru_lambda), 'w2': w2, 'b2': _lane_row(gla_b2, 0, QKC_PAD), 'norm_c': dup(norm_c_w),
        'w_out': w_out.astype(BF16), 'ln1_g': row(ln1_g), 'ln1_b': row(ln1_b),
        'wg': gate_w, 'wv': val_w,
        'cw': ffn_conv_w.astype(F32).reshape(CONV_F, N_FF_CHUNKS, FF_CHUNK).transpose(1, 0, 2),
        'cb': ffn_conv_b.astype(F32).reshape(N_FF_CHUNKS, 1, FF_CHUNK),
        'wd': ffn_w_down.reshape(N_FF_CHUNKS, FF_CHUNK, d).astype(BF16),
        'ln2_g': row(ln2_g), 'ln2_b': row(ln2_b),
    }


def _pad_hist(state):
    b, k, ch = state.shape
    return jnp.concatenate([jnp.zeros((b, HIST - k, ch), F32), state.astype(F32)], axis=1)


def _trunk(x, states, layer_weights, *, n_seq, tt, c):
    st_dconv, st_delta, st_lconv, st_lru, st_gla, st_fconv = states
    depth = len(layer_weights)
    nb = x.shape[0]
    outs = [[] for _ in range(6)]
    for l in range(depth):
        w = layer_weights[l]
        hista = _pad_hist(st_dconv[l])
        sd = st_delta[l].astype(F32).reshape(nb, N_PAIRS, 2 * DK_A, DV_A)
        histb = _pad_hist(st_lconv[l])
        h0 = st_lru[l].astype(F32).reshape(nb, 1, W_B)
        sg = st_gla[l].astype(F32).reshape(nb, QK_C, DV_C)
        histf = _pad_hist(st_fconv[l]).reshape(nb * HIST, N_FF_CHUNKS, FF_CHUNK).transpose(1, 0, 2)
        x1, taila, sd_o, tailb, h_o, sg_o = _mixer_call(x, hista, sd, histb, h0, sg, w,
                                                         n_seq=n_seq, tt=tt, c=c, depth=depth)
        x, tailf = _ffn_call(x1, histf, w, n_seq=n_seq, tt=tt)
        outs[0].append(taila[:, HIST - (CONV_A - 1):, :])
        outs[1].append(sd_o.reshape(nb, H_A, DK_A, DV_A))
        outs[2].append(tailb[:, HIST - (CONV_B - 1):, :])
        outs[3].append(h_o.reshape(nb, W_B))
        outs[4].append(sg_o.reshape(nb, H_C, DK_C, DV_C))
        tailf = tailf.transpose(1, 0, 2).reshape(nb, HIST, D_FF)
        outs[5].append(tailf[:, HIST - (CONV_F - 1):, :])
    return x, tuple(jnp.stack(o) for o in outs)


PROMPT_TT = 256
SAMPLE_NSEQ = 16


def kernel(x_prompt, x_sample, state_delta_conv, state_delta, state_lru_conv, state_lru, state_gla, state_ffn_conv,
           w_in, conv_a_w, a_log, dt_bias, norm_a_w, conv_b_w, conv_b_b, lru_w_r, lru_b_r, lru_w_i, lru_b_i,
           lru_lambda, gla_w2, gla_b2, norm_c_w, w_out, ln1_g, ln1_b, ffn_w_up, ffn_conv_w, ffn_conv_b, ffn_w_down,
           ln2_g, ln2_b):
    weights = (w_in, conv_a_w, a_log, dt_bias, norm_a_w, conv_b_w, conv_b_b, lru_w_r, lru_b_r, lru_w_i, lru_b_i,
               lru_lambda, gla_w2, gla_b2, norm_c_w, w_out, ln1_g, ln1_b, ffn_w_up, ffn_conv_w, ffn_conv_b,
               ffn_w_down, ln2_g, ln2_b)
    depth = w_in.shape[0]
    layer_weights = [_prep_layer_weights(*(w[l] for w in weights)) for l in range(depth)]
    sample_states = (state_delta_conv, state_delta, state_lru_conv, state_lru, state_gla, state_ffn_conv)
    n_prompt = x_prompt.shape[0]
    prompt_states = tuple(jnp.zeros((depth, n_prompt) + s.shape[2:], F32) for s in sample_states)
    t_p = x_prompt.shape[1]
    t_s = x_sample.shape[1]
    y_p, p_st = _trunk(x_prompt, prompt_states, layer_weights, n_seq=1, tt=min(PROMPT_TT, t_p), c=CHUNK)
    y_s, s_st = _trunk(x_sample, sample_states, layer_weights,
                       n_seq=min(SAMPLE_NSEQ, x_sample.shape[0]), tt=t_s, c=t_s)
    return (y_p, y_s) + p_st + s_st
```

```python
import functools
import math

import jax
import jax.numpy as jnp
from jax import lax
from jax.experimental import pallas as pl
from jax.experimental.pallas import tpu as pltpu

F32 = jnp.float32
BF16 = jnp.bfloat16

D_MODEL = 1024
H_A, DK_A, DV_A = 6, 64, 64
QK_A = H_A * DK_A
W_A = H_A * DV_A
CONV_A = 4
W_B = 256
LRU_BLOCKS = 4
LRU_BLOCK = W_B // LRU_BLOCKS
CONV_B = 4
LRU_C = 8.0
H_C, DK_C, DV_C = 6, 32, 64
QK_C = H_C * DK_C
W_C = H_C * DV_C
GLA_RANK = 16
GLA_TAU = 16.0
D_MIX = W_A + W_B + W_C
CHUNK = 64
D_FF = 2816
CONV_F = 3
EPS = 1e-6
CONV_A_WIDTH = 2 * QK_A + W_A
DEPTH = 2
ALPHA = (2.0 * DEPTH) ** 0.25

LANES = 128
SUBLANES = 8
HALF = LANES // 2
GROUP_ROWS = 64
STACK = 2 * GROUP_ROWS
N_PAIRS = 3
HIST = SUBLANES
FF_CHUNK = 256
N_FF_CHUNKS = D_FF // FF_CHUNK
VMEM_LIMIT_BYTES = 56 * 1024 * 1024

OFF_QA = 0
OFF_KA = OFF_QA + QK_A
OFF_VA = OFF_KA + QK_A
OFF_ZA = OFF_VA + W_A
OFF_XB = OFF_ZA + W_A
OFF_GB = OFF_XB + W_B
OFF_QC = OFF_GB + W_B
QKC_PAD = 2 * LANES
OFF_KC = OFF_QC + QKC_PAD
OFF_VC = OFF_KC + QKC_PAD
OFF_ZC = OFF_VC + W_C
OFF_SMALL = OFF_ZC + W_C
D_IN_PAD = OFF_SMALL + LANES
LANE_BETA = 0
LANE_DECAY = 8
LANE_LOWRANK = 16


def _mm(a, b):
    return jnp.dot(a.astype(BF16), b.astype(BF16), preferred_element_type=F32)


def _mm_nt(a, b):
    return lax.dot_general(a.astype(BF16), b.astype(BF16), (((1,), (1,)), ((), ())), preferred_element_type=F32)


def _split(a, n):
    parts = []
    r = a
    for i in range(n):
        p = r.astype(BF16)
        parts.append(p)
        if i + 1 < n:
            r = r - p.astype(F32)
    return parts


def _mm_mask_lhs(mask_bf16, a, n=2):
    out = None
    for p in _split(a, n):
        t = jnp.dot(mask_bf16, p, preferred_element_type=F32)
        out = t if out is None else out + t
    return out


def _mm_mask_rhs(a, mask_bf16, n=2):
    out = None
    for p in _split(a, n):
        t = jnp.dot(p, mask_bf16, preferred_element_type=F32)
        out = t if out is None else out + t
    return out


def _mm_hi(a, b):
    ah, al = _split(a, 2)
    bh, bl = _split(b, 2)
    return (jnp.dot(ah, bh, preferred_element_type=F32) + jnp.dot(ah, bl, preferred_element_type=F32)
            + jnp.dot(al, bh, preferred_element_type=F32))


def _softplus(x):
    return jnp.maximum(x, 0.0) + jnp.log1p(jnp.exp(-jnp.abs(x)))


def _sigmoid(x):
    return 1.0 / (1.0 + jnp.exp(-x))


def _silu(x):
    return x * _sigmoid(x)


def _gelu_tanh(x):
    return 0.5 * x * (1.0 + jnp.tanh(math.sqrt(2.0 / math.pi) * (x + 0.044715 * (x * x * x))))


def _layer_norm(x, g, b):
    mu = jnp.mean(x, axis=-1, keepdims=True)
    xc = x - mu
    var = jnp.mean(xc * xc, axis=-1, keepdims=True)
    return xc * lax.rsqrt(var + EPS) * g + b


def _iota(shape, axis):
    return lax.broadcasted_iota(jnp.int32, shape, axis)


def _log2(n):
    l = int(math.log2(n))
    assert (1 << l) == n
    return l


def _causal_conv(pre, hist, w_ref, width, tt):
    rows = pre.shape[0]
    y = pre * w_ref[width - 1:width, :]
    for d in range(1, width):
        rolled = pltpu.roll(pre, d, axis=0)
        if tt == HIST:
            i_in = _iota(pre.shape, 0) & (HIST - 1)
            from_hist = pltpu.roll(hist, (d - HIST) % rows, axis=0)
            shifted = jnp.where(i_in < d, from_hist, rolled)
        else:
            i_in = _iota((HIST, pre.shape[1]), 0)
            first = jnp.where(i_in < d, pltpu.roll(hist, d, axis=0), rolled[0:HIST])
            shifted = jnp.concatenate([first, rolled[HIST:]], axis=0)
        y = y + shifted * w_ref[width - 1 - d:width - d, :]
    return y


def _stack_rows(x, n_sb, c, pieces):
    del x
    if n_sb == 1:
        return jnp.concatenate([pieces[0], pieces[1]], axis=0)
    out = []
    for s in range(n_sb):
        for a in range(2):
            out.append(pieces[a][s * c:(s + 1) * c])
    return jnp.concatenate(out, axis=0)


def _unstack_rows(o, n_sb, c):
    lo = _iota((c, LANES), 1) < HALF
    out = []
    for s in range(n_sb):
        base = s * 2 * c
        out.append(jnp.where(lo, o[base:base + c], o[base + c:base + 2 * c]))
    return out[0] if n_sb == 1 else jnp.concatenate(out, axis=0)


def _col_bcast_stack(vals, lane0, n_sb, c):
    pieces = [jnp.broadcast_to(vals[:, lane0 + a:lane0 + a + 1], (GROUP_ROWS, LANES)) for a in range(2)]
    return _stack_rows(None, n_sb, c, pieces)


def _dup_values(v):
    lo = _iota(v.shape, 1) < HALF
    vr = pltpu.roll(v, HALF, axis=1)
    return jnp.where(lo, v, vr), jnp.where(lo, vr, v)


def _mixer_kernel(
        x_ref, hista_ref, sd_ref, histb_ref, h0_ref, sg_ref,
        w_in_ref, conv_a_w_ref, alog_ref, dtb_ref, norm_a_ref, conv_b_w_ref, conv_b_b_ref,
        wr_ref, br_ref, wi_ref, bi_ref, lam_ref, w2_ref, b2_ref, norm_c_ref, w_out_ref, ln_g_ref, ln_b_ref,
        x1_ref, taila_ref, sd_out_ref, tailb_ref, h_out_ref, sg_out_ref,
        sd_s, sg_s, hista_s, histb_s, h_s, heads_s,
        qa_s, ka_s, va_s, beta_s, gcum_s, gtot_s,
        qc_s, kc_s, vc_s, qb_s, kdec_s, btot_s, qt_s, kt_s,
        t_s, p_s, rhs_s, qe_s, qkd_s, kdst_s, gtst_s, u_s, wk_s, av_s, qbst_s, gk_s,
        *, n_seq, tt, c):
    t_idx = pl.program_id(1)
    n_t = pl.num_programs(1)
    rows = n_seq * tt
    n_groups = rows // GROUP_ROWS
    n_sb = GROUP_ROWS // c
    lc = _log2(c)
    n_levels = lc

    @pl.when(t_idx == 0)
    def _():
        for s in range(n_seq):
            for p in range(N_PAIRS):
                sd = sd_ref[s, p]
                sd_s[s, p] = jnp.concatenate([sd, sd], axis=-1)
                r0 = GROUP_ROWS * (p % 2)
                sg = sg_ref[s, 2 * DK_C * p:2 * DK_C * (p + 1), :]
                sg = jnp.concatenate([sg, sg], axis=-1)
                zeros = jnp.zeros((GROUP_ROWS, LANES), F32)
                sg_s[s, p] = jnp.concatenate([sg, zeros] if r0 == 0 else [zeros, sg], axis=0)
        hista_s[...] = hista_ref[...].reshape(n_seq * HIST, CONV_A_WIDTH)
        histb_s[...] = histb_ref[...].reshape(n_seq * HIST, W_B)
        h_s[...] = h0_ref[...].reshape(n_seq, W_B)

    x = x_ref[...].reshape(rows, D_MODEL)
    proj = _mm(x, w_in_ref[...])

    r_i = _iota((rows, rows), 0)
    r_j = _iota((rows, rows), 1)
    same_chunk = (r_i >> lc) == (r_j >> lc)
    m_cum = jnp.where(same_chunk & (r_j <= r_i), 1.0, 0.0).astype(BF16)
    m_tot = jnp.where(same_chunk, 1.0, 0.0).astype(BF16)

    pre_a = proj[:, OFF_QA:OFF_QA + CONV_A_WIDTH]
    qkv = _silu(_causal_conv(pre_a, hista_s[...], conv_a_w_ref, CONV_A, tt))
    if tt == HIST:
        hista_s[...] = pre_a
    else:
        hista_s[...] = pre_a[rows - HIST:rows]
    e_i = _iota((QK_A, QK_A), 0) >> _log2(DK_A)
    e_j = _iota((QK_A, QK_A), 1) >> _log2(DK_A)
    head_ones = jnp.where(e_i == e_j, 1.0, 0.0).astype(BF16)
    qa = qkv[:, 0:QK_A]
    ka = qkv[:, QK_A:2 * QK_A]
    qa = qa * lax.rsqrt(_mm_mask_rhs(qa * qa, head_ones) + EPS) * (DK_A ** -0.5)
    ka = ka * lax.rsqrt(_mm_mask_rhs(ka * ka, head_ones) + EPS)
    qa_s[...] = qa
    ka_s[...] = ka
    va_s[...] = qkv[:, 2 * QK_A:]

    small = proj[:, OFF_SMALL:OFF_SMALL + LANES]
    beta_s[...] = _sigmoid(small)
    g_full = -jnp.exp(alog_ref[...]) * _softplus(small + dtb_ref[...])
    gcum_s[...] = _mm_mask_lhs(m_cum, g_full)
    gtot_s[...] = _mm_mask_lhs(m_tot, g_full)

    pre_b = proj[:, OFF_XB:OFF_XB + W_B]
    xc = _causal_conv(pre_b, histb_s[...], conv_b_w_ref, CONV_B, tt) + conv_b_b_ref[...]
    if tt == HIST:
        histb_s[...] = pre_b
    else:
        histb_s[...] = pre_b[rows - HIST:rows]
    gate_r = _sigmoid(_mm(xc, wr_ref[...]) + br_ref[...])
    gate_i = _sigmoid(_mm(xc, wi_ref[...]) + bi_ref[...])
    log_a = -LRU_C * gate_r * _softplus(-lam_ref[...])
    a_t = jnp.exp(log_a)
    b_t = jnp.sqrt(-jnp.tanh(log_a) * (a_t * a_t + 1.0)) * (gate_i * xc)
    i_seq = _iota((rows, W_B), 0) & (tt - 1)
    if n_seq == 1:
        h_prev = jnp.broadcast_to(h_s[...], (rows, W_B))
    else:
        h_prev = jnp.concatenate([jnp.broadcast_to(h_s[s:s + 1, :], (tt, W_B)) for s in range(n_seq)], axis=0)
    b_t = b_t + jnp.where(i_seq == 0, a_t * h_prev, 0.0)
    d = 1
    while d < tt:
        a_sh = pltpu.roll(a_t, d, axis=0)
        b_sh = pltpu.roll(b_t, d, axis=0)
        ok = i_seq >= d
        b_t = jnp.where(ok, a_t * b_sh + b_t, b_t)
        a_t = jnp.where(ok, a_t * a_sh, a_t)
        d *= 2
    if n_seq == 1:
        h_s[...] = b_t[rows - 1:rows]
    else:
        h_s[...] = jnp.concatenate([b_t[(s + 1) * tt - 1:(s + 1) * tt] for s in range(n_seq)], axis=0)
    heads_s[:, W_A:W_A + W_B] = b_t * _gelu_tanh(proj[:, OFF_GB:OFF_GB + W_B])

    qc = proj[:, OFF_QC:OFF_QC + QKC_PAD] * (DK_C ** -0.5)
    kc = proj[:, OFF_KC:OFF_KC + QKC_PAD]
    logf = -_softplus(-(_mm(small, w2_ref[...]) + b2_ref[...])) * (1.0 / GLA_TAU)
    b_cum = _mm_mask_lhs(m_cum, logf)
    b_tot = _mm_mask_lhs(m_tot, logf)
    qc_s[...] = qc
    kc_s[...] = kc
    vc_s[...] = proj[:, OFF_VC:OFF_VC + W_C]
    qb_s[...] = qc * jnp.exp(b_cum)
    kdec_s[...] = kc * jnp.exp(b_tot - b_cum)
    btot_s[...] = b_tot
    for lvl in range(n_levels):
        lh = lc - 1 - lvl
        hi_i = (r_i >> lh) & 1
        start_i = (r_i >> lh) << lh
        next_i = ((r_i >> lh) + 1) << lh
        m_q = jnp.where((hi_i == 1) & (r_j >= start_i) & (r_j <= r_i), 1.0, 0.0).astype(BF16)
        m_k = jnp.where((hi_i == 0) & (r_j > r_i) & (r_j < next_i), 1.0, 0.0).astype(BF16)
        qt_s[lvl] = qc * jnp.exp(_mm_mask_lhs(m_q, logf))
        kt_s[lvl] = kc * jnp.exp(_mm_mask_lhs(m_k, logf))

    s_i = _iota((STACK, STACK), 0)
    s_j = _iota((STACK, STACK), 1)
    same_blk = (s_i >> lc) == (s_j >> lc)
    m_incl = same_blk & (s_j <= s_i)
    m_strict = same_blk & (s_j < s_i)
    eye = jnp.where(s_i == s_j, 1.0, 0.0)
    lane = _iota((GROUP_ROWS, LANES), 1)
    m0 = lane < HALF

    def bdot(a, b):
        return jnp.dot(a, b, preferred_element_type=F32)

    def precompute_body(gi, carry):
        r0 = pl.multiple_of(gi * GROUP_ROWS, GROUP_ROWS)
        rs = pl.ds(r0, GROUP_ROWS)
        beta_g = beta_s[rs, :]
        gcum_g = gcum_s[rs, :]
        gtot_g = gtot_s[rs, :]
        for p in range(N_PAIRS):
            sl = pl.ds(p * LANES, LANES)
            q_p = qa_s[rs, sl]
            k_p = ka_s[rs, sl]
            v0, v1 = _dup_values(va_s[rs, sl])
            q_st = _stack_rows(None, n_sb, c, [jnp.where(m0, q_p, 0.0), jnp.where(m0, 0.0, q_p)])
            k_st = _stack_rows(None, n_sb, c, [jnp.where(m0, k_p, 0.0), jnp.where(m0, 0.0, k_p)])
            v_st = _stack_rows(None, n_sb, c, [v0, v1])
            beta_c = _col_bcast_stack(beta_g, LANE_BETA + 2 * p, n_sb, c)
            g_c = _col_bcast_stack(gcum_g, LANE_DECAY + 2 * p, n_sb, c)
            gt_c = _col_bcast_stack(gtot_g, LANE_DECAY + 2 * p, n_sb, c)
            dec = jnp.exp(jnp.where(m_incl, g_c - g_c.T, -1e30))
            kk = _mm_nt(k_st, k_st)
            qk = _mm_nt(q_st, k_st)
            n_mat = -(beta_c * kk * jnp.where(m_strict, dec, 0.0))
            eg = jnp.exp(g_c)
            t_s[gi, p] = eye + n_mat
            p_s[gi, p] = n_mat.astype(BF16)
            rhs_s[gi, p] = jnp.concatenate([beta_c * v_st, beta_c * eg * k_st], axis=1).astype(BF16)
            qe_s[gi, p] = (eg * q_st).astype(BF16)
            qkd_s[gi, p] = (qk * dec).astype(BF16)
            kdst_s[gi, p] = k_st * jnp.exp(gt_c - g_c)
            gtst_s[gi, p] = gt_c
        for j in range(1, lc):
            for p in range(N_PAIRS):
                pw = p_s[gi, p]
                if j >= 2:
                    t_old = t_s[gi, p]
                    t_s[gi, p] = t_old + bdot(t_old.astype(BF16), pw)
                p_s[gi, p] = bdot(pw, pw).astype(BF16)
        for p in range(N_PAIRS):
            t_old = t_s[gi, p]
            t_fin = t_old + bdot(t_old.astype(BF16), p_s[gi, p])
            uwk = bdot(t_fin.astype(BF16), rhs_s[gi, p])
            u_s[gi, p] = uwk[:, 0:LANES]
            wk_s[gi, p] = uwk[:, LANES:].astype(BF16)

        for p in range(N_PAIRS):
            sl = pl.ds(p * LANES, LANES)
            slab = pl.ds((p // 2) * LANES, LANES)
            l0 = 2 * DK_C * (p % 2)
            mh0 = (lane >= l0) & (lane < l0 + DK_C)
            mh1 = (lane >= l0 + DK_C) & (lane < l0 + 2 * DK_C)

            def stack_c(ref, idx=None):
                val = ref[rs, slab] if idx is None else ref[idx, rs, slab]
                return _stack_rows(None, n_sb, c, [jnp.where(mh0, val, 0.0), jnp.where(mh1, val, 0.0)])

            att = jnp.where(s_i == s_j, _mm_nt(stack_c(qc_s), stack_c(kc_s)), 0.0)
            for lvl in range(n_levels):
                lh = lc - 1 - lvl
                valid = (((s_i >> (lh + 1)) == (s_j >> (lh + 1)))
                         & (((s_i >> lh) & 1) == 1) & (((s_j >> lh) & 1) == 0))
                att = att + jnp.where(valid, _mm_nt(stack_c(qt_s, lvl), stack_c(kt_s, lvl)), 0.0)
            vc0, vc1 = _dup_values(vc_s[rs, sl])
            vc_st = _stack_rows(None, n_sb, c, [vc0, vc1]).astype(BF16)
            av_s[gi, p] = bdot(att.astype(BF16), vc_st)
            qbst_s[gi, p] = stack_c(qb_s).astype(BF16)
            kd_st = stack_c(kdec_s)
            for sb in range(n_sb):
                blk = slice(sb * 2 * c, (sb + 1) * 2 * c)
                gk_s[gi, p, sb] = bdot(kd_st[blk].T.astype(BF16), vc_st[blk])
        return carry

    def recurrence_body(gi, carry):
        r0 = pl.multiple_of(gi * GROUP_ROWS, GROUP_ROWS)
        rs = pl.ds(r0, GROUP_ROWS)
        for p in range(N_PAIRS):
            sl = pl.ds(p * LANES, LANES)
            u = u_s[gi, p]
            wk = wk_s[gi, p]
            qe = qe_s[gi, p]
            k_dec = kdst_s[gi, p]
            gt_c = gtst_s[gi, p]
            w_parts, qs_parts = [], []
            for sb in range(n_sb):
                seq = 0 if n_seq == 1 else gi * n_sb + sb
                blk = slice(sb * 2 * c, (sb + 1) * 2 * c)
                s_old = sd_s[seq, p]
                s_bf = s_old.astype(BF16)
                w_sb = u[blk] - bdot(wk[blk], s_bf)
                w_parts.append(w_sb)
                qs_parts.append(bdot(qe[blk], s_bf))
                decay_rows = jnp.concatenate(
                    [jnp.broadcast_to(gt_c[sb * 2 * c + a * c:sb * 2 * c + a * c + 1, :], (GROUP_ROWS, LANES))
                     for a in range(2)], axis=0)
                sd_s[seq, p] = jnp.exp(decay_rows) * s_old + bdot(k_dec[blk].T.astype(BF16), w_sb.astype(BF16))
            w_all = w_parts[0] if n_sb == 1 else jnp.concatenate(w_parts, axis=0)
            qs_all = qs_parts[0] if n_sb == 1 else jnp.concatenate(qs_parts, axis=0)
            o_st = qs_all + bdot(qkd_s[gi, p], w_all.astype(BF16))
            o_st = o_st * lax.rsqrt(jnp.mean(o_st * o_st, axis=-1, keepdims=True) + EPS) * norm_a_ref[...]
            heads_s[rs, sl] = _unstack_rows(o_st, n_sb, c)
            slab = pl.ds((p // 2) * LANES, LANES)
            qb_st = qbst_s[gi, p]
            btot_g = btot_s[rs, slab]
            oi_parts = []
            for sb in range(n_sb):
                seq = 0 if n_seq == 1 else gi * n_sb + sb
                blk = slice(sb * 2 * c, (sb + 1) * 2 * c)
                s_old = sg_s[seq, p]
                oi_parts.append(bdot(qb_st[blk], s_old.astype(BF16)))
                decay_col = jnp.broadcast_to(btot_g[sb * c:sb * c + 1, :], (LANES, LANES)).T
                sg_s[seq, p] = jnp.exp(decay_col) * s_old + gk_s[gi, p, sb]
            oi_all = oi_parts[0] if n_sb == 1 else jnp.concatenate(oi_parts, axis=0)
            oc_st = oi_all + av_s[gi, p]
            oc_st = oc_st * lax.rsqrt(jnp.mean(oc_st * oc_st, axis=-1, keepdims=True) + EPS) * norm_c_ref[...]
            heads_s[rs, pl.ds(W_A + W_B + p * LANES, LANES)] = _unstack_rows(oc_st, n_sb, c)
        return carry

    if n_groups == 1:
        precompute_body(0, 0)
        recurrence_body(0, 0)
    else:
        lax.fori_loop(0, n_groups, precompute_body, 0)
        lax.fori_loop(0, n_groups, recurrence_body, 0)

    za = proj[:, OFF_ZA:OFF_ZA + W_A]
    zc = proj[:, OFF_ZC:OFF_ZC + W_C]
    heads_s[:, 0:W_A] = heads_s[:, 0:W_A] * _silu(za)
    heads_s[:, W_A + W_B:] = heads_s[:, W_A + W_B:] * _silu(zc)
    y = ALPHA * x + _mm(heads_s[...], w_out_ref[...])
    x1_ref[...] = _layer_norm(y, ln_g_ref[...], ln_b_ref[...]).reshape(n_seq, tt, D_MODEL)

    @pl.when(t_idx == n_t - 1)
    def _():
        taila_ref[...] = hista_s[...].reshape(n_seq, HIST, CONV_A_WIDTH)
        tailb_ref[...] = histb_s[...].reshape(n_seq, HIST, W_B)
        h_out_ref[...] = h_s[...].reshape(n_seq, 1, W_B)
        for s in range(n_seq):
            for p in range(N_PAIRS):
                sd_out_ref[s, p] = sd_s[s, p][:, 0:HALF]
                r0 = GROUP_ROWS * (p % 2)
                sg_out_ref[s, 2 * DK_C * p:2 * DK_C * (p + 1), :] = sg_s[s, p][r0:r0 + GROUP_ROWS, 0:HALF]


def _const_spec(shape):
    nd = len(shape)
    return pl.BlockSpec(shape, lambda b, t: (0,) * nd)


def _mixer_call(x, hista, sd, histb, h0, sg, w, *, n_seq, tt, c, depth):
    nb, t_total, _ = x.shape
    assert nb % n_seq == 0 and t_total % tt == 0
    assert (n_seq == 1 and tt % GROUP_ROWS == 0 and c == GROUP_ROWS) or (tt == HIST and c == HIST)
    assert depth == DEPTH
    rows = n_seq * tt
    assert rows % GROUP_ROWS == 0
    grid = (nb // n_seq, t_total // tt)
    n_levels = _log2(c)
    n_groups = rows // GROUP_ROWS

    def seq_spec(shape_tail):
        nd = len(shape_tail)
        return pl.BlockSpec((n_seq,) + shape_tail, lambda b, t: (b,) + (0,) * nd)

    weights = (w['w_in'], w['conv_a_w'], w['alog'], w['dtb'], w['norm_a'], w['conv_b_w'], w['conv_b_b'],
               w['wr'], w['br'], w['wi'], w['bi'], w['lam'], w['w2'], w['b2'], w['norm_c'], w['w_out'],
               w['ln1_g'], w['ln1_b'])
    in_specs = [
        pl.BlockSpec((n_seq, tt, D_MODEL), lambda b, t: (b, t, 0)),
        seq_spec((HIST, CONV_A_WIDTH)),
        seq_spec((N_PAIRS, 2 * DK_A, DV_A)),
        seq_spec((HIST, W_B)),
        seq_spec((1, W_B)),
        seq_spec((QK_C, DV_C)),
    ] + [_const_spec(a.shape) for a in weights]
    out_shape = (
        jax.ShapeDtypeStruct((nb, t_total, D_MODEL), F32),
        jax.ShapeDtypeStruct((nb, HIST, CONV_A_WIDTH), F32),
        jax.ShapeDtypeStruct((nb, N_PAIRS, 2 * DK_A, DV_A), F32),
        jax.ShapeDtypeStruct((nb, HIST, W_B), F32),
        jax.ShapeDtypeStruct((nb, 1, W_B), F32),
        jax.ShapeDtypeStruct((nb, QK_C, DV_C), F32),
    )
    out_specs = (
        pl.BlockSpec((n_seq, tt, D_MODEL), lambda b, t: (b, t, 0)),
        seq_spec((HIST, CONV_A_WIDTH)),
        seq_spec((N_PAIRS, 2 * DK_A, DV_A)),
        seq_spec((HIST, W_B)),
        seq_spec((1, W_B)),
        seq_spec((QK_C, DV_C)),
    )
    scratch = [
        pltpu.VMEM((n_seq, N_PAIRS, STACK, LANES), F32),
        pltpu.VMEM((n_seq, N_PAIRS, LANES, LANES), F32),
        pltpu.VMEM((n_seq * HIST, CONV_A_WIDTH), F32),
        pltpu.VMEM((n_seq * HIST, W_B), F32),
        pltpu.VMEM((n_seq, W_B), F32),
        pltpu.VMEM((rows, D_MIX), F32),
        pltpu.VMEM((rows, QK_A), F32),
        pltpu.VMEM((rows, QK_A), F32),
        pltpu.VMEM((rows, W_A), F32),
        pltpu.VMEM((rows, LANES), F32),
        pltpu.VMEM((rows, LANES), F32),
        pltpu.VMEM((rows, LANES), F32),
        pltpu.VMEM((rows, QKC_PAD), F32),
        pltpu.VMEM((rows, QKC_PAD), F32),
        pltpu.VMEM((rows, W_C), F32),
        pltpu.VMEM((rows, QKC_PAD), F32),
        pltpu.VMEM((rows, QKC_PAD), F32),
        pltpu.VMEM((rows, QKC_PAD), F32),
        pltpu.VMEM((n_levels, rows, QKC_PAD), F32),
        pltpu.VMEM((n_levels, rows, QKC_PAD), F32),
        pltpu.VMEM((n_groups, N_PAIRS, STACK, STACK), F32),
        pltpu.VMEM((n_groups, N_PAIRS, STACK, STACK), BF16),
        pltpu.VMEM((n_groups, N_PAIRS, STACK, 2 * LANES), BF16),
        pltpu.VMEM((n_groups, N_PAIRS, STACK, LANES), BF16),
        pltpu.VMEM((n_groups, N_PAIRS, STACK, STACK), BF16),
        pltpu.VMEM((n_groups, N_PAIRS, STACK, LANES), F32),
        pltpu.VMEM((n_groups, N_PAIRS, STACK, LANES), F32),
        pltpu.VMEM((n_groups, N_PAIRS, STACK, LANES), F32),
        pltpu.VMEM((n_groups, N_PAIRS, STACK, LANES), BF16),
        pltpu.VMEM((n_groups, N_PAIRS, STACK, LANES), F32),
        pltpu.VMEM((n_groups, N_PAIRS, STACK, LANES), BF16),
        pltpu.VMEM((n_groups, N_PAIRS, GROUP_ROWS // c, LANES, LANES), F32),
    ]
    kern = functools.partial(_mixer_kernel, n_seq=n_seq, tt=tt, c=c)
    return pl.pallas_call(
        kern,
        grid=grid,
        in_specs=in_specs,
        out_specs=out_specs,
        out_shape=out_shape,
        scratch_shapes=scratch,
        compiler_params=pltpu.CompilerParams(
            dimension_semantics=("arbitrary", "arbitrary"), vmem_limit_bytes=VMEM_LIMIT_BYTES),
        name=f"mixer_nseq{n_seq}_tt{tt}",
    )(x, hista, sd, histb, h0, sg, *weights)


def _ffn_kernel(x_ref, hist_ref, wg_ref, wv_ref, cw_ref, cb_ref, wd_ref, ln_g_ref, ln_b_ref,
                y_ref, tail_ref, hist_s, acc_s, *, n_seq, tt):
    t_idx = pl.program_id(1)
    n_t = pl.num_programs(1)
    rows = n_seq * tt

    @pl.when(t_idx == 0)
    def _():
        hist_s[...] = hist_ref[...]

    x = x_ref[...].reshape(rows, D_MODEL)
    xb = x.astype(BF16)
    acc_s[...] = jnp.zeros((rows, D_MODEL), F32)

    def chunk_body(j, carry):
        gate = jnp.dot(xb, wg_ref[j], preferred_element_type=F32)
        val = jnp.dot(xb, wv_ref[j], preferred_element_type=F32)
        conv = _causal_conv(gate, hist_s[j], cw_ref.at[j], CONV_F, tt) + cb_ref[j]
        if tt == HIST:
            hist_s[j] = gate
        else:
            hist_s[j] = gate[rows - HIST:rows]
        h = _gelu_tanh(conv) * val
        acc_s[...] += _mm(h, wd_ref[j])
        return carry

    lax.fori_loop(0, N_FF_CHUNKS, chunk_body, 0)
    y = ALPHA * x + acc_s[...]
    y_ref[...] = _layer_norm(y, ln_g_ref[...], ln_b_ref[...]).reshape(n_seq, tt, D_MODEL)

    @pl.when(t_idx == n_t - 1)
    def _():
        tail_ref[...] = hist_s[...]


def _ffn_call(x, hist, w, *, n_seq, tt):
    nb, t_total, _ = x.shape
    rows = n_seq * tt
    grid = (nb // n_seq, t_total // tt)
    weights = (w['wg'], w['wv'], w['cw'], w['cb'], w['wd'], w['ln2_g'], w['ln2_b'])
    hist_spec = pl.BlockSpec((N_FF_CHUNKS, n_seq * HIST, FF_CHUNK), lambda b, t: (0, b, 0))
    in_specs = [pl.BlockSpec((n_seq, tt, D_MODEL), lambda b, t: (b, t, 0)), hist_spec] + [
        _const_spec(a.shape) for a in weights]
    out_shape = (jax.ShapeDtypeStruct((nb, t_total, D_MODEL), F32),
                 jax.ShapeDtypeStruct((N_FF_CHUNKS, nb * HIST, FF_CHUNK), F32))
    out_specs = (pl.BlockSpec((n_seq, tt, D_MODEL), lambda b, t: (b, t, 0)), hist_spec)
    scratch = [pltpu.VMEM((N_FF_CHUNKS, n_seq * HIST, FF_CHUNK), F32), pltpu.VMEM((rows, D_MODEL), F32)]
    kern = functools.partial(_ffn_kernel, n_seq=n_seq, tt=tt)
    return pl.pallas_call(
        kern,
        grid=grid,
        in_specs=in_specs,
        out_specs=out_specs,
        out_shape=out_shape,
        scratch_shapes=scratch,
        compiler_params=pltpu.CompilerParams(
            dimension_semantics=("arbitrary", "arbitrary"), vmem_limit_bytes=VMEM_LIMIT_BYTES),
        name=f"ffn_nseq{n_seq}_tt{tt}",
    )(x, hist, *weights)


def _lane_row(vec, lane0, width=LANES):
    out = jnp.zeros((1, width), F32)
    return out.at[0, lane0:lane0 + vec.shape[0]].set(vec.astype(F32))


def _prep_layer_weights(w_in, conv_a_w, a_log, dt_bias, norm_a_w, conv_b_w, conv_b_b, lru_w_r, lru_b_r, lru_w_i,
                        lru_b_i, lru_lambda, gla_w2, gla_b2, norm_c_w, w_out, ln1_g, ln1_b, ffn_w_up, ffn_conv_w,
                        ffn_conv_b, ffn_w_down, ln2_g, ln2_b):
    pts = [0]
    for s in (QK_A, QK_A, W_A, W_A, H_A, H_A, W_B, W_B, QK_C, QK_C, W_C, W_C, GLA_RANK):
        pts.append(pts[-1] + s)
    (qa, ka, va, za, ba, aa, xb, gb, qc, kc, vc, zc, lc) = [w_in[:, pts[i]:pts[i + 1]] for i in range(13)]
    d = w_in.shape[0]
    zpad = jnp.zeros((d, QKC_PAD - QK_C), w_in.dtype)
    small = jnp.zeros((d, LANES), w_in.dtype)
    small = small.at[:, LANE_BETA:LANE_BETA + H_A].set(ba)
    small = small.at[:, LANE_DECAY:LANE_DECAY + H_A].set(aa)
    small = small.at[:, LANE_LOWRANK:LANE_LOWRANK + GLA_RANK].set(lc)
    w_in_p = jnp.concatenate([qa, ka, va, za, xb, gb, qc, zpad, kc, zpad, vc, zc, small], axis=1).astype(BF16)
    assert w_in_p.shape[1] == D_IN_PAD

    def block_diag(wb):
        out = jnp.zeros((W_B, W_B), F32)
        for n in range(LRU_BLOCKS):
            out = out.at[n * LRU_BLOCK:(n + 1) * LRU_BLOCK, n * LRU_BLOCK:(n + 1) * LRU_BLOCK].set(wb[n])
        return out.astype(BF16)

    w2 = jnp.zeros((LANES, QKC_PAD), F32).at[LANE_LOWRANK:LANE_LOWRANK + GLA_RANK, 0:QK_C].set(gla_w2).astype(BF16)
    row = lambda v: v.astype(F32).reshape(1, -1)
    dup = lambda v: jnp.concatenate([v, v]).astype(F32).reshape(1, LANES)
    gate_w = ffn_w_up[:, :D_FF].reshape(d, N_FF_CHUNKS, FF_CHUNK).transpose(1, 0, 2).astype(BF16)
    val_w = ffn_w_up[:, D_FF:].reshape(d, N_FF_CHUNKS, FF_CHUNK).transpose(1, 0, 2).astype(BF16)
    return {
        'w_in': w_in_p, 'conv_a_w': conv_a_w.astype(F32),
        'alog': _lane_row(a_log, LANE_DECAY), 'dtb': _lane_row(dt_bias, LANE_DECAY), 'norm_a': dup(norm_a_w),
        'conv_b_w': conv_b_w.astype(F32), 'conv_b_b': row(conv_b_b),
        'wr': block_diag(lru_w_r), 'br': row(lru_b_r), 'wi': block_diag(lru_w_i), 'bi': row(lru_b_i),
        'lam': row(lru_lambda), 'w2': w2, 'b2': _lane_row(gla_b2, 0, QKC_PAD), 'norm_c': dup(norm_c_w),
        'w_out': w_out.astype(BF16), 'ln1_g': row(ln1_g), 'ln1_b': row(ln1_b),
        'wg': gate_w, 'wv': val_w,
        'cw': ffn_conv_w.astype(F32).reshape(CONV_F, N_FF_CHUNKS, FF_CHUNK).transpose(1, 0, 2),
        'cb': ffn_conv_b.astype(F32).reshape(N_FF_CHUNKS, 1, FF_CHUNK),
        'wd': ffn_w_down.reshape(N_FF_CHUNKS, FF_CHUNK, d).astype(BF16),
        'ln2_g': row(ln2_g), 'ln2_b': row(ln2_b),
    }


def _pad_hist(state):
    b, k, ch = state.shape
    return jnp.concatenate([jnp.zeros((b, HIST - k, ch), F32), state.astype(F32)], axis=1)


def _trunk(x, states, layer_weights, *, n_seq, tt, c):
    st_dconv, st_delta, st_lconv, st_lru, st_gla, st_fconv = states
    depth = len(layer_weights)
    nb = x.shape[0]
    outs = [[] for _ in range(6)]
    for l in range(depth):
        w = layer_weights[l]
        hista = _pad_hist(st_dconv[l])
        sd = st_delta[l].astype(F32).reshape(nb, N_PAIRS, 2 * DK_A, DV_A)
        histb = _pad_hist(st_lconv[l])
        h0 = st_lru[l].astype(F32).reshape(nb, 1, W_B)
        sg = st_gla[l].astype(F32).reshape(nb, QK_C, DV_C)
        histf = _pad_hist(st_fconv[l]).reshape(nb * HIST, N_FF_CHUNKS, FF_CHUNK).transpose(1, 0, 2)
        x1, taila, sd_o, tailb, h_o, sg_o = _mixer_call(x, hista, sd, histb, h0, sg, w,
                                                         n_seq=n_seq, tt=tt, c=c, depth=depth)
        x, tailf = _ffn_call(x1, histf, w, n_seq=n_seq, tt=tt)
        outs[0].append(taila[:, HIST - (CONV_A - 1):, :])
        outs[1].append(sd_o.reshape(nb, H_A, DK_A, DV_A))
        outs[2].append(tailb[:, HIST - (CONV_B - 1):, :])
        outs[3].append(h_o.reshape(nb, W_B))
        outs[4].append(sg_o.reshape(nb, H_C, DK_C, DV_C))
        tailf = tailf.transpose(1, 0, 2).reshape(nb, HIST, D_FF)
        outs[5].append(tailf[:, HIST - (CONV_F - 1):, :])
    return x, tuple(jnp.stack(o) for o in outs)


PROMPT_TT = 256
SAMPLE_NSEQ = 16


def kernel(x_prompt, x_sample, state_delta_conv, state_delta, state_lru_conv, state_lru, state_gla, state_ffn_conv,
           w_in, conv_a_w, a_log, dt_bias, norm_a_w, conv_b_w, conv_b_b, lru_w_r, lru_b_r, lru_w_i, lru_b_i,
           lru_lambda, gla_w2, gla_b2, norm_c_w, w_out, ln1_g, ln1_b, ffn_w_up, ffn_conv_w, ffn_conv_b, ffn_w_down,
           ln2_g, ln2_b):
    weights = (w_in, conv_a_w, a_log, dt_bias, norm_a_w, conv_b_w, conv_b_b, lru_w_r, lru_b_r, lru_w_i, lru_b_i,
               lru_lambda, gla_w2, gla_b2, norm_c_w, w_out, ln1_g, ln1_b, ffn_w_up, ffn_conv_w, ffn_conv_b,
               ffn_w_down, ln2_g, ln2_b)
    depth = w_in.shape[0]
    layer_weights = [_prep_layer_weights(*(w[l] for w in weights)) for l in range(depth)]
    sample_states = (state_delta_conv, state_delta, state_lru_conv, state_lru, state_gla, state_ffn_conv)
    n_prompt = x_prompt.shape[0]
    prompt_states = tuple(jnp.zeros((depth, n_prompt) + s.shape[2:], F32) for s in sample_states)
    t_p = x_prompt.shape[1]
    t_s = x_sample.shape[1]
    y_p, p_st = _trunk(x_prompt, prompt_states, layer_weights, n_seq=1, tt=min(PROMPT_TT, t_p), c=CHUNK)
    y_s, s_st = _trunk(x_sample, sample_states, layer_weights,
                       n_seq=min(SAMPLE_NSEQ, x_sample.shape[0]), tt=t_s, c=t_s)
    return (y_p, y_s) + p_st + s_st
```

```python
import functools
import math

import numpy as np
import jax
import jax.numpy as jnp
from jax import lax
from jax.experimental import pallas as pl
from jax.experimental.pallas import tpu as pltpu

F32 = jnp.float32
BF16 = jnp.bfloat16

D_MODEL = 1024
H_A, DK_A, DV_A = 6, 64, 64
QK_A = H_A * DK_A
W_A = H_A * DV_A
CONV_A = 4
W_B = 256
LRU_BLOCKS = 4
LRU_BLOCK = W_B // LRU_BLOCKS
CONV_B = 4
LRU_C = 8.0
H_C, DK_C, DV_C = 6, 32, 64
QK_C = H_C * DK_C
W_C = H_C * DV_C
GLA_RANK = 16
GLA_TAU = 16.0
D_MIX = W_A + W_B + W_C
CHUNK = 64
D_FF = 2816
CONV_F = 3
EPS = 1e-6
CONV_A_WIDTH = 2 * QK_A + W_A
DEPTH = 2
ALPHA = (2.0 * DEPTH) ** 0.25

LANES = 128
SUBLANES = 8
HALF = LANES // 2
GROUP_ROWS = 64
STACK = 2 * GROUP_ROWS
N_PAIRS = 3
HIST = SUBLANES
FF_CHUNK = 256
N_FF_CHUNKS = D_FF // FF_CHUNK
VMEM_LIMIT_BYTES = 56 * 1024 * 1024
PROMPT_TT = 256
SAMPLE_NSEQ = 16

OFF_QA = 0
OFF_KA = OFF_QA + QK_A
OFF_VA = OFF_KA + QK_A
OFF_ZA = OFF_VA + W_A
OFF_XB = OFF_ZA + W_A
OFF_GB = OFF_XB + W_B
OFF_QC = OFF_GB + W_B
QKC_PAD = 2 * LANES
OFF_KC = OFF_QC + QKC_PAD
OFF_VC = OFF_KC + QKC_PAD
OFF_ZC = OFF_VC + W_C
OFF_SMALL = OFF_ZC + W_C
D_IN_PAD = OFF_SMALL + LANES
LANE_BETA = 0
LANE_DECAY = 8
LANE_LOWRANK = 16

VEC_W = CONV_A_WIDTH
(ROW_CONV_A, ROW_CONV_B, ROW_CONV_B_BIAS, ROW_ALOG, ROW_DTB, ROW_NORM_A, ROW_BR, ROW_BI, ROW_LAM, ROW_B2,
 ROW_NORM_C, ROW_LN1_G, ROW_LN1_B) = (0, 4, 8, 9, 10, 11, 12, 13, 14, 15, 16, 17, 18)
MIX_VEC_ROWS = 24
(ROW_CONV_F, ROW_CONV_F_BIAS, ROW_LN2_G, ROW_LN2_B) = (0, 3, 4, 5)
FFN_VEC_ROWS = 8


def _bdot(a, b):
    return jnp.dot(a, b, preferred_element_type=F32)


def _mm(a, b):
    return _bdot(a.astype(BF16), b.astype(BF16))


def _mm_nt(a, b):
    return lax.dot_general(a.astype(BF16), b.astype(BF16), (((1,), (1,)), ((), ())), preferred_element_type=F32)


def _split(a, n):
    parts = []
    r = a
    for i in range(n):
        p = r.astype(BF16)
        parts.append(p)
        if i + 1 < n:
            r = r - p.astype(F32)
    return parts


def _mm_mask_lhs(mask_bf16, a, n):
    out = None
    for p in _split(a, n):
        t = _bdot(mask_bf16, p)
        out = t if out is None else out + t
    return out


def _softplus(x):
    return jnp.maximum(x, 0.0) + jnp.log1p(jnp.exp(-jnp.abs(x)))


def _sigmoid(x):
    return 0.5 * jnp.tanh(0.5 * x) + 0.5


def _silu(x):
    h = 0.5 * x
    return h * jnp.tanh(h) + h


def _gelu_tanh(x):
    return 0.5 * x * (1.0 + jnp.tanh(math.sqrt(2.0 / math.pi) * (x + 0.044715 * (x * x * x))))


def _layer_norm(x, g, b):
    mu = jnp.mean(x, axis=-1, keepdims=True)
    xc = x - mu
    var = jnp.mean(xc * xc, axis=-1, keepdims=True)
    return xc * lax.rsqrt(var + EPS) * g + b


def _iota(shape, axis):
    return lax.broadcasted_iota(jnp.int32, shape, axis)


def _log2(n):
    l = int(math.log2(n))
    assert (1 << l) == n
    return l


def _conv_from_buffer(buf_ref, vec_ref, row0, width, tt, cols):
    y = None
    for d in range(width):
        tap = vec_ref[row0 + width - 1 - d:row0 + width - d, cols]
        term = buf_ref[:, HIST - d:HIST - d + tt, cols] * tap
        y = term if y is None else y + term
    return y


def _stack_rows(n_sb, c, pieces):
    if n_sb == 1:
        return jnp.concatenate([pieces[0], pieces[1]], axis=0)
    out = []
    for s in range(n_sb):
        for a in range(2):
            out.append(pieces[a][s * c:(s + 1) * c])
    return jnp.concatenate(out, axis=0)


def _unstack_rows(o, n_sb, c):
    lo = _iota((c, LANES), 1) < HALF
    out = []
    for s in range(n_sb):
        base = s * 2 * c
        out.append(jnp.where(lo, o[base:base + c], o[base + c:base + 2 * c]))
    return out[0] if n_sb == 1 else jnp.concatenate(out, axis=0)


def _col_bcast_stack(vals, lane0, n_sb, c):
    pieces = [jnp.broadcast_to(vals[:, lane0 + a:lane0 + a + 1], (GROUP_ROWS, LANES)) for a in range(2)]
    return _stack_rows(n_sb, c, pieces)


def _dup_values(v):
    lo = _iota(v.shape, 1) < HALF
    vr = pltpu.roll(v, HALF, axis=1)
    return jnp.where(lo, v, vr), jnp.where(lo, vr, v)


def _tile_masks(rows, c):
    lc = _log2(c)
    i = np.arange(rows)[:, None]
    j = np.arange(rows)[None, :]
    same = (i >> lc) == (j >> lc)
    out = [same & (j <= i), same]
    for lvl in range(lc):
        lh = lc - 1 - lvl
        hi = (i >> lh) & 1
        start = (i >> lh) << lh
        nxt = ((i >> lh) + 1) << lh
        out.append((hi == 1) & (j >= start) & (j <= i))
        out.append((hi == 0) & (j > i) & (j < nxt))
    return jnp.asarray(np.stack(out).astype(np.float32), dtype=BF16)


def _mixer_kernel(*refs, n_seq, tt, c, has_state):
    n_state = 5 if has_state else 0
    x_ref = refs[0]
    state_refs = refs[1:1 + n_state]
    (masks_ref, w_in_ref, vec_ref, wr_ref, wi_ref, w2_ref, w_out_ref) = refs[1 + n_state:8 + n_state]
    (x1_ref, dconv_out_ref, sd_out_ref, lconv_out_ref, h_out_ref, sg_out_ref) = refs[8 + n_state:14 + n_state]
    (sd_s, sg_s, abuf_s, bbuf_s, h_s, heads_s,
     qa_s, ka_s, va_s, beta_s, gcum_s, gtot_s,
     qc_s, kc_s, vc_s, qb_s, kdec_s, btot_s, qt_s, kt_s,
     t_s, p_s, rhs_s, qe_s, qkd_s, kdst_s, gtst_s, u_s, wk_s, av_s, qbst_s, gk_s) = refs[14 + n_state:]

    t_idx = pl.program_id(1)
    n_t = pl.num_programs(1)
    rows = n_seq * tt
    n_groups = rows // GROUP_ROWS
    n_sb = GROUP_ROWS // c
    lc = _log2(c)
    n_levels = lc
    a_tail = slice(HIST - (CONV_A - 1), HIST)
    b_tail = slice(HIST - (CONV_B - 1), HIST)

    @pl.when(t_idx == 0)
    def _():
        if has_state:
            dconv_ref, sd_ref, lconv_ref, h0_ref, sg_ref = state_refs
            abuf_s[:, a_tail, :] = dconv_ref[...]
            bbuf_s[:, b_tail, :] = lconv_ref[...]
            h_s[...] = h0_ref[...].reshape(n_seq, W_B)
            zeros = jnp.zeros((GROUP_ROWS, LANES), F32)
            for s in range(n_seq):
                for p in range(N_PAIRS):
                    sd = sd_ref[s, p]
                    sd_s[s, p] = jnp.concatenate([sd, sd], axis=-1)
                    sg = sg_ref[s, 2 * DK_C * p:2 * DK_C * (p + 1), :]
                    sg = jnp.concatenate([sg, sg], axis=-1)
                    sg_s[s, p] = jnp.concatenate([sg, zeros] if p % 2 == 0 else [zeros, sg], axis=0)
        else:
            abuf_s[:, a_tail, :] = jnp.zeros((n_seq, CONV_A - 1, CONV_A_WIDTH), F32)
            bbuf_s[:, b_tail, :] = jnp.zeros((n_seq, CONV_B - 1, W_B), F32)
            h_s[...] = jnp.zeros((n_seq, W_B), F32)
            sd_s[...] = jnp.zeros(sd_s.shape, F32)
            sg_s[...] = jnp.zeros(sg_s.shape, F32)

    def vec(row, width):
        return vec_ref[row:row + 1, 0:width]

    x = x_ref[...].reshape(rows, D_MODEL)
    proj = _mm(x, w_in_ref[...])
    m_cum = masks_ref[0]
    m_tot = masks_ref[1]

    abuf_s[:, HIST:, :] = proj[:, OFF_QA:OFF_QA + CONV_A_WIDTH].reshape(n_seq, tt, CONV_A_WIDTH)
    conv_a = _conv_from_buffer(abuf_s, vec_ref, ROW_CONV_A, CONV_A, tt, slice(0, CONV_A_WIDTH))
    qkv = _silu(conv_a.reshape(rows, CONV_A_WIDTH))
    abuf_s[:, a_tail, :] = abuf_s[:, tt + HIST - (CONV_A - 1):tt + HIST, :]
    p_i = _iota((LANES, LANES), 0) >> _log2(DK_A)
    p_j = _iota((LANES, LANES), 1) >> _log2(DK_A)
    pair_ones = jnp.where(p_i == p_j, 1.0, 0.0).astype(BF16)
    for p in range(N_PAIRS):
        for off, dst, scale in ((0, qa_s, DK_A ** -0.5), (QK_A, ka_s, 1.0)):
            v = qkv[:, off + p * LANES:off + (p + 1) * LANES]
            ssq = _bdot((v * v).astype(BF16), pair_ones)
            dst[:, p * LANES:(p + 1) * LANES] = v * (lax.rsqrt(ssq + EPS) * scale)
    va_s[...] = qkv[:, 2 * QK_A:]

    small = proj[:, OFF_SMALL:OFF_SMALL + LANES]
    beta_s[...] = _sigmoid(small)
    g_full = -jnp.exp(vec(ROW_ALOG, LANES)) * _softplus(small + vec(ROW_DTB, LANES))
    gcum_s[...] = _mm_mask_lhs(m_cum, g_full, 2)
    gtot_s[...] = _mm_mask_lhs(m_tot, g_full, 2)

    bbuf_s[:, HIST:, :] = proj[:, OFF_XB:OFF_XB + W_B].reshape(n_seq, tt, W_B)
    xc = (_conv_from_buffer(bbuf_s, vec_ref, ROW_CONV_B, CONV_B, tt, slice(0, W_B)).reshape(rows, W_B)
          + vec(ROW_CONV_B_BIAS, W_B))
    bbuf_s[:, b_tail, :] = bbuf_s[:, tt + HIST - (CONV_B - 1):tt + HIST, :]
    gate_r = _sigmoid(_mm(xc, wr_ref[...]) + vec(ROW_BR, W_B))
    gate_i = _sigmoid(_mm(xc, wi_ref[...]) + vec(ROW_BI, W_B))
    log_a = -LRU_C * gate_r * _softplus(-vec(ROW_LAM, W_B))
    a_t = jnp.exp(log_a)
    b_t = jnp.sqrt(-jnp.tanh(log_a) * (a_t * a_t + 1.0)) * (gate_i * xc)
    i_seq = _iota((rows, W_B), 0) & (tt - 1)
    if n_seq == 1:
        h_prev = jnp.broadcast_to(h_s[...], (rows, W_B))
    else:
        h_prev = jnp.concatenate([jnp.broadcast_to(h_s[s:s + 1, :], (tt, W_B)) for s in range(n_seq)], axis=0)
    b_t = b_t + jnp.where(i_seq == 0, a_t * h_prev, 0.0)
    d = 1
    while d < tt:
        a_sh = pltpu.roll(a_t, d, axis=0)
        b_sh = pltpu.roll(b_t, d, axis=0)
        ok = i_seq >= d
        b_t = jnp.where(ok, a_t * b_sh + b_t, b_t)
        a_t = jnp.where(ok, a_t * a_sh, a_t)
        d *= 2
    if n_seq == 1:
        h_s[...] = b_t[rows - 1:rows]
    else:
        h_s[...] = jnp.concatenate([b_t[(s + 1) * tt - 1:(s + 1) * tt] for s in range(n_seq)], axis=0)
    heads_s[:, W_A:W_A + W_B] = b_t * _gelu_tanh(proj[:, OFF_GB:OFF_GB + W_B])

    qc = proj[:, OFF_QC:OFF_QC + QKC_PAD] * (DK_C ** -0.5)
    kc = proj[:, OFF_KC:OFF_KC + QKC_PAD]
    logf = -_softplus(-(_mm(small, w2_ref[...]) + vec(ROW_B2, QKC_PAD))) * (1.0 / GLA_TAU)
    b_cum = _mm_mask_lhs(m_cum, logf, 2)
    b_tot = _mm_mask_lhs(m_tot, logf, 2)
    qc_s[...] = qc
    kc_s[...] = kc
    vc_s[...] = proj[:, OFF_VC:OFF_VC + W_C]
    qb_s[...] = qc * jnp.exp(b_cum)
    kdec_s[...] = kc * jnp.exp(b_tot - b_cum)
    btot_s[...] = b_tot
    logf_bf = logf.astype(BF16)
    for lvl in range(n_levels):
        qt_s[lvl] = qc * jnp.exp(_bdot(masks_ref[2 + 2 * lvl], logf_bf))
        kt_s[lvl] = kc * jnp.exp(_bdot(masks_ref[3 + 2 * lvl], logf_bf))

    s_i = _iota((STACK, STACK), 0)
    s_j = _iota((STACK, STACK), 1)
    same_blk = (s_i >> lc) == (s_j >> lc)
    m_incl = same_blk & (s_j <= s_i)
    m_strict = same_blk & (s_j < s_i)
    eye = jnp.where(s_i == s_j, 1.0, 0.0)
    lane = _iota((GROUP_ROWS, LANES), 1)
    m0 = lane < HALF
    norm_a = vec(ROW_NORM_A, LANES)
    norm_c = vec(ROW_NORM_C, LANES)

    def delta_setup(gi):
        rs = pl.ds(gi * GROUP_ROWS, GROUP_ROWS)
        beta_g = beta_s[rs, :]
        gcum_g = gcum_s[rs, :]
        gtot_g = gtot_s[rs, :]
        for p in range(N_PAIRS):
            sl = pl.ds(p * LANES, LANES)
            q_p = qa_s[rs, sl]
            k_p = ka_s[rs, sl]
            v0, v1 = _dup_values(va_s[rs, sl])
            q_st = _stack_rows(n_sb, c, [jnp.where(m0, q_p, 0.0), jnp.where(m0, 0.0, q_p)])
            k_st = _stack_rows(n_sb, c, [jnp.where(m0, k_p, 0.0), jnp.where(m0, 0.0, k_p)])
            v_st = _stack_rows(n_sb, c, [v0, v1])
            beta_c = _col_bcast_stack(beta_g, LANE_BETA + 2 * p, n_sb, c)
            g_c = _col_bcast_stack(gcum_g, LANE_DECAY + 2 * p, n_sb, c)
            gt_c = _col_bcast_stack(gtot_g, LANE_DECAY + 2 * p, n_sb, c)
            dec = jnp.exp(jnp.where(m_incl, g_c - g_c.T, -1e30))
            kk = _mm_nt(k_st, k_st)
            qk = _mm_nt(q_st, k_st)
            n_mat = -(beta_c * kk * jnp.where(m_strict, dec, 0.0))
            eg = jnp.exp(g_c)
            t_s[gi, p] = eye + n_mat
            p_s[gi, p] = n_mat.astype(BF16)
            rhs_s[gi, p] = jnp.concatenate([beta_c * v_st, beta_c * eg * k_st], axis=1).astype(BF16)
            qe_s[gi, p] = (eg * q_st).astype(BF16)
            qkd_s[gi, p] = (qk * dec).astype(BF16)
            kdst_s[gi, p] = k_st * jnp.exp(gt_c - g_c)
            gtst_s[gi, p] = gt_c

    def inverse_level(gi, j):
        for p in range(N_PAIRS):
            pw = p_s[gi, p]
            if j >= 2:
                t_old = t_s[gi, p]
                t_s[gi, p] = t_old + _bdot(t_old.astype(BF16), pw)
            p_s[gi, p] = _bdot(pw, pw).astype(BF16)

    def delta_solve(gi):
        for p in range(N_PAIRS):
            t_old = t_s[gi, p]
            t_fin = t_old + _bdot(t_old.astype(BF16), p_s[gi, p])
            uwk = _bdot(t_fin.astype(BF16), rhs_s[gi, p])
            u_s[gi, p] = uwk[:, 0:LANES]
            wk_s[gi, p] = uwk[:, LANES:].astype(BF16)

    def gla_precompute(gi):
        rs = pl.ds(gi * GROUP_ROWS, GROUP_ROWS)
        for p in range(N_PAIRS):
            sl = pl.ds(p * LANES, LANES)
            slab = pl.ds((p // 2) * LANES, LANES)
            l0 = 2 * DK_C * (p % 2)
            mh0 = (lane >= l0) & (lane < l0 + DK_C)
            mh1 = (lane >= l0 + DK_C) & (lane < l0 + 2 * DK_C)

            def stack_c(ref, idx=None):
                val = ref[rs, slab] if idx is None else ref[idx, rs, slab]
                return _stack_rows(n_sb, c, [jnp.where(mh0, val, 0.0), jnp.where(mh1, val, 0.0)])

            att = jnp.where(s_i == s_j, _mm_nt(stack_c(qc_s), stack_c(kc_s)), 0.0)
            for lvl in range(n_levels):
                lh = lc - 1 - lvl
                valid = (((s_i >> (lh + 1)) == (s_j >> (lh + 1)))
                         & (((s_i >> lh) & 1) == 1) & (((s_j >> lh) & 1) == 0))
                att = att + jnp.where(valid, _mm_nt(stack_c(qt_s, lvl), stack_c(kt_s, lvl)), 0.0)
            vc0, vc1 = _dup_values(vc_s[rs, sl])
            vc_st = _stack_rows(n_sb, c, [vc0, vc1]).astype(BF16)
            av_s[gi, p] = _bdot(att.astype(BF16), vc_st)
            qbst_s[gi, p] = stack_c(qb_s).astype(BF16)
            kd_st = stack_c(kdec_s)
            for sb in range(n_sb):
                blk = slice(sb * 2 * c, (sb + 1) * 2 * c)
                gk_s[gi, p, sb] = _bdot(kd_st[blk].T.astype(BF16), vc_st[blk])

    for gi in range(n_groups):
        delta_setup(gi)
    for j in range(1, lc):
        for gi in range(n_groups):
            inverse_level(gi, j)
        if j <= n_groups:
            gla_precompute(j - 1)
    for gi in range(lc - 1, n_groups):
        gla_precompute(gi)
    for gi in range(n_groups):
        delta_solve(gi)

    def recurrence_body(gi, carry):
        r0 = pl.multiple_of(gi * GROUP_ROWS, GROUP_ROWS)
        rs = pl.ds(r0, GROUP_ROWS)
        for p in range(N_PAIRS):
            sl = pl.ds(p * LANES, LANES)
            u = u_s[gi, p]
            wk = wk_s[gi, p]
            qe = qe_s[gi, p]
            k_dec = kdst_s[gi, p]
            gt_c = gtst_s[gi, p]
            w_parts, qs_parts = [], []
            for sb in range(n_sb):
                seq = 0 if n_seq == 1 else gi * n_sb + sb
                blk = slice(sb * 2 * c, (sb + 1) * 2 * c)
                s_old = sd_s[seq, p]
                s_bf = s_old.astype(BF16)
                w_sb = u[blk] - _bdot(wk[blk], s_bf)
                w_parts.append(w_sb)
                qs_parts.append(_bdot(qe[blk], s_bf))
                decay_rows = jnp.concatenate(
                    [jnp.broadcast_to(gt_c[sb * 2 * c + a * c:sb * 2 * c + a * c + 1, :], (GROUP_ROWS, LANES))
                     for a in range(2)], axis=0)
                sd_s[seq, p] = jnp.exp(decay_rows) * s_old + _bdot(k_dec[blk].T.astype(BF16), w_sb.astype(BF16))
            w_all = w_parts[0] if n_sb == 1 else jnp.concatenate(w_parts, axis=0)
            qs_all = qs_parts[0] if n_sb == 1 else jnp.concatenate(qs_parts, axis=0)
            o_st = qs_all + _bdot(qkd_s[gi, p], w_all.astype(BF16))
            o_st = o_st * lax.rsqrt(jnp.mean(o_st * o_st, axis=-1, keepdims=True) + EPS) * norm_a
            heads_s[rs, sl] = _unstack_rows(o_st, n_sb, c)
            slab = pl.ds((p // 2) * LANES, LANES)
            qb_st = qbst_s[gi, p]
            btot_g = btot_s[rs, slab]
            oi_parts = []
            for sb in range(n_sb):
                seq = 0 if n_seq == 1 else gi * n_sb + sb
                blk = slice(sb * 2 * c, (sb + 1) * 2 * c)
                s_old = sg_s[seq, p]
                oi_parts.append(_bdot(qb_st[blk], s_old.astype(BF16)))
                decay_col = jnp.broadcast_to(btot_g[sb * c:sb * c + 1, :], (LANES, LANES)).T
                sg_s[seq, p] = jnp.exp(decay_col) * s_old + gk_s[gi, p, sb]
            oi_all = oi_parts[0] if n_sb == 1 else jnp.concatenate(oi_parts, axis=0)
            oc_st = oi_all + av_s[gi, p]
            oc_st = oc_st * lax.rsqrt(jnp.mean(oc_st * oc_st, axis=-1, keepdims=True) + EPS) * norm_c
            heads_s[rs, pl.ds(W_A + W_B + p * LANES, LANES)] = _unstack_rows(oc_st, n_sb, c)
        return carry

    if n_groups == 1:
        recurrence_body(0, 0)
    else:
        lax.fori_loop(0, n_groups, recurrence_body, 0)

    za = proj[:, OFF_ZA:OFF_ZA + W_A]
    zc = proj[:, OFF_ZC:OFF_ZC + W_C]
    heads_s[:, 0:W_A] = heads_s[:, 0:W_A] * _silu(za)
    heads_s[:, W_A + W_B:] = heads_s[:, W_A + W_B:] * _silu(zc)
    y = ALPHA * x + _mm(heads_s[...], w_out_ref[...])
    x1_ref[...] = _layer_norm(y, vec(ROW_LN1_G, D_MODEL), vec(ROW_LN1_B, D_MODEL)).reshape(n_seq, tt, D_MODEL)

    @pl.when(t_idx == n_t - 1)
    def _():
        dconv_out_ref[...] = abuf_s[:, a_tail, :]
        lconv_out_ref[...] = bbuf_s[:, b_tail, :]
        h_out_ref[...] = h_s[...].reshape(n_seq, 1, W_B)
        for s in range(n_seq):
            for p in range(N_PAIRS):
                sd_out_ref[s, p] = sd_s[s, p][:, 0:HALF]
                r0 = GROUP_ROWS * (p % 2)
                sg_out_ref[s, 2 * DK_C * p:2 * DK_C * (p + 1), :] = sg_s[s, p][r0:r0 + GROUP_ROWS, 0:HALF]


def _layer_spec(arr, layer):
    nd = arr.ndim - 1
    return pl.BlockSpec((None,) + arr.shape[1:], lambda b, t: (layer,) + (0,) * nd)


def _mixer_call(x, states, w, layer, *, n_seq, tt, c):
    nb, t_total, _ = x.shape
    assert nb % n_seq == 0 and t_total % tt == 0
    assert (n_seq == 1 and tt % GROUP_ROWS == 0 and c == GROUP_ROWS) or (tt == HIST and c == HIST)
    rows = n_seq * tt
    assert rows % GROUP_ROWS == 0
    grid = (nb // n_seq, t_total // tt)
    n_levels = _log2(c)
    n_groups = rows // GROUP_ROWS
    has_state = states is not None
    masks = _tile_masks(rows, c)

    state_shapes = ((CONV_A - 1, CONV_A_WIDTH), (N_PAIRS, 2 * DK_A, DV_A), (CONV_B - 1, W_B), (1, W_B),
                    (QK_C, DV_C))

    def seq_block(tail):
        return (n_seq,) + tail, (0,) * len(tail)

    in_specs = [pl.BlockSpec((n_seq, tt, D_MODEL), lambda b, t: (b, t, 0))]
    operands = [x]
    if has_state:
        for arr, tail in zip(states, state_shapes):
            assert arr.shape[2:] == tail
            in_specs.append(pl.BlockSpec((None, n_seq) + tail, lambda b, t, z=(0,) * len(tail): (layer, b) + z))
            operands.append(arr)
    weights = (w['w_in'], w['mix_vec'], w['wr'], w['wi'], w['w2'], w['w_out'])
    in_specs.append(pl.BlockSpec(masks.shape, lambda b, t: (0, 0, 0)))
    operands.append(masks)
    for a in weights:
        in_specs.append(_layer_spec(a, layer))
        operands.append(a)

    out_shape = [jax.ShapeDtypeStruct((nb, t_total, D_MODEL), F32)]
    out_specs = [pl.BlockSpec((n_seq, tt, D_MODEL), lambda b, t: (b, t, 0))]
    for tail in state_shapes:
        out_shape.append(jax.ShapeDtypeStruct((nb,) + tail, F32))
        out_specs.append(pl.BlockSpec((n_seq,) + tail, lambda b, t, z=(0,) * len(tail): (b,) + z))

    scratch = [
        pltpu.VMEM((n_seq, N_PAIRS, STACK, LANES), F32),
        pltpu.VMEM((n_seq, N_PAIRS, LANES, LANES), F32),
        pltpu.VMEM((n_seq, HIST + tt, CONV_A_WIDTH), F32),
        pltpu.VMEM((n_seq, HIST + tt, W_B), F32),
        pltpu.VMEM((n_seq, W_B), F32),
        pltpu.VMEM((rows, D_MIX), F32),
        pltpu.VMEM((rows, QK_A), F32),
        pltpu.VMEM((rows, QK_A), F32),
        pltpu.VMEM((rows, W_A), F32),
        pltpu.VMEM((rows, LANES), F32),
        pltpu.VMEM((rows, LANES), F32),
        pltpu.VMEM((rows, LANES), F32),
        pltpu.VMEM((rows, QKC_PAD), F32),
        pltpu.VMEM((rows, QKC_PAD), F32),
        pltpu.VMEM((rows, W_C), F32),
        pltpu.VMEM((rows, QKC_PAD), F32),
        pltpu.VMEM((rows, QKC_PAD), F32),
        pltpu.VMEM((rows, QKC_PAD), F32),
        pltpu.VMEM((n_levels, rows, QKC_PAD), F32),
        pltpu.VMEM((n_levels, rows, QKC_PAD), F32),
        pltpu.VMEM((n_groups, N_PAIRS, STACK, STACK), F32),
        pltpu.VMEM((n_groups, N_PAIRS, STACK, STACK), BF16),
        pltpu.VMEM((n_groups, N_PAIRS, STACK, 2 * LANES), BF16),
        pltpu.VMEM((n_groups, N_PAIRS, STACK, LANES), BF16),
        pltpu.VMEM((n_groups, N_PAIRS, STACK, STACK), BF16),
        pltpu.VMEM((n_groups, N_PAIRS, STACK, LANES), F32),
        pltpu.VMEM((n_groups, N_PAIRS, STACK, LANES), F32),
        pltpu.VMEM((n_groups, N_PAIRS, STACK, LANES), F32),
        pltpu.VMEM((n_groups, N_PAIRS, STACK, LANES), BF16),
        pltpu.VMEM((n_groups, N_PAIRS, STACK, LANES), F32),
        pltpu.VMEM((n_groups, N_PAIRS, STACK, LANES), BF16),
        pltpu.VMEM((n_groups, N_PAIRS, GROUP_ROWS // c, LANES, LANES), F32),
    ]
    kern = functools.partial(_mixer_kernel, n_seq=n_seq, tt=tt, c=c, has_state=has_state)
    return pl.pallas_call(
        kern,
        grid=grid,
        in_specs=in_specs,
        out_specs=out_specs,
        out_shape=out_shape,
        scratch_shapes=scratch,
        compiler_params=pltpu.CompilerParams(
            dimension_semantics=("arbitrary", "arbitrary"), vmem_limit_bytes=VMEM_LIMIT_BYTES),
        name=f"mixer_nseq{n_seq}_tt{tt}",
    )(*operands)


def _ffn_kernel(*refs, n_seq, tt, has_state):
    n_state = 1 if has_state else 0
    x_ref = refs[0]
    (w_up_ref, vec_ref, wd_ref) = refs[1 + n_state:4 + n_state]
    (y_ref, tail_ref) = refs[4 + n_state:6 + n_state]
    (gbuf_s, h_s) = refs[6 + n_state:]
    t_idx = pl.program_id(1)
    n_t = pl.num_programs(1)
    rows = n_seq * tt
    tail = slice(HIST - (CONV_F - 1), HIST)

    @pl.when(t_idx == 0)
    def _():
        if has_state:
            gbuf_s[:, tail, :] = refs[1][...]
        else:
            gbuf_s[:, tail, :] = jnp.zeros((n_seq, CONV_F - 1, D_FF), F32)

    x = x_ref[...].reshape(rows, D_MODEL)
    xb = x.astype(BF16)
    for j in range(N_FF_CHUNKS):
        cols = slice(j * FF_CHUNK, (j + 1) * FF_CHUNK)
        gate = _bdot(xb, w_up_ref[:, cols])
        val = _bdot(xb, w_up_ref[:, D_FF + j * FF_CHUNK:D_FF + (j + 1) * FF_CHUNK])
        gbuf_s[:, HIST:, cols] = gate.reshape(n_seq, tt, FF_CHUNK)
        conv = _conv_from_buffer(gbuf_s, vec_ref, ROW_CONV_F, CONV_F, tt, cols).reshape(rows, FF_CHUNK)
        h = _gelu_tanh(conv + vec_ref[ROW_CONV_F_BIAS:ROW_CONV_F_BIAS + 1, cols]) * val
        h_s[:, cols] = h.astype(BF16)
    gbuf_s[:, tail, :] = gbuf_s[:, tt + HIST - (CONV_F - 1):tt + HIST, :]
    y = ALPHA * x + _bdot(h_s[...], wd_ref[...])
    ln_g = vec_ref[ROW_LN2_G:ROW_LN2_G + 1, 0:D_MODEL]
    ln_b = vec_ref[ROW_LN2_B:ROW_LN2_B + 1, 0:D_MODEL]
    y_ref[...] = _layer_norm(y, ln_g, ln_b).reshape(n_seq, tt, D_MODEL)

    @pl.when(t_idx == n_t - 1)
    def _():
        tail_ref[...] = gbuf_s[:, tail, :]


def _ffn_call(x, state, w, layer, *, n_seq, tt):
    nb, t_total, _ = x.shape
    rows = n_seq * tt
    grid = (nb // n_seq, t_total // tt)
    has_state = state is not None
    tail_shape = (CONV_F - 1, D_FF)
    in_specs = [pl.BlockSpec((n_seq, tt, D_MODEL), lambda b, t: (b, t, 0))]
    operands = [x]
    if has_state:
        in_specs.append(pl.BlockSpec((None, n_seq) + tail_shape, lambda b, t: (layer, b, 0, 0)))
        operands.append(state)
    for a in (w['w_up'], w['ffn_vec'], w['w_down']):
        in_specs.append(_layer_spec(a, layer))
        operands.append(a)
    out_shape = (jax.ShapeDtypeStruct((nb, t_total, D_MODEL), F32),
                 jax.ShapeDtypeStruct((nb,) + tail_shape, F32))
    out_specs = (pl.BlockSpec((n_seq, tt, D_MODEL), lambda b, t: (b, t, 0)),
                 pl.BlockSpec((n_seq,) + tail_shape, lambda b, t: (b, 0, 0)))
    scratch = [pltpu.VMEM((n_seq, HIST + tt, D_FF), F32), pltpu.VMEM((rows, D_FF), BF16)]
    kern = functools.partial(_ffn_kernel, n_seq=n_seq, tt=tt, has_state=has_state)
    return pl.pallas_call(
        kern,
        grid=grid,
        in_specs=in_specs,
        out_specs=out_specs,
        out_shape=out_shape,
        scratch_shapes=scratch,
        compiler_params=pltpu.CompilerParams(
            dimension_semantics=("arbitrary", "arbitrary"), vmem_limit_bytes=VMEM_LIMIT_BYTES),
        name=f"ffn_nseq{n_seq}_tt{tt}",
    )(*operands)


def _vec_table(pieces, width, n_rows):
    rows = []
    used = 0
    for arr, lane0 in pieces:
        arr = arr.astype(F32)
        rows.append(jnp.pad(arr, ((0, 0), (0, 0), (lane0, width - lane0 - arr.shape[2]))))
        used += arr.shape[1]
    depth = pieces[0][0].shape[0]
    rows.append(jnp.zeros((depth, n_rows - used, width), F32))
    return jnp.concatenate(rows, axis=1)


def _prep_weights(w_in, conv_a_w, a_log, dt_bias, norm_a_w, conv_b_w, conv_b_b, lru_w_r, lru_b_r, lru_w_i,
                  lru_b_i, lru_lambda, gla_w2, gla_b2, norm_c_w, w_out, ln1_g, ln1_b, ffn_w_up, ffn_conv_w,
                  ffn_conv_b, ffn_w_down, ln2_g, ln2_b):
    depth, d, _ = w_in.shape
    pts = [0]
    for s in (QK_A, QK_A, W_A, W_A, H_A, H_A, W_B, W_B, QK_C, QK_C, W_C, W_C, GLA_RANK):
        pts.append(pts[-1] + s)
    (qa, ka, va, za, ba, aa, xb, gb, qc, kc, vc, zc, lc) = [w_in[:, :, pts[i]:pts[i + 1]] for i in range(13)]
    z = lambda n: jnp.zeros((depth, d, n), w_in.dtype)
    small = jnp.concatenate([ba, z(LANE_DECAY - H_A), aa, z(LANE_LOWRANK - LANE_DECAY - H_A), lc,
                             z(LANES - LANE_LOWRANK - GLA_RANK)], axis=2)
    w_in_p = jnp.concatenate([qa, ka, va, za, xb, gb, qc, z(QKC_PAD - QK_C), kc, z(QKC_PAD - QK_C), vc, zc, small],
                             axis=2).astype(BF16)
    assert w_in_p.shape[2] == D_IN_PAD

    eye_blocks = jnp.eye(LRU_BLOCKS, dtype=F32)[None, :, None, :, None]

    def block_diag(wb):
        return (wb[:, :, :, None, :] * eye_blocks).reshape(depth, W_B, W_B).astype(BF16)

    w2 = jnp.pad(gla_w2, ((0, 0), (LANE_LOWRANK, LANES - LANE_LOWRANK - GLA_RANK), (0, QKC_PAD - QK_C))).astype(BF16)
    r1 = lambda v: v[:, None, :]
    dup = lambda v: jnp.concatenate([v, v], axis=1)[:, None, :]
    mix_vec = _vec_table(
        [(conv_a_w, 0), (conv_b_w, 0), (r1(conv_b_b), 0), (r1(a_log), LANE_DECAY), (r1(dt_bias), LANE_DECAY),
         (dup(norm_a_w), 0), (r1(lru_b_r), 0), (r1(lru_b_i), 0), (r1(lru_lambda), 0), (r1(gla_b2), 0),
         (dup(norm_c_w), 0), (r1(ln1_g), 0), (r1(ln1_b), 0)], VEC_W, MIX_VEC_ROWS)
    ffn_vec = _vec_table([(ffn_conv_w, 0), (r1(ffn_conv_b), 0), (r1(ln2_g), 0), (r1(ln2_b), 0)], D_FF, FFN_VEC_ROWS)
    return {
        'w_in': w_in_p, 'mix_vec': mix_vec, 'wr': block_diag(lru_w_r), 'wi': block_diag(lru_w_i), 'w2': w2,
        'w_out': w_out.astype(BF16), 'w_up': ffn_w_up.astype(BF16), 'ffn_vec': ffn_vec,
        'w_down': ffn_w_down.astype(BF16),
    }


def _trunk(x, states, w, *, n_seq, tt, c):
    nb = x.shape[0]
    mix_states = ffn_state = None
    if states is not None:
        st_dconv, st_delta, st_lconv, st_lru, st_gla, st_fconv = (s.astype(F32) for s in states)
        mix_states = (st_dconv, st_delta.reshape(DEPTH, nb, N_PAIRS, 2 * DK_A, DV_A), st_lconv,
                      st_lru.reshape(DEPTH, nb, 1, W_B), st_gla.reshape(DEPTH, nb, QK_C, DV_C))
        ffn_state = st_fconv
    outs = [[] for _ in range(6)]
    for l in range(DEPTH):
        x1, dconv, sd, lconv, h, sg = _mixer_call(x, mix_states, w, l, n_seq=n_seq, tt=tt, c=c)
        x, fconv = _ffn_call(x1, ffn_state, w, l, n_seq=n_seq, tt=tt)
        for o, v in zip(outs, (dconv, sd.reshape(nb, H_A, DK_A, DV_A), lconv, h.reshape(nb, W_B),
                               sg.reshape(nb, H_C, DK_C, DV_C), fconv)):
            o.append(v)
    return x, tuple(jnp.stack(o) for o in outs)


def kernel(x_prompt, x_sample, state_delta_conv, state_delta, state_lru_conv, state_lru, state_gla, state_ffn_conv,
           w_in, conv_a_w, a_log, dt_bias, norm_a_w, conv_b_w, conv_b_b, lru_w_r, lru_b_r, lru_w_i, lru_b_i,
           lru_lambda, gla_w2, gla_b2, norm_c_w, w_out, ln1_g, ln1_b, ffn_w_up, ffn_conv_w, ffn_conv_b, ffn_w_down,
           ln2_g, ln2_b):
    assert w_in.shape[0] == DEPTH
    w = _prep_weights(w_in, conv_a_w, a_log, dt_bias, norm_a_w, conv_b_w, conv_b_b, lru_w_r, lru_b_r, lru_w_i,
                      lru_b_i, lru_lambda, gla_w2, gla_b2, norm_c_w, w_out, ln1_g, ln1_b, ffn_w_up, ffn_conv_w,
                      ffn_conv_b, ffn_w_down, ln2_g, ln2_b)
    sample_states = (state_delta_conv, state_delta, state_lru_conv, state_lru, state_gla, state_ffn_conv)
    t_p = x_prompt.shape[1]
    t_s = x_sample.shape[1]
    y_p, p_st = _trunk(x_prompt, None, w, n_seq=1, tt=min(PROMPT_TT, t_p), c=CHUNK)
    y_s, s_st = _trunk(x_sample, sample_states, w, n_seq=min(SAMPLE_NSEQ, x_sample.shape[0]), tt=t_s, c=t_s)
    return (y_p, y_s) + p_st + s_st
```

```python
import functools
import math

import numpy as np
import jax
import jax.numpy as jnp
from jax import lax
from jax.experimental import pallas as pl
from jax.experimental.pallas import tpu as pltpu

F32 = jnp.float32
BF16 = jnp.bfloat16

D_MODEL = 1024
H_A, DK_A, DV_A = 6, 64, 64
QK_A = H_A * DK_A
W_A = H_A * DV_A
CONV_A = 4
W_B = 256
LRU_BLOCKS = 4
LRU_BLOCK = W_B // LRU_BLOCKS
CONV_B = 4
LRU_C = 8.0
H_C, DK_C, DV_C = 6, 32, 64
QK_C = H_C * DK_C
W_C = H_C * DV_C
GLA_RANK = 16
GLA_TAU = 16.0
D_MIX = W_A + W_B + W_C
CHUNK = 64
D_FF = 2816
CONV_F = 3
EPS = 1e-6
CONV_A_WIDTH = 2 * QK_A + W_A
DEPTH = 2
ALPHA = (2.0 * DEPTH) ** 0.25

LANES = 128
SUBLANES = 8
HALF = LANES // 2
GROUP_ROWS = 64
STACK = 2 * GROUP_ROWS
N_PAIRS = 3
HIST = SUBLANES
FF_CHUNK = 256
N_FF_CHUNKS = D_FF // FF_CHUNK
VMEM_LIMIT_BYTES = 56 * 1024 * 1024
PROMPT_TT = 256
FFN_PROMPT_TT = 512
SAMPLE_NSEQ = 16

OFF_QA = 0
OFF_KA = OFF_QA + QK_A
OFF_VA = OFF_KA + QK_A
OFF_ZA = OFF_VA + W_A
OFF_XB = OFF_ZA + W_A
OFF_GB = OFF_XB + W_B
OFF_QC = OFF_GB + W_B
QKC_PAD = 2 * LANES
OFF_KC = OFF_QC + QKC_PAD
OFF_VC = OFF_KC + QKC_PAD
OFF_ZC = OFF_VC + W_C
OFF_SMALL = OFF_ZC + W_C
D_IN_PAD = OFF_SMALL + LANES
LANE_BETA = 0
LANE_DECAY = 8
LANE_LOWRANK = 16

VEC_W = CONV_A_WIDTH
(ROW_CONV_A, ROW_CONV_B, ROW_CONV_B_BIAS, ROW_ALOG, ROW_DTB, ROW_NORM_A, ROW_BR, ROW_BI, ROW_LAM, ROW_B2,
 ROW_NORM_C, ROW_LN1_G, ROW_LN1_B) = (0, 4, 8, 9, 10, 11, 12, 13, 14, 15, 16, 17, 18)
MIX_VEC_ROWS = 24
(ROW_CONV_F, ROW_CONV_F_BIAS, ROW_LN2_G, ROW_LN2_B) = (0, 3, 4, 5)
FFN_VEC_ROWS = 8


def _bdot(a, b):
    return jnp.dot(a, b, preferred_element_type=F32)


def _mm(a, b):
    return _bdot(a.astype(BF16), b.astype(BF16))


def _mm_nt(a, b):
    return lax.dot_general(a.astype(BF16), b.astype(BF16), (((1,), (1,)), ((), ())), preferred_element_type=F32)


def _split(a, n):
    parts = []
    r = a
    for i in range(n):
        p = r.astype(BF16)
        parts.append(p)
        if i + 1 < n:
            r = r - p.astype(F32)
    return parts


def _mm_mask_lhs(mask_bf16, a, n):
    out = None
    for p in _split(a, n):
        t = _bdot(mask_bf16, p)
        out = t if out is None else out + t
    return out


def _softplus(x):
    return jnp.maximum(x, 0.0) + jnp.log1p(jnp.exp(-jnp.abs(x)))


def _sigmoid(x):
    return 0.5 * jnp.tanh(0.5 * x) + 0.5


def _silu(x):
    h = 0.5 * x
    return h * jnp.tanh(h) + h


def _gelu_tanh(x):
    return 0.5 * x * (1.0 + jnp.tanh(math.sqrt(2.0 / math.pi) * (x + 0.044715 * (x * x * x))))


def _layer_norm(x, g, b):
    mu = jnp.mean(x, axis=-1, keepdims=True)
    xc = x - mu
    var = jnp.mean(xc * xc, axis=-1, keepdims=True)
    return xc * lax.rsqrt(var + EPS) * g + b


def _iota(shape, axis):
    return lax.broadcasted_iota(jnp.int32, shape, axis)


def _log2(n):
    l = int(math.log2(n))
    assert (1 << l) == n
    return l


def _conv_from_buffer(buf_ref, vec_ref, row0, width, tt, cols):
    y = None
    for d in range(width):
        tap = vec_ref[row0 + width - 1 - d:row0 + width - d, cols]
        term = buf_ref[:, HIST - d:HIST - d + tt, cols] * tap
        y = term if y is None else y + term
    return y


def _stack_rows(n_sb, c, pieces):
    if n_sb == 1:
        return jnp.concatenate([pieces[0], pieces[1]], axis=0)
    out = []
    for s in range(n_sb):
        for a in range(2):
            out.append(pieces[a][s * c:(s + 1) * c])
    return jnp.concatenate(out, axis=0)


def _unstack_rows(o, n_sb, c):
    lo = _iota((c, LANES), 1) < HALF
    out = []
    for s in range(n_sb):
        base = s * 2 * c
        out.append(jnp.where(lo, o[base:base + c], o[base + c:base + 2 * c]))
    return out[0] if n_sb == 1 else jnp.concatenate(out, axis=0)


def _col_bcast_stack(vals, lane0, n_sb, c):
    pieces = [jnp.broadcast_to(vals[:, lane0 + a:lane0 + a + 1], (GROUP_ROWS, LANES)) for a in range(2)]
    return _stack_rows(n_sb, c, pieces)


def _dup_values(v):
    lo = _iota(v.shape, 1) < HALF
    vr = pltpu.roll(v, HALF, axis=1)
    return jnp.where(lo, v, vr), jnp.where(lo, vr, v)


def _tile_masks(rows, c):
    lc = _log2(c)
    i = np.arange(rows)[:, None]
    j = np.arange(rows)[None, :]
    same = (i >> lc) == (j >> lc)
    out = [same & (j <= i), same]
    for lvl in range(lc):
        lh = lc - 1 - lvl
        hi = (i >> lh) & 1
        start = (i >> lh) << lh
        nxt = ((i >> lh) + 1) << lh
        out.append((hi == 1) & (j >= start) & (j <= i))
        out.append((hi == 0) & (j > i) & (j < nxt))
    return jnp.asarray(np.stack(out).astype(np.float32), dtype=BF16)


N_MIXER_WEIGHTS = 7
N_MIXER_OUTS = 5
N_MIXER_SCRATCH = 34


def _mixer_stage(stage, x_ref, state_refs, weight_refs, out_refs, scratch_refs, write_x1, *, n_seq, tt, c):
    has_state = len(state_refs) > 0
    (masks_ref, w_in_ref, vec_ref, wr_ref, wi_ref, w2_ref, w_out_ref) = weight_refs
    (dconv_out_ref, sd_out_ref, lconv_out_ref, h_out_ref, sg_out_ref) = out_refs
    (sd_s, sg_s, abuf_s, bbuf_s, h_s, heads_s,
     qa_s, ka_s, va_s, beta_s, gcum_s, gtot_s,
     qc_s, kc_s, vc_s, qb_s, kdec_s, btot_s, qt_s, kt_s,
     t_s, p_s, rhs_s, qe_s, qkd_s, kdst_s, gtst_s, oprime_s, qprime_s, c_s, kw_s, av_s, qbst_s, gk_s,
     ) = scratch_refs

    rows = n_seq * tt
    n_groups = rows // GROUP_ROWS
    n_sb = GROUP_ROWS // c
    lc = _log2(c)
    n_levels = lc
    a_tail = slice(HIST - (CONV_A - 1), HIST)
    b_tail = slice(HIST - (CONV_B - 1), HIST)

    if stage == 'init':
        if has_state:
            dconv_ref, sd_ref, lconv_ref, h0_ref, sg_ref = state_refs
            abuf_s[:, a_tail, :] = dconv_ref[...]
            bbuf_s[:, b_tail, :] = lconv_ref[...]
            h_s[...] = h0_ref[...].reshape(n_seq, W_B)
            zeros = jnp.zeros((GROUP_ROWS, LANES), F32)
            for s in range(n_seq):
                for p in range(N_PAIRS):
                    sd = sd_ref[s, p]
                    sd_s[s, p] = jnp.concatenate([sd, sd], axis=-1)
                    sg = sg_ref[s, 2 * DK_C * p:2 * DK_C * (p + 1), :]
                    sg = jnp.concatenate([sg, sg], axis=-1)
                    sg_s[s, p] = jnp.concatenate([sg, zeros] if p % 2 == 0 else [zeros, sg], axis=0)
        else:
            abuf_s[:, a_tail, :] = jnp.zeros((n_seq, CONV_A - 1, CONV_A_WIDTH), F32)
            bbuf_s[:, b_tail, :] = jnp.zeros((n_seq, CONV_B - 1, W_B), F32)
            h_s[...] = jnp.zeros((n_seq, W_B), F32)
            sd_s[...] = jnp.zeros(sd_s.shape, F32)
            sg_s[...] = jnp.zeros(sg_s.shape, F32)
        return

    if stage == 'final':
        dconv_out_ref[...] = abuf_s[:, a_tail, :]
        lconv_out_ref[...] = bbuf_s[:, b_tail, :]
        h_out_ref[...] = h_s[...].reshape(n_seq, 1, W_B)
        for s in range(n_seq):
            for p in range(N_PAIRS):
                sd_out_ref[s, p] = sd_s[s, p][:, 0:HALF]
                r0 = GROUP_ROWS * (p % 2)
                sg_out_ref[s, 2 * DK_C * p:2 * DK_C * (p + 1), :] = sg_s[s, p][r0:r0 + GROUP_ROWS, 0:HALF]
        return

    def vec(row, width):
        return vec_ref[row:row + 1, 0:width]

    x = x_ref[...].reshape(rows, D_MODEL)
    proj = _mm(x, w_in_ref[...])
    m_cum = masks_ref[0]
    m_tot = masks_ref[1]

    abuf_s[:, HIST:, :] = proj[:, OFF_QA:OFF_QA + CONV_A_WIDTH].reshape(n_seq, tt, CONV_A_WIDTH)
    conv_a = _conv_from_buffer(abuf_s, vec_ref, ROW_CONV_A, CONV_A, tt, slice(0, CONV_A_WIDTH))
    qkv = _silu(conv_a.reshape(rows, CONV_A_WIDTH))
    abuf_s[:, a_tail, :] = abuf_s[:, tt + HIST - (CONV_A - 1):tt + HIST, :]
    p_i = _iota((LANES, LANES), 0) >> _log2(DK_A)
    p_j = _iota((LANES, LANES), 1) >> _log2(DK_A)
    pair_ones = jnp.where(p_i == p_j, 1.0, 0.0).astype(BF16)
    for p in range(N_PAIRS):
        for off, dst, scale in ((0, qa_s, DK_A ** -0.5), (QK_A, ka_s, 1.0)):
            v = qkv[:, off + p * LANES:off + (p + 1) * LANES]
            ssq = _bdot((v * v).astype(BF16), pair_ones)
            dst[:, p * LANES:(p + 1) * LANES] = v * (lax.rsqrt(ssq + EPS) * scale)
    va_s[...] = qkv[:, 2 * QK_A:]

    small = proj[:, OFF_SMALL:OFF_SMALL + LANES]
    beta_s[...] = _sigmoid(small)
    g_full = -jnp.exp(vec(ROW_ALOG, LANES)) * _softplus(small + vec(ROW_DTB, LANES))
    gcum_s[...] = _mm_mask_lhs(m_cum, g_full, 2)
    gtot_s[...] = _mm_mask_lhs(m_tot, g_full, 2)

    bbuf_s[:, HIST:, :] = proj[:, OFF_XB:OFF_XB + W_B].reshape(n_seq, tt, W_B)
    xc = (_conv_from_buffer(bbuf_s, vec_ref, ROW_CONV_B, CONV_B, tt, slice(0, W_B)).reshape(rows, W_B)
          + vec(ROW_CONV_B_BIAS, W_B))
    bbuf_s[:, b_tail, :] = bbuf_s[:, tt + HIST - (CONV_B - 1):tt + HIST, :]
    gate_r = _sigmoid(_mm(xc, wr_ref[...]) + vec(ROW_BR, W_B))
    gate_i = _sigmoid(_mm(xc, wi_ref[...]) + vec(ROW_BI, W_B))
    log_a = -LRU_C * gate_r * _softplus(-vec(ROW_LAM, W_B))
    a_t = jnp.exp(log_a)
    b_t = jnp.sqrt(-jnp.tanh(log_a) * (a_t * a_t + 1.0)) * (gate_i * xc)
    i_seq = _iota((rows, W_B), 0) & (tt - 1)
    if n_seq == 1:
        h_prev = jnp.broadcast_to(h_s[...], (rows, W_B))
    else:
        h_prev = jnp.concatenate([jnp.broadcast_to(h_s[s:s + 1, :], (tt, W_B)) for s in range(n_seq)], axis=0)
    b_t = b_t + jnp.where(i_seq == 0, a_t * h_prev, 0.0)
    d = 1
    while d < tt:
        a_sh = pltpu.roll(a_t, d, axis=0)
        b_sh = pltpu.roll(b_t, d, axis=0)
        ok = i_seq >= d
        b_t = jnp.where(ok, a_t * b_sh + b_t, b_t)
        a_t = jnp.where(ok, a_t * a_sh, a_t)
        d *= 2
    if n_seq == 1:
        h_s[...] = b_t[rows - 1:rows]
    else:
        h_s[...] = jnp.concatenate([b_t[(s + 1) * tt - 1:(s + 1) * tt] for s in range(n_seq)], axis=0)
    heads_s[:, W_A:W_A + W_B] = b_t * _gelu_tanh(proj[:, OFF_GB:OFF_GB + W_B])

    qc = proj[:, OFF_QC:OFF_QC + QKC_PAD] * (DK_C ** -0.5)
    kc = proj[:, OFF_KC:OFF_KC + QKC_PAD]
    logf = -_softplus(-(_mm(small, w2_ref[...]) + vec(ROW_B2, QKC_PAD))) * (1.0 / GLA_TAU)
    b_cum = _mm_mask_lhs(m_cum, logf, 2)
    b_tot = _mm_mask_lhs(m_tot, logf, 2)
    qc_s[...] = qc
    kc_s[...] = kc
    vc_s[...] = proj[:, OFF_VC:OFF_VC + W_C]
    qb_s[...] = qc * jnp.exp(b_cum)
    kdec_s[...] = kc * jnp.exp(b_tot - b_cum)
    btot_s[...] = b_tot
    logf_bf = logf.astype(BF16)
    for lvl in range(n_levels):
        qt_s[lvl] = qc * jnp.exp(_bdot(masks_ref[2 + 2 * lvl], logf_bf))
        kt_s[lvl] = kc * jnp.exp(_bdot(masks_ref[3 + 2 * lvl], logf_bf))

    s_i = _iota((STACK, STACK), 0)
    s_j = _iota((STACK, STACK), 1)
    same_blk = (s_i >> lc) == (s_j >> lc)
    m_incl = same_blk & (s_j <= s_i)
    m_strict = same_blk & (s_j < s_i)
    eye = jnp.where(s_i == s_j, 1.0, 0.0)
    lane = _iota((GROUP_ROWS, LANES), 1)
    m0 = lane < HALF
    norm_a = vec(ROW_NORM_A, LANES)
    norm_c = vec(ROW_NORM_C, LANES)

    def delta_setup(gi):
        rs = pl.ds(gi * GROUP_ROWS, GROUP_ROWS)
        beta_g = beta_s[rs, :]
        gcum_g = gcum_s[rs, :]
        gtot_g = gtot_s[rs, :]
        for p in range(N_PAIRS):
            sl = pl.ds(p * LANES, LANES)
            q_p = qa_s[rs, sl]
            k_p = ka_s[rs, sl]
            v0, v1 = _dup_values(va_s[rs, sl])
            q_st = _stack_rows(n_sb, c, [jnp.where(m0, q_p, 0.0), jnp.where(m0, 0.0, q_p)])
            k_st = _stack_rows(n_sb, c, [jnp.where(m0, k_p, 0.0), jnp.where(m0, 0.0, k_p)])
            v_st = _stack_rows(n_sb, c, [v0, v1])
            beta_c = _col_bcast_stack(beta_g, LANE_BETA + 2 * p, n_sb, c)
            g_c = _col_bcast_stack(gcum_g, LANE_DECAY + 2 * p, n_sb, c)
            gt_c = _col_bcast_stack(gtot_g, LANE_DECAY + 2 * p, n_sb, c)
            dec = jnp.exp(jnp.where(m_incl, g_c - g_c.T, -1e30))
            kq = _mm_nt(jnp.concatenate([k_st, q_st], axis=0), k_st)
            kk, qk = kq[0:STACK], kq[STACK:]
            n_mat = -(beta_c * kk * jnp.where(m_strict, dec, 0.0))
            eg = jnp.exp(g_c)
            t_s[gi, p] = eye + n_mat
            p_s[gi, p] = n_mat.astype(BF16)
            rhs_s[gi, p] = jnp.concatenate([beta_c * v_st, beta_c * eg * k_st], axis=1).astype(BF16)
            qe_s[gi, p] = (eg * q_st).astype(BF16)
            qkd_s[gi, p] = (qk * dec).astype(BF16)
            kdst_s[gi, p] = k_st * jnp.exp(gt_c - g_c)
            gtst_s[gi, p] = gt_c

    def inverse_level(gi, j):
        for p in range(N_PAIRS):
            pw = p_s[gi, p]
            if j >= 2:
                t_old = t_s[gi, p]
                both = _bdot(jnp.concatenate([pw, t_old.astype(BF16)], axis=0), pw)
                t_s[gi, p] = t_old + both[STACK:]
                p_s[gi, p] = both[0:STACK].astype(BF16)
            else:
                p_s[gi, p] = _bdot(pw, pw).astype(BF16)

    def delta_solve(gi):
        for p in range(N_PAIRS):
            t_old = t_s[gi, p]
            t_fin = t_old + _bdot(t_old.astype(BF16), p_s[gi, p])
            uwk = _bdot(t_fin.astype(BF16), rhs_s[gi, p]).astype(BF16)
            k_dec = kdst_s[gi, p]
            if n_sb == 1:
                both = _bdot(jnp.concatenate([qkd_s[gi, p], k_dec.T.astype(BF16)], axis=0), uwk)
                o_qw, c_kw = both[0:STACK], [both[STACK:]]
            else:
                o_qw = _bdot(qkd_s[gi, p], uwk)
                c_kw = [_bdot(k_dec[sb * 2 * c:(sb + 1) * 2 * c].T.astype(BF16), uwk[sb * 2 * c:(sb + 1) * 2 * c])
                        for sb in range(n_sb)]
            oprime_s[gi, p] = o_qw[:, 0:LANES]
            qprime_s[gi, p] = (qe_s[gi, p].astype(F32) - o_qw[:, LANES:]).astype(BF16)
            for sb in range(n_sb):
                c_s[gi, p, sb] = c_kw[sb][:, 0:LANES]
                kw_s[gi, p, sb] = c_kw[sb][:, LANES:].astype(BF16)

    def gla_precompute(gi):
        rs = pl.ds(gi * GROUP_ROWS, GROUP_ROWS)
        for p in range(N_PAIRS):
            sl = pl.ds(p * LANES, LANES)
            slab = pl.ds((p // 2) * LANES, LANES)
            l0 = 2 * DK_C * (p % 2)
            mh0 = (lane >= l0) & (lane < l0 + DK_C)
            mh1 = (lane >= l0 + DK_C) & (lane < l0 + 2 * DK_C)

            def stack_c(ref, idx=None):
                val = ref[rs, slab] if idx is None else ref[idx, rs, slab]
                return _stack_rows(n_sb, c, [jnp.where(mh0, val, 0.0), jnp.where(mh1, val, 0.0)])

            att = jnp.where(s_i == s_j, _mm_nt(stack_c(qc_s), stack_c(kc_s)), 0.0)
            for lvl in range(n_levels):
                lh = lc - 1 - lvl
                valid = (((s_i >> (lh + 1)) == (s_j >> (lh + 1)))
                         & (((s_i >> lh) & 1) == 1) & (((s_j >> lh) & 1) == 0))
                att = att + jnp.where(valid, _mm_nt(stack_c(qt_s, lvl), stack_c(kt_s, lvl)), 0.0)
            vc0, vc1 = _dup_values(vc_s[rs, sl])
            vc_st = _stack_rows(n_sb, c, [vc0, vc1]).astype(BF16)
            qbst_s[gi, p] = stack_c(qb_s).astype(BF16)
            kd_st = stack_c(kdec_s)
            if n_sb == 1:
                both = _bdot(jnp.concatenate([att.astype(BF16), kd_st.T.astype(BF16)], axis=0), vc_st)
                av_s[gi, p] = both[0:STACK]
                gk_s[gi, p, 0] = both[STACK:]
            else:
                av_s[gi, p] = _bdot(att.astype(BF16), vc_st)
                for sb in range(n_sb):
                    blk = slice(sb * 2 * c, (sb + 1) * 2 * c)
                    gk_s[gi, p, sb] = _bdot(kd_st[blk].T.astype(BF16), vc_st[blk])

    for gi in range(n_groups):
        delta_setup(gi)
    for j in range(1, lc):
        for gi in range(n_groups):
            inverse_level(gi, j)
        if j <= n_groups:
            gla_precompute(j - 1)
    for gi in range(lc - 1, n_groups):
        gla_precompute(gi)
    for gi in range(n_groups):
        delta_solve(gi)

    def recurrence_body(gi):
        rs = pl.ds(gi * GROUP_ROWS, GROUP_ROWS)
        for p in range(N_PAIRS):
            sl = pl.ds(p * LANES, LANES)
            q_prime = qprime_s[gi, p]
            gt_c = gtst_s[gi, p]
            qs_parts = []
            for sb in range(n_sb):
                seq = 0 if n_seq == 1 else gi * n_sb + sb
                blk = slice(sb * 2 * c, (sb + 1) * 2 * c)
                s_old = sd_s[seq, p]
                s_bf = s_old.astype(BF16)
                if n_sb == 1:
                    both = _bdot(jnp.concatenate([q_prime, kw_s[gi, p, 0]], axis=0), s_bf)
                    qs, kws = both[0:STACK], both[STACK:]
                else:
                    qs, kws = _bdot(q_prime[blk], s_bf), _bdot(kw_s[gi, p, sb], s_bf)
                qs_parts.append(qs)
                decay_rows = jnp.concatenate(
                    [jnp.broadcast_to(gt_c[sb * 2 * c + a * c:sb * 2 * c + a * c + 1, :], (GROUP_ROWS, LANES))
                     for a in range(2)], axis=0)
                sd_s[seq, p] = jnp.exp(decay_rows) * s_old + (c_s[gi, p, sb] - kws)
            qs_all = qs_parts[0] if n_sb == 1 else jnp.concatenate(qs_parts, axis=0)
            o_st = qs_all + oprime_s[gi, p]
            o_st = o_st * lax.rsqrt(jnp.mean(o_st * o_st, axis=-1, keepdims=True) + EPS) * norm_a
            heads_s[rs, sl] = _unstack_rows(o_st, n_sb, c)
            slab = pl.ds((p // 2) * LANES, LANES)
            qb_st = qbst_s[gi, p]
            btot_g = btot_s[rs, slab]
            oi_parts = []
            for sb in range(n_sb):
                seq = 0 if n_seq == 1 else gi * n_sb + sb
                blk = slice(sb * 2 * c, (sb + 1) * 2 * c)
                s_old = sg_s[seq, p]
                oi_parts.append(_bdot(qb_st[blk], s_old.astype(BF16)))
                decay_col = jnp.broadcast_to(btot_g[sb * c:sb * c + 1, :], (LANES, LANES)).T
                sg_s[seq, p] = jnp.exp(decay_col) * s_old + gk_s[gi, p, sb]
            oi_all = oi_parts[0] if n_sb == 1 else jnp.concatenate(oi_parts, axis=0)
            oc_st = oi_all + av_s[gi, p]
            oc_st = oc_st * lax.rsqrt(jnp.mean(oc_st * oc_st, axis=-1, keepdims=True) + EPS) * norm_c
            heads_s[rs, pl.ds(W_A + W_B + p * LANES, LANES)] = _unstack_rows(oc_st, n_sb, c)

    for gi in range(n_groups):
        recurrence_body(gi)

    za = proj[:, OFF_ZA:OFF_ZA + W_A]
    zc = proj[:, OFF_ZC:OFF_ZC + W_C]
    heads_s[:, 0:W_A] = heads_s[:, 0:W_A] * _silu(za)
    heads_s[:, W_A + W_B:] = heads_s[:, W_A + W_B:] * _silu(zc)
    y = ALPHA * x + _mm(heads_s[...], w_out_ref[...])
    write_x1(_layer_norm(y, vec(ROW_LN1_G, D_MODEL), vec(ROW_LN1_B, D_MODEL)))


def _mixer_scratch(n_seq, tt, c):
    rows = n_seq * tt
    n_levels = _log2(c)
    n_groups = rows // GROUP_ROWS
    scratch = [
        pltpu.VMEM((n_seq, N_PAIRS, STACK, LANES), F32),
        pltpu.VMEM((n_seq, N_PAIRS, LANES, LANES), F32),
        pltpu.VMEM((n_seq, HIST + tt, CONV_A_WIDTH), F32),
        pltpu.VMEM((n_seq, HIST + tt, W_B), F32),
        pltpu.VMEM((n_seq, W_B), F32),
        pltpu.VMEM((rows, D_MIX), F32),
        pltpu.VMEM((rows, QK_A), F32),
        pltpu.VMEM((rows, QK_A), F32),
        pltpu.VMEM((rows, W_A), F32),
        pltpu.VMEM((rows, LANES), F32),
        pltpu.VMEM((rows, LANES), F32),
        pltpu.VMEM((rows, LANES), F32),
        pltpu.VMEM((rows, QKC_PAD), F32),
        pltpu.VMEM((rows, QKC_PAD), F32),
        pltpu.VMEM((rows, W_C), F32),
        pltpu.VMEM((rows, QKC_PAD), F32),
        pltpu.VMEM((rows, QKC_PAD), F32),
        pltpu.VMEM((rows, QKC_PAD), F32),
        pltpu.VMEM((n_levels, rows, QKC_PAD), F32),
        pltpu.VMEM((n_levels, rows, QKC_PAD), F32),
        pltpu.VMEM((n_groups, N_PAIRS, STACK, STACK), F32),
        pltpu.VMEM((n_groups, N_PAIRS, STACK, STACK), BF16),
        pltpu.VMEM((n_groups, N_PAIRS, STACK, 2 * LANES), BF16),
        pltpu.VMEM((n_groups, N_PAIRS, STACK, LANES), BF16),
        pltpu.VMEM((n_groups, N_PAIRS, STACK, STACK), BF16),
        pltpu.VMEM((n_groups, N_PAIRS, STACK, LANES), F32),
        pltpu.VMEM((n_groups, N_PAIRS, STACK, LANES), F32),
        pltpu.VMEM((n_groups, N_PAIRS, STACK, LANES), F32),
        pltpu.VMEM((n_groups, N_PAIRS, STACK, LANES), BF16),
        pltpu.VMEM((n_groups, N_PAIRS, GROUP_ROWS // c, LANES, LANES), F32),
        pltpu.VMEM((n_groups, N_PAIRS, GROUP_ROWS // c, LANES, LANES), BF16),
        pltpu.VMEM((n_groups, N_PAIRS, STACK, LANES), F32),
        pltpu.VMEM((n_groups, N_PAIRS, STACK, LANES), BF16),
        pltpu.VMEM((n_groups, N_PAIRS, GROUP_ROWS // c, LANES, LANES), F32),
    ]
    assert len(scratch) == N_MIXER_SCRATCH
    return scratch


def _ffn_stage(stage, x, state_ref, weight_refs, y_ref, tail_ref, scratch_refs, *, n_seq, tt):
    (w_up_ref, vec_ref, wd_ref) = weight_refs
    (gbuf_s, h_s) = scratch_refs
    rows = n_seq * tt
    tail = slice(HIST - (CONV_F - 1), HIST)

    if stage == 'init':
        if state_ref is not None:
            gbuf_s[:, tail, :] = state_ref[...]
        else:
            gbuf_s[:, tail, :] = jnp.zeros((n_seq, CONV_F - 1, D_FF), F32)
        return
    if stage == 'final':
        tail_ref[...] = gbuf_s[:, tail, :]
        return

    xb = x.astype(BF16)
    for j in range(N_FF_CHUNKS):
        cols = slice(j * FF_CHUNK, (j + 1) * FF_CHUNK)
        gate = _bdot(xb, w_up_ref[:, cols])
        val = _bdot(xb, w_up_ref[:, D_FF + j * FF_CHUNK:D_FF + (j + 1) * FF_CHUNK])
        gbuf_s[:, HIST:, cols] = gate.reshape(n_seq, tt, FF_CHUNK)
        conv = _conv_from_buffer(gbuf_s, vec_ref, ROW_CONV_F, CONV_F, tt, cols).reshape(rows, FF_CHUNK)
        h = _gelu_tanh(conv + vec_ref[ROW_CONV_F_BIAS:ROW_CONV_F_BIAS + 1, cols]) * val
        h_s[:, cols] = h.astype(BF16)
    gbuf_s[:, tail, :] = gbuf_s[:, tt + HIST - (CONV_F - 1):tt + HIST, :]
    y = ALPHA * x + _bdot(h_s[...], wd_ref[...])
    ln_g = vec_ref[ROW_LN2_G:ROW_LN2_G + 1, 0:D_MODEL]
    ln_b = vec_ref[ROW_LN2_B:ROW_LN2_B + 1, 0:D_MODEL]
    y_ref[...] = _layer_norm(y, ln_g, ln_b).reshape(n_seq, tt, D_MODEL)


def _mixer_kernel(*refs, n_seq, tt, c, has_state):
    n_state = N_MIXER_OUTS if has_state else 0
    k = 1 + n_state
    x_ref, state_refs = refs[0], refs[1:k]
    weight_refs = refs[k:k + N_MIXER_WEIGHTS]
    x1_ref = refs[k + N_MIXER_WEIGHTS]
    out_refs = refs[k + N_MIXER_WEIGHTS + 1:k + N_MIXER_WEIGHTS + 1 + N_MIXER_OUTS]
    scratch_refs = refs[k + N_MIXER_WEIGHTS + 1 + N_MIXER_OUTS:]
    t_idx = pl.program_id(1)

    def write_x1(val):
        x1_ref[...] = val.reshape(n_seq, tt, D_MODEL)

    stage = functools.partial(_mixer_stage, x_ref=x_ref, state_refs=state_refs, weight_refs=weight_refs,
                              out_refs=out_refs, scratch_refs=scratch_refs, write_x1=write_x1,
                              n_seq=n_seq, tt=tt, c=c)
    pl.when(t_idx == 0)(lambda: stage('init'))
    stage('body')
    pl.when(t_idx == pl.num_programs(1) - 1)(lambda: stage('final'))


def _ffn_kernel(*refs, n_seq, tt, has_state):
    k = 2 if has_state else 1
    x_ref = refs[0]
    state_ref = refs[1] if has_state else None
    weight_refs = refs[k:k + 3]
    y_ref, tail_ref = refs[k + 3:k + 5]
    scratch_refs = refs[k + 5:]
    t_idx = pl.program_id(1)
    stage = functools.partial(_ffn_stage, state_ref=state_ref, weight_refs=weight_refs, y_ref=y_ref,
                              tail_ref=tail_ref, scratch_refs=scratch_refs, n_seq=n_seq, tt=tt)
    pl.when(t_idx == 0)(lambda: stage('init', None))
    stage('body', x_ref[...].reshape(n_seq * tt, D_MODEL))
    pl.when(t_idx == pl.num_programs(1) - 1)(lambda: stage('final', None))


def _block_call(kern, name, x, states, state_shapes, weights, scratch, layer, *, n_seq, tt):
    nb, t_total, _ = x.shape
    assert nb % n_seq == 0 and t_total % tt == 0
    x_spec = pl.BlockSpec((n_seq, tt, D_MODEL), lambda b, t: (b, t, 0))
    in_specs, operands = [x_spec], [x]
    if states is not None:
        for arr, tail in zip(states, state_shapes):
            assert arr.shape[2:] == tail
            in_specs.append(pl.BlockSpec((None, n_seq) + tail, lambda b, t, z=(0,) * len(tail): (layer, b) + z))
            operands.append(arr)
    for a, per_layer in weights:
        if per_layer:
            in_specs.append(pl.BlockSpec((None,) + a.shape[1:], lambda b, t, z=(0,) * (a.ndim - 1): (layer,) + z))
        else:
            in_specs.append(pl.BlockSpec(a.shape, lambda b, t, z=(0,) * a.ndim: z))
        operands.append(a)
    out_shape = [jax.ShapeDtypeStruct((nb, t_total, D_MODEL), F32)]
    out_specs = [x_spec]
    for tail in state_shapes:
        out_shape.append(jax.ShapeDtypeStruct((nb,) + tail, F32))
        out_specs.append(pl.BlockSpec((n_seq,) + tail, lambda b, t, z=(0,) * len(tail): (b,) + z))
    return pl.pallas_call(
        kern,
        grid=(nb // n_seq, t_total // tt),
        in_specs=in_specs,
        out_specs=out_specs,
        out_shape=out_shape,
        scratch_shapes=scratch,
        compiler_params=pltpu.CompilerParams(
            dimension_semantics=("arbitrary", "arbitrary"), vmem_limit_bytes=VMEM_LIMIT_BYTES),
        name=f"{name}_nseq{n_seq}_tt{tt}",
    )(*operands)


def _mixer_call(x, states, w, layer, *, n_seq, tt, c):
    assert (n_seq == 1 and tt % GROUP_ROWS == 0 and c == GROUP_ROWS) or (tt == HIST and c == HIST)
    assert (n_seq * tt) % GROUP_ROWS == 0
    state_shapes = ((CONV_A - 1, CONV_A_WIDTH), (N_PAIRS, 2 * DK_A, DV_A), (CONV_B - 1, W_B), (1, W_B),
                    (QK_C, DV_C))
    weights = [(_tile_masks(n_seq * tt, c), False)] + [
        (w[k], True) for k in ('w_in', 'mix_vec', 'wr', 'wi', 'w2', 'w_out')]
    kern = functools.partial(_mixer_kernel, n_seq=n_seq, tt=tt, c=c, has_state=states is not None)
    return _block_call(kern, 'mixer', x, states, state_shapes, weights, _mixer_scratch(n_seq, tt, c), layer,
                       n_seq=n_seq, tt=tt)


def _ffn_call(x, state, w, layer, *, n_seq, tt):
    weights = [(w[k], True) for k in ('w_up', 'ffn_vec', 'w_down')]
    scratch = [pltpu.VMEM((n_seq, HIST + tt, D_FF), F32),
               pltpu.VMEM((n_seq * tt, D_FF), BF16)]
    kern = functools.partial(_ffn_kernel, n_seq=n_seq, tt=tt, has_state=state is not None)
    return _block_call(kern, 'ffn', x, None if state is None else (state,), ((CONV_F - 1, D_FF),), weights,
                       scratch, layer, n_seq=n_seq, tt=tt)


def _vec_table(pieces, width, n_rows):
    rows = []
    used = 0
    for arr, lane0 in pieces:
        arr = arr.astype(F32)
        rows.append(jnp.pad(arr, ((0, 0), (0, 0), (lane0, width - lane0 - arr.shape[2]))))
        used += arr.shape[1]
    depth = pieces[0][0].shape[0]
    rows.append(jnp.zeros((depth, n_rows - used, width), F32))
    return jnp.concatenate(rows, axis=1)


def _prep_weights(w_in, conv_a_w, a_log, dt_bias, norm_a_w, conv_b_w, conv_b_b, lru_w_r, lru_b_r, lru_w_i,
                  lru_b_i, lru_lambda, gla_w2, gla_b2, norm_c_w, w_out, ln1_g, ln1_b, ffn_w_up, ffn_conv_w,
                  ffn_conv_b, ffn_w_down, ln2_g, ln2_b):
    depth, d, _ = w_in.shape
    pts = [0]
    for s in (QK_A, QK_A, W_A, W_A, H_A, H_A, W_B, W_B, QK_C, QK_C, W_C, W_C, GLA_RANK):
        pts.append(pts[-1] + s)
    (qa, ka, va, za, ba, aa, xb, gb, qc, kc, vc, zc, lc) = [w_in[:, :, pts[i]:pts[i + 1]] for i in range(13)]
    z = lambda n: jnp.zeros((depth, d, n), w_in.dtype)
    small = jnp.concatenate([ba, z(LANE_DECAY - H_A), aa, z(LANE_LOWRANK - LANE_DECAY - H_A), lc,
                             z(LANES - LANE_LOWRANK - GLA_RANK)], axis=2)
    w_in_p = jnp.concatenate([qa, ka, va, za, xb, gb, qc, z(QKC_PAD - QK_C), kc, z(QKC_PAD - QK_C), vc, zc, small],
                             axis=2).astype(BF16)
    assert w_in_p.shape[2] == D_IN_PAD

    eye_blocks = jnp.eye(LRU_BLOCKS, dtype=F32)[None, :, None, :, None]

    def block_diag(wb):
        return (wb[:, :, :, None, :] * eye_blocks).reshape(depth, W_B, W_B).astype(BF16)

    w2 = jnp.pad(gla_w2, ((0, 0), (LANE_LOWRANK, LANES - LANE_LOWRANK - GLA_RANK), (0, QKC_PAD - QK_C))).astype(BF16)
    r1 = lambda v: v[:, None, :]
    dup = lambda v: jnp.concatenate([v, v], axis=1)[:, None, :]
    mix_vec = _vec_table(
        [(conv_a_w, 0), (conv_b_w, 0), (r1(conv_b_b), 0), (r1(a_log), LANE_DECAY), (r1(dt_bias), LANE_DECAY),
         (dup(norm_a_w), 0), (r1(lru_b_r), 0), (r1(lru_b_i), 0), (r1(lru_lambda), 0), (r1(gla_b2), 0),
         (dup(norm_c_w), 0), (r1(ln1_g), 0), (r1(ln1_b), 0)], VEC_W, MIX_VEC_ROWS)
    ffn_vec = _vec_table([(ffn_conv_w, 0), (r1(ffn_conv_b), 0), (r1(ln2_g), 0), (r1(ln2_b), 0)], D_FF, FFN_VEC_ROWS)
    return {
        'w_in': w_in_p, 'mix_vec': mix_vec, 'wr': block_diag(lru_w_r), 'wi': block_diag(lru_w_i), 'w2': w2,
        'w_out': w_out.astype(BF16), 'w_up': ffn_w_up.astype(BF16), 'ffn_vec': ffn_vec,
        'w_down': ffn_w_down.astype(BF16),
    }


def _trunk(x, states, w, *, n_seq, tt, c):
    nb = x.shape[0]
    if states is not None:
        st_dconv, st_delta, st_lconv, st_lru, st_gla, st_fconv = (s.astype(F32) for s in states)
        states = (st_dconv, st_delta.reshape(DEPTH, nb, N_PAIRS, 2 * DK_A, DV_A), st_lconv,
                  st_lru.reshape(DEPTH, nb, 1, W_B), st_gla.reshape(DEPTH, nb, QK_C, DV_C), st_fconv)
    outs = [[] for _ in range(6)]
    for l in range(DEPTH):
        x1, dconv, sd, lconv, h, sg = _mixer_call(x, None if states is None else states[:5], w, l,
                                                  n_seq=n_seq, tt=tt, c=c)
        x, fconv = _ffn_call(x1, None if states is None else states[5], w, l, n_seq=n_seq,
                             tt=tt if n_seq > 1 else min(FFN_PROMPT_TT, x.shape[1]))
        for o, v in zip(outs, (dconv, sd.reshape(nb, H_A, DK_A, DV_A), lconv, h.reshape(nb, W_B),
                               sg.reshape(nb, H_C, DK_C, DV_C), fconv)):
            o.append(v)
    return x, tuple(jnp.stack(o) for o in outs)


def kernel(x_prompt, x_sample, state_delta_conv, state_delta, state_lru_conv, state_lru, state_gla, state_ffn_conv,
           w_in, conv_a_w, a_log, dt_bias, norm_a_w, conv_b_w, conv_b_b, lru_w_r, lru_b_r, lru_w_i, lru_b_i,
           lru_lambda, gla_w2, gla_b2, norm_c_w, w_out, ln1_g, ln1_b, ffn_w_up, ffn_conv_w, ffn_conv_b, ffn_w_down,
           ln2_g, ln2_b):
    assert w_in.shape[0] == DEPTH
    w = _prep_weights(w_in, conv_a_w, a_log, dt_bias, norm_a_w, conv_b_w, conv_b_b, lru_w_r, lru_b_r, lru_w_i,
                      lru_b_i, lru_lambda, gla_w2, gla_b2, norm_c_w, w_out, ln1_g, ln1_b, ffn_w_up, ffn_conv_w,
                      ffn_conv_b, ffn_w_down, ln2_g, ln2_b)
    sample_states = (state_delta_conv, state_delta, state_lru_conv, state_lru, state_gla, state_ffn_conv)
    t_p = x_prompt.shape[1]
    t_s = x_sample.shape[1]
    y_p, p_st = _trunk(x_prompt, None, w, n_seq=1, tt=min(PROMPT_TT, t_p), c=CHUNK)
    y_s, s_st = _trunk(x_sample, sample_states, w, n_seq=min(SAMPLE_NSEQ, x_sample.shape[0]), tt=t_s, c=t_s)
    return (y_p, y_s) + p_st + s_st
```

```python
import functools
import math

import numpy as np
import jax
import jax.numpy as jnp
from jax import lax
from jax.experimental import pallas as pl
from jax.experimental.pallas import tpu as pltpu

F32 = jnp.float32
BF16 = jnp.bfloat16

D_MODEL = 1024
H_A, DK_A, DV_A = 6, 64, 64
QK_A = H_A * DK_A
W_A = H_A * DV_A
CONV_A = 4
W_B = 256
LRU_BLOCKS = 4
LRU_BLOCK = W_B // LRU_BLOCKS
CONV_B = 4
LRU_C = 8.0
H_C, DK_C, DV_C = 6, 32, 64
QK_C = H_C * DK_C
W_C = H_C * DV_C
GLA_RANK = 16
GLA_TAU = 16.0
D_MIX = W_A + W_B + W_C
CHUNK = 64
D_FF = 2816
CONV_F = 3
EPS = 1e-6
CONV_A_WIDTH = 2 * QK_A + W_A
DEPTH = 2
ALPHA = (2.0 * DEPTH) ** 0.25

LANES = 128
SUBLANES = 8
HALF = LANES // 2
GROUP_ROWS = 64
STACK = 2 * GROUP_ROWS
N_PAIRS = 3
HIST = SUBLANES
FF_CHUNK = 256
N_FF_CHUNKS = D_FF // FF_CHUNK
VMEM_LIMIT_BYTES = 56 * 1024 * 1024
PROMPT_TT = 512
PROMPT_NSEQ = 1
MASK_ROWS = 256
FFN_PROMPT_TT = 512
SAMPLE_NSEQ = 16

OFF_QA = 0
OFF_KA = OFF_QA + QK_A
OFF_VA = OFF_KA + QK_A
OFF_ZA = OFF_VA + W_A
OFF_XB = OFF_ZA + W_A
OFF_GB = OFF_XB + W_B
OFF_QC = OFF_GB + W_B
QKC_PAD = 2 * LANES
OFF_KC = OFF_QC + QKC_PAD
OFF_VC = OFF_KC + QKC_PAD
OFF_ZC = OFF_VC + W_C
OFF_SMALL = OFF_ZC + W_C
D_IN_PAD = OFF_SMALL + LANES
LANE_BETA = 0
LANE_DECAY = 8
LANE_LOWRANK = 16

VEC_W = CONV_A_WIDTH
(ROW_CONV_A, ROW_CONV_B, ROW_CONV_B_BIAS, ROW_ALOG, ROW_DTB, ROW_NORM_A, ROW_BR, ROW_BI, ROW_LAM, ROW_B2,
 ROW_NORM_C, ROW_LN1_G, ROW_LN1_B) = (0, 4, 8, 9, 10, 11, 12, 13, 14, 15, 16, 17, 18)
MIX_VEC_ROWS = 24
(ROW_CONV_F, ROW_CONV_F_BIAS, ROW_LN2_G, ROW_LN2_B) = (0, 3, 4, 5)
FFN_VEC_ROWS = 8


def _bdot(a, b):
    return jnp.dot(a, b, preferred_element_type=F32)


def _mm(a, b):
    return _bdot(a.astype(BF16), b.astype(BF16))


def _mm_nt(a, b):
    return lax.dot_general(a.astype(BF16), b.astype(BF16), (((1,), (1,)), ((), ())), preferred_element_type=F32)


def _split(a, n):
    parts = []
    r = a
    for i in range(n):
        p = r.astype(BF16)
        parts.append(p)
        if i + 1 < n:
            r = r - p.astype(F32)
    return parts


def _mm_mask_lhs(mask_bf16, a, n):
    out = None
    for p in _split(a, n):
        t = _bdot(mask_bf16, p)
        out = t if out is None else out + t
    return out


def _softplus(x):
    return jnp.maximum(x, 0.0) + jnp.log1p(jnp.exp(-jnp.abs(x)))


def _sigmoid(x):
    return 0.5 * jnp.tanh(0.5 * x) + 0.5


def _silu(x):
    h = 0.5 * x
    return h * jnp.tanh(h) + h


def _gelu_tanh(x):
    return 0.5 * x * (1.0 + jnp.tanh(math.sqrt(2.0 / math.pi) * (x + 0.044715 * (x * x * x))))


def _layer_norm(x, g, b):
    mu = jnp.mean(x, axis=-1, keepdims=True)
    xc = x - mu
    var = jnp.mean(xc * xc, axis=-1, keepdims=True)
    return xc * lax.rsqrt(var + EPS) * g + b


def _iota(shape, axis):
    return lax.broadcasted_iota(jnp.int32, shape, axis)


def _log2(n):
    l = int(math.log2(n))
    assert (1 << l) == n
    return l


def _conv_from_buffer(buf_ref, vec_ref, row0, width, tt, cols):
    y = None
    for d in range(width):
        tap = vec_ref[row0 + width - 1 - d:row0 + width - d, cols]
        term = buf_ref[:, HIST - d:HIST - d + tt, cols] * tap
        y = term if y is None else y + term
    return y


def _stack_rows(n_sb, c, pieces):
    if n_sb == 1:
        return jnp.concatenate([pieces[0], pieces[1]], axis=0)
    out = []
    for s in range(n_sb):
        for a in range(2):
            out.append(pieces[a][s * c:(s + 1) * c])
    return jnp.concatenate(out, axis=0)


def _unstack_rows(o, n_sb, c):
    lo = _iota((c, LANES), 1) < HALF
    out = []
    for s in range(n_sb):
        base = s * 2 * c
        out.append(jnp.where(lo, o[base:base + c], o[base + c:base + 2 * c]))
    return out[0] if n_sb == 1 else jnp.concatenate(out, axis=0)


def _col_bcast_stack(vals, lane0, n_sb, c):
    pieces = [jnp.broadcast_to(vals[:, lane0 + a:lane0 + a + 1], (GROUP_ROWS, LANES)) for a in range(2)]
    return _stack_rows(n_sb, c, pieces)


def _dup_values(v):
    lo = _iota(v.shape, 1) < HALF
    vr = pltpu.roll(v, HALF, axis=1)
    return jnp.where(lo, v, vr), jnp.where(lo, vr, v)


def _tile_masks(rows, c):
    lc = _log2(c)
    i = np.arange(rows)[:, None]
    j = np.arange(rows)[None, :]
    same = (i >> lc) == (j >> lc)
    out = [same & (j <= i), same]
    for lvl in range(lc):
        lh = lc - 1 - lvl
        hi = (i >> lh) & 1
        start = (i >> lh) << lh
        nxt = ((i >> lh) + 1) << lh
        out.append((hi == 1) & (j >= start) & (j <= i))
        out.append((hi == 0) & (j > i) & (j < nxt))
    return jnp.asarray(np.stack(out).astype(np.float32), dtype=BF16)


N_MIXER_WEIGHTS = 7
N_MIXER_OUTS = 5
N_MIXER_SCRATCH = 35


def _mixer_stage(stage, x_ref, state_refs, weight_refs, out_refs, scratch_refs, write_x1, *, n_seq, tt, c):
    has_state = len(state_refs) > 0
    (masks_ref, w_in_ref, vec_ref, wr_ref, wi_ref, w2_ref, w_out_ref) = weight_refs
    (dconv_out_ref, sd_out_ref, lconv_out_ref, h_out_ref, sg_out_ref) = out_refs
    (sd_s, sg_s, abuf_s, bbuf_s, h_s, heads_s, xb_s,
     qa_s, ka_s, va_s, beta_s, gcum_s, gtot_s,
     qc_s, kc_s, vc_s, qb_s, kdec_s, btot_s, qt_s, kt_s,
     t_s, p_s, rhs_s, qe_s, qkd_s, kdst_s, gtst_s, oprime_s, qprime_s, c_s, kw_s, av_s, qbst_s, gk_s,
     ) = scratch_refs

    rows = n_seq * tt
    n_groups = rows // GROUP_ROWS
    n_sb = GROUP_ROWS // c
    lc = _log2(c)
    n_levels = lc
    a_tail = slice(HIST - (CONV_A - 1), HIST)
    b_tail = slice(HIST - (CONV_B - 1), HIST)

    if stage == 'init':
        if has_state:
            dconv_ref, sd_ref, lconv_ref, h0_ref, sg_ref = state_refs
            abuf_s[:, a_tail, :] = dconv_ref[...]
            bbuf_s[:, b_tail, :] = lconv_ref[...]
            h_s[...] = h0_ref[...].reshape(n_seq, W_B)
            zeros = jnp.zeros((GROUP_ROWS, LANES), F32)
            for s in range(n_seq):
                for p in range(N_PAIRS):
                    sd = sd_ref[s, p]
                    sd_s[s, p] = jnp.concatenate([sd, sd], axis=-1)
                    sg = sg_ref[s, 2 * DK_C * p:2 * DK_C * (p + 1), :]
                    sg = jnp.concatenate([sg, sg], axis=-1)
                    sg_s[s, p] = jnp.concatenate([sg, zeros] if p % 2 == 0 else [zeros, sg], axis=0)
        else:
            abuf_s[:, a_tail, :] = jnp.zeros((n_seq, CONV_A - 1, CONV_A_WIDTH), F32)
            bbuf_s[:, b_tail, :] = jnp.zeros((n_seq, CONV_B - 1, W_B), F32)
            h_s[...] = jnp.zeros((n_seq, W_B), F32)
            sd_s[...] = jnp.zeros(sd_s.shape, F32)
            sg_s[...] = jnp.zeros(sg_s.shape, F32)
        return

    if stage == 'final':
        dconv_out_ref[...] = abuf_s[:, tt + HIST - (CONV_A - 1):tt + HIST, :]
        lconv_out_ref[...] = bbuf_s[:, tt + HIST - (CONV_B - 1):tt + HIST, :]
        h_out_ref[...] = h_s[...].reshape(n_seq, 1, W_B)
        for s in range(n_seq):
            for p in range(N_PAIRS):
                sd_out_ref[s, p] = sd_s[s, p][:, 0:HALF]
                r0 = GROUP_ROWS * (p % 2)
                sg_out_ref[s, 2 * DK_C * p:2 * DK_C * (p + 1), :] = sg_s[s, p][r0:r0 + GROUP_ROWS, 0:HALF]
        return

    def vec(row, width):
        return vec_ref[row:row + 1, 0:width]

    x = x_ref[...].reshape(rows, D_MODEL)
    xb_s[...] = x.astype(BF16)
    proj = _bdot(xb_s[...], w_in_ref[...])
    m_cum = masks_ref[0]
    m_tot = masks_ref[1]
    mask_rows = masks_ref.shape[1]

    def masked_sum(mask, a, n):
        out = [_mm_mask_lhs(mask, a[r:r + mask_rows], n) for r in range(0, rows, mask_rows)]
        return out[0] if len(out) == 1 else jnp.concatenate(out, axis=0)

    abuf_s[:, HIST:, :] = proj[:, OFF_QA:OFF_QA + CONV_A_WIDTH].reshape(n_seq, tt, CONV_A_WIDTH)
    conv_a = _conv_from_buffer(abuf_s, vec_ref, ROW_CONV_A, CONV_A, tt, slice(0, CONV_A_WIDTH))
    qkv = _silu(conv_a.reshape(rows, CONV_A_WIDTH))
    abuf_s[:, a_tail, :] = abuf_s[:, tt + HIST - (CONV_A - 1):tt + HIST, :]
    p_i = _iota((LANES, LANES), 0) >> _log2(DK_A)
    p_j = _iota((LANES, LANES), 1) >> _log2(DK_A)
    pair_ones = jnp.where(p_i == p_j, 1.0, 0.0).astype(BF16)
    for p in range(N_PAIRS):
        for off, dst, scale in ((0, qa_s, DK_A ** -0.5), (QK_A, ka_s, 1.0)):
            v = qkv[:, off + p * LANES:off + (p + 1) * LANES]
            ssq = _bdot((v * v).astype(BF16), pair_ones)
            dst[:, p * LANES:(p + 1) * LANES] = v * (lax.rsqrt(ssq + EPS) * scale)
    va_s[...] = qkv[:, 2 * QK_A:]

    small = proj[:, OFF_SMALL:OFF_SMALL + LANES]
    beta_s[...] = _sigmoid(small)
    g_full = -jnp.exp(vec(ROW_ALOG, LANES)) * _softplus(small + vec(ROW_DTB, LANES))
    gcum_s[...] = masked_sum(m_cum, g_full, 2)
    gtot_s[...] = masked_sum(m_tot, g_full, 2)

    def mixer_b():
        bbuf_s[:, HIST:, :] = proj[:, OFF_XB:OFF_XB + W_B].reshape(n_seq, tt, W_B)
        xc = (_conv_from_buffer(bbuf_s, vec_ref, ROW_CONV_B, CONV_B, tt, slice(0, W_B)).reshape(rows, W_B)
              + vec(ROW_CONV_B_BIAS, W_B))
        bbuf_s[:, b_tail, :] = bbuf_s[:, tt + HIST - (CONV_B - 1):tt + HIST, :]
        gate_r = _sigmoid(_mm(xc, wr_ref[...]) + vec(ROW_BR, W_B))
        gate_i = _sigmoid(_mm(xc, wi_ref[...]) + vec(ROW_BI, W_B))
        log_a = -LRU_C * gate_r * _softplus(-vec(ROW_LAM, W_B))
        a_t = jnp.exp(log_a)
        b_t = jnp.sqrt(-jnp.tanh(log_a) * (a_t * a_t + 1.0)) * (gate_i * xc)
        i_seq = _iota((rows, W_B), 0) & (tt - 1)
        h_prev = jnp.concatenate([jnp.broadcast_to(h_s[s:s + 1, :], (tt, W_B)) for s in range(n_seq)], axis=0)
        b_t = b_t + jnp.where(i_seq == 0, a_t * h_prev, 0.0)
        d = 1
        while d < tt:
            a_sh = pltpu.roll(a_t, d, axis=0)
            b_sh = pltpu.roll(b_t, d, axis=0)
            ok = i_seq >= d
            b_t = jnp.where(ok, a_t * b_sh + b_t, b_t)
            a_t = jnp.where(ok, a_t * a_sh, a_t)
            d *= 2
        h_s[...] = jnp.concatenate([b_t[(s + 1) * tt - 1:(s + 1) * tt] for s in range(n_seq)], axis=0)
        heads_s[:, W_A:W_A + W_B] = (b_t * _gelu_tanh(proj[:, OFF_GB:OFF_GB + W_B])).astype(BF16)

    qc = proj[:, OFF_QC:OFF_QC + QKC_PAD] * (DK_C ** -0.5)
    kc = proj[:, OFF_KC:OFF_KC + QKC_PAD]
    logf = -_softplus(-(_mm(small, w2_ref[...]) + vec(ROW_B2, QKC_PAD))) * (1.0 / GLA_TAU)
    b_cum = masked_sum(m_cum, logf, 2)
    b_tot = masked_sum(m_tot, logf, 2)
    qc_s[...] = qc
    kc_s[...] = kc
    vc_s[...] = proj[:, OFF_VC:OFF_VC + W_C]
    qb_s[...] = qc * jnp.exp(b_cum)
    kdec_s[...] = kc * jnp.exp(b_tot - b_cum)
    btot_s[...] = b_tot
    for lvl in range(n_levels):
        qt_s[lvl] = qc * jnp.exp(masked_sum(masks_ref[2 + 2 * lvl], logf, 1))
        kt_s[lvl] = kc * jnp.exp(masked_sum(masks_ref[3 + 2 * lvl], logf, 1))

    s_i = _iota((STACK, STACK), 0)
    s_j = _iota((STACK, STACK), 1)
    same_blk = (s_i >> lc) == (s_j >> lc)
    m_incl = same_blk & (s_j <= s_i)
    m_strict = same_blk & (s_j < s_i)
    eye = jnp.where(s_i == s_j, 1.0, 0.0)
    lane = _iota((GROUP_ROWS, LANES), 1)
    m0 = lane < HALF
    norm_a = vec(ROW_NORM_A, LANES)
    norm_c = vec(ROW_NORM_C, LANES)

    def delta_setup(gi):
        rs = pl.ds(gi * GROUP_ROWS, GROUP_ROWS)
        beta_g = beta_s[rs, :]
        gcum_g = gcum_s[rs, :]
        gtot_g = gtot_s[rs, :]
        for p in range(N_PAIRS):
            sl = pl.ds(p * LANES, LANES)
            q_p = qa_s[rs, sl]
            k_p = ka_s[rs, sl]
            v0, v1 = _dup_values(va_s[rs, sl])
            q_st = _stack_rows(n_sb, c, [jnp.where(m0, q_p, 0.0), jnp.where(m0, 0.0, q_p)])
            k_st = _stack_rows(n_sb, c, [jnp.where(m0, k_p, 0.0), jnp.where(m0, 0.0, k_p)])
            v_st = _stack_rows(n_sb, c, [v0, v1])
            beta_c = _col_bcast_stack(beta_g, LANE_BETA + 2 * p, n_sb, c)
            g_c = _col_bcast_stack(gcum_g, LANE_DECAY + 2 * p, n_sb, c)
            gt_c = _col_bcast_stack(gtot_g, LANE_DECAY + 2 * p, n_sb, c)
            dec = jnp.exp(jnp.where(m_incl, g_c - g_c.T, -1e30))
            kq = _mm_nt(jnp.concatenate([k_st, q_st], axis=0), k_st)
            kk, qk = kq[0:STACK], kq[STACK:]
            n_mat = -(beta_c * kk * jnp.where(m_strict, dec, 0.0))
            eg = jnp.exp(g_c)
            t_s[gi, p] = eye + n_mat
            p_s[gi, p] = n_mat.astype(BF16)
            rhs_s[gi, p] = jnp.concatenate([beta_c * v_st, beta_c * eg * k_st], axis=1).astype(BF16)
            qe_s[gi, p] = (eg * q_st).astype(BF16)
            qkd_s[gi, p] = (qk * dec).astype(BF16)
            kdst_s[gi, p] = k_st * jnp.exp(gt_c - g_c)
            gtst_s[gi, p] = gt_c

    def inverse_level(gi, j):
        for p in range(N_PAIRS):
            pw = p_s[gi, p]
            if j >= 2:
                t_old = t_s[gi, p]
                both = _bdot(jnp.concatenate([pw, t_old.astype(BF16)], axis=0), pw)
                t_s[gi, p] = t_old + both[STACK:]
                p_s[gi, p] = both[0:STACK].astype(BF16)
            else:
                p_s[gi, p] = _bdot(pw, pw).astype(BF16)

    def delta_solve(gi):
        for p in range(N_PAIRS):
            t_old = t_s[gi, p]
            t_fin = t_old + _bdot(t_old.astype(BF16), p_s[gi, p])
            uwk = _bdot(t_fin.astype(BF16), rhs_s[gi, p]).astype(BF16)
            k_dec = kdst_s[gi, p]
            if n_sb == 1:
                both = _bdot(jnp.concatenate([qkd_s[gi, p], k_dec.T.astype(BF16)], axis=0), uwk)
                o_qw, c_kw = both[0:STACK], [both[STACK:]]
            else:
                o_qw = _bdot(qkd_s[gi, p], uwk)
                c_kw = [_bdot(k_dec[sb * 2 * c:(sb + 1) * 2 * c].T.astype(BF16), uwk[sb * 2 * c:(sb + 1) * 2 * c])
                        for sb in range(n_sb)]
            oprime_s[gi, p] = o_qw[:, 0:LANES]
            qprime_s[gi, p] = (qe_s[gi, p].astype(F32) - o_qw[:, LANES:]).astype(BF16)
            for sb in range(n_sb):
                c_s[gi, p, sb] = c_kw[sb][:, 0:LANES]
                kw_s[gi, p, sb] = c_kw[sb][:, LANES:].astype(BF16)

    def gla_precompute(gi):
        rs = pl.ds(gi * GROUP_ROWS, GROUP_ROWS)
        for p in range(N_PAIRS):
            sl = pl.ds(p * LANES, LANES)
            slab = pl.ds((p // 2) * LANES, LANES)
            l0 = 2 * DK_C * (p % 2)
            mh0 = (lane >= l0) & (lane < l0 + DK_C)
            mh1 = (lane >= l0 + DK_C) & (lane < l0 + 2 * DK_C)

            def stack_c(ref, idx=None):
                val = ref[rs, slab] if idx is None else ref[idx, rs, slab]
                return _stack_rows(n_sb, c, [jnp.where(mh0, val, 0.0), jnp.where(mh1, val, 0.0)])

            att = jnp.where(s_i == s_j, _mm_nt(stack_c(qc_s), stack_c(kc_s)), 0.0)
            for lvl in range(n_levels):
                lh = lc - 1 - lvl
                valid = (((s_i >> (lh + 1)) == (s_j >> (lh + 1)))
                         & (((s_i >> lh) & 1) == 1) & (((s_j >> lh) & 1) == 0))
                att = att + jnp.where(valid, _mm_nt(stack_c(qt_s, lvl), stack_c(kt_s, lvl)), 0.0)
            vc0, vc1 = _dup_values(vc_s[rs, sl])
            vc_st = _stack_rows(n_sb, c, [vc0, vc1]).astype(BF16)
            qbst_s[gi, p] = stack_c(qb_s).astype(BF16)
            kd_st = stack_c(kdec_s)
            if n_sb == 1:
                both = _bdot(jnp.concatenate([att.astype(BF16), kd_st.T.astype(BF16)], axis=0), vc_st)
                av_s[gi, p] = both[0:STACK]
                gk_s[gi, p, 0] = both[STACK:]
            else:
                av_s[gi, p] = _bdot(att.astype(BF16), vc_st)
                for sb in range(n_sb):
                    blk = slice(sb * 2 * c, (sb + 1) * 2 * c)
                    gk_s[gi, p, sb] = _bdot(kd_st[blk].T.astype(BF16), vc_st[blk])

    for gi in range(n_groups):
        delta_setup(gi)
    for j in range(1, lc):
        for gi in range(n_groups):
            inverse_level(gi, j)
        if j <= n_groups:
            gla_precompute(j - 1)
        if j == 1:
            mixer_b()
    for gi in range(lc - 1, n_groups):
        gla_precompute(gi)
    for gi in range(n_groups):
        delta_solve(gi)

    def recurrence_body(gi):
        rs = pl.ds(gi * GROUP_ROWS, GROUP_ROWS)
        for p in range(N_PAIRS):
            sl = pl.ds(p * LANES, LANES)
            q_prime = qprime_s[gi, p]
            gt_c = gtst_s[gi, p]
            qs_parts = []
            for sb in range(n_sb):
                seq = (gi * GROUP_ROWS + sb * c) // tt
                blk = slice(sb * 2 * c, (sb + 1) * 2 * c)
                s_old = sd_s[seq, p]
                s_bf = s_old.astype(BF16)
                if n_sb == 1:
                    both = _bdot(jnp.concatenate([q_prime, kw_s[gi, p, 0]], axis=0), s_bf)
                    qs, kws = both[0:STACK], both[STACK:]
                else:
                    qs, kws = _bdot(q_prime[blk], s_bf), _bdot(kw_s[gi, p, sb], s_bf)
                qs_parts.append(qs)
                decay_rows = jnp.concatenate(
                    [jnp.broadcast_to(gt_c[sb * 2 * c + a * c:sb * 2 * c + a * c + 1, :], (GROUP_ROWS, LANES))
                     for a in range(2)], axis=0)
                sd_s[seq, p] = jnp.exp(decay_rows) * s_old + (c_s[gi, p, sb] - kws)
            qs_all = qs_parts[0] if n_sb == 1 else jnp.concatenate(qs_parts, axis=0)
            o_st = qs_all + oprime_s[gi, p]
            o_st = o_st * lax.rsqrt(jnp.mean(o_st * o_st, axis=-1, keepdims=True) + EPS) * norm_a
            z_a = proj[gi * GROUP_ROWS:(gi + 1) * GROUP_ROWS, OFF_ZA + p * LANES:OFF_ZA + (p + 1) * LANES]
            heads_s[rs, sl] = (_unstack_rows(o_st, n_sb, c) * _silu(z_a)).astype(BF16)
            slab = pl.ds((p // 2) * LANES, LANES)
            qb_st = qbst_s[gi, p]
            btot_t = btot_s[rs, slab].T
            oi_parts = []
            for sb in range(n_sb):
                seq = (gi * GROUP_ROWS + sb * c) // tt
                blk = slice(sb * 2 * c, (sb + 1) * 2 * c)
                s_old = sg_s[seq, p]
                oi_parts.append(_bdot(qb_st[blk], s_old.astype(BF16)))
                decay_col = jnp.broadcast_to(btot_t[:, sb * c:sb * c + 1], (LANES, LANES))
                sg_s[seq, p] = jnp.exp(decay_col) * s_old + gk_s[gi, p, sb]
            oi_all = oi_parts[0] if n_sb == 1 else jnp.concatenate(oi_parts, axis=0)
            oc_st = oi_all + av_s[gi, p]
            oc_st = oc_st * lax.rsqrt(jnp.mean(oc_st * oc_st, axis=-1, keepdims=True) + EPS) * norm_c
            z_c = proj[gi * GROUP_ROWS:(gi + 1) * GROUP_ROWS, OFF_ZC + p * LANES:OFF_ZC + (p + 1) * LANES]
            heads_s[rs, pl.ds(W_A + W_B + p * LANES, LANES)] = (
                _unstack_rows(oc_st, n_sb, c) * _silu(z_c)).astype(BF16)

    groups_per_seq = max(tt // GROUP_ROWS, 1)
    for k in range(groups_per_seq):
        for gi in range(k, n_groups, groups_per_seq):
            recurrence_body(gi)

    y = ALPHA * x + _bdot(heads_s[...], w_out_ref[...])
    write_x1(_layer_norm(y, vec(ROW_LN1_G, D_MODEL), vec(ROW_LN1_B, D_MODEL)))


def _mixer_scratch(n_seq, tt, c):
    rows = n_seq * tt
    n_levels = _log2(c)
    n_groups = rows // GROUP_ROWS
    scratch = [
        pltpu.VMEM((n_seq, N_PAIRS, STACK, LANES), F32),
        pltpu.VMEM((n_seq, N_PAIRS, LANES, LANES), F32),
        pltpu.VMEM((n_seq, HIST + tt, CONV_A_WIDTH), F32),
        pltpu.VMEM((n_seq, HIST + tt, W_B), F32),
        pltpu.VMEM((n_seq, W_B), F32),
        pltpu.VMEM((rows, D_MIX), BF16),
        pltpu.VMEM((rows, D_MODEL), BF16),
        pltpu.VMEM((rows, QK_A), F32),
        pltpu.VMEM((rows, QK_A), F32),
        pltpu.VMEM((rows, W_A), F32),
        pltpu.VMEM((rows, LANES), F32),
        pltpu.VMEM((rows, LANES), F32),
        pltpu.VMEM((rows, LANES), F32),
        pltpu.VMEM((rows, QKC_PAD), F32),
        pltpu.VMEM((rows, QKC_PAD), F32),
        pltpu.VMEM((rows, W_C), F32),
        pltpu.VMEM((rows, QKC_PAD), F32),
        pltpu.VMEM((rows, QKC_PAD), F32),
        pltpu.VMEM((rows, QKC_PAD), F32),
        pltpu.VMEM((n_levels, rows, QKC_PAD), F32),
        pltpu.VMEM((n_levels, rows, QKC_PAD), F32),
        pltpu.VMEM((n_groups, N_PAIRS, STACK, STACK), F32),
        pltpu.VMEM((n_groups, N_PAIRS, STACK, STACK), BF16),
        pltpu.VMEM((n_groups, N_PAIRS, STACK, 2 * LANES), BF16),
        pltpu.VMEM((n_groups, N_PAIRS, STACK, LANES), BF16),
        pltpu.VMEM((n_groups, N_PAIRS, STACK, STACK), BF16),
        pltpu.VMEM((n_groups, N_PAIRS, STACK, LANES), F32),
        pltpu.VMEM((n_groups, N_PAIRS, STACK, LANES), F32),
        pltpu.VMEM((n_groups, N_PAIRS, STACK, LANES), F32),
        pltpu.VMEM((n_groups, N_PAIRS, STACK, LANES), BF16),
        pltpu.VMEM((n_groups, N_PAIRS, GROUP_ROWS // c, LANES, LANES), F32),
        pltpu.VMEM((n_groups, N_PAIRS, GROUP_ROWS // c, LANES, LANES), BF16),
        pltpu.VMEM((n_groups, N_PAIRS, STACK, LANES), F32),
        pltpu.VMEM((n_groups, N_PAIRS, STACK, LANES), BF16),
        pltpu.VMEM((n_groups, N_PAIRS, GROUP_ROWS // c, LANES, LANES), F32),
    ]
    assert len(scratch) == N_MIXER_SCRATCH
    return scratch


def _ffn_stage(stage, x, state_ref, weight_refs, y_ref, tail_ref, scratch_refs, *, n_seq, tt):
    (w_up_ref, vec_ref, wd_ref) = weight_refs
    (gbuf_s, h_s, xb_s) = scratch_refs
    rows = n_seq * tt
    tail = slice(HIST - (CONV_F - 1), HIST)

    if stage == 'init':
        if state_ref is not None:
            gbuf_s[:, tail, :] = state_ref[...]
        else:
            gbuf_s[:, tail, :] = jnp.zeros((n_seq, CONV_F - 1, D_FF), F32)
        return
    if stage == 'final':
        tail_ref[...] = gbuf_s[:, tt + HIST - (CONV_F - 1):tt + HIST, :]
        return

    xb_s[...] = x.astype(BF16)
    for j in range(N_FF_CHUNKS):
        cols = slice(j * FF_CHUNK, (j + 1) * FF_CHUNK)
        gate = _bdot(xb_s[...], w_up_ref[:, cols])
        val = _bdot(xb_s[...], w_up_ref[:, D_FF + j * FF_CHUNK:D_FF + (j + 1) * FF_CHUNK])
        gbuf_s[:, HIST:, cols] = gate.reshape(n_seq, tt, FF_CHUNK)
        conv = _conv_from_buffer(gbuf_s, vec_ref, ROW_CONV_F, CONV_F, tt, cols).reshape(rows, FF_CHUNK)
        h = _gelu_tanh(conv + vec_ref[ROW_CONV_F_BIAS:ROW_CONV_F_BIAS + 1, cols]) * val
        h_s[:, cols] = h.astype(BF16)
    gbuf_s[:, tail, :] = gbuf_s[:, tt + HIST - (CONV_F - 1):tt + HIST, :]
    y = ALPHA * x + _bdot(h_s[...], wd_ref[...])
    ln_g = vec_ref[ROW_LN2_G:ROW_LN2_G + 1, 0:D_MODEL]
    ln_b = vec_ref[ROW_LN2_B:ROW_LN2_B + 1, 0:D_MODEL]
    y_ref[...] = _layer_norm(y, ln_g, ln_b).reshape(n_seq, tt, D_MODEL)


def _mixer_kernel(*refs, n_seq, tt, c, has_state):
    n_state = N_MIXER_OUTS if has_state else 0
    k = 1 + n_state
    x_ref, state_refs = refs[0], refs[1:k]
    weight_refs = refs[k:k + N_MIXER_WEIGHTS]
    x1_ref = refs[k + N_MIXER_WEIGHTS]
    out_refs = refs[k + N_MIXER_WEIGHTS + 1:k + N_MIXER_WEIGHTS + 1 + N_MIXER_OUTS]
    scratch_refs = refs[k + N_MIXER_WEIGHTS + 1 + N_MIXER_OUTS:]
    t_idx = pl.program_id(1)

    def write_x1(val):
        x1_ref[...] = val.reshape(n_seq, tt, D_MODEL)

    stage = functools.partial(_mixer_stage, x_ref=x_ref, state_refs=state_refs, weight_refs=weight_refs,
                              out_refs=out_refs, scratch_refs=scratch_refs, write_x1=write_x1,
                              n_seq=n_seq, tt=tt, c=c)
    pl.when(t_idx == 0)(lambda: stage('init'))
    stage('body')
    pl.when(t_idx == pl.num_programs(1) - 1)(lambda: stage('final'))


def _ffn_kernel(*refs, n_seq, tt, has_state):
    k = 2 if has_state else 1
    x_ref = refs[0]
    state_ref = refs[1] if has_state else None
    weight_refs = refs[k:k + 3]
    y_ref, tail_ref = refs[k + 3:k + 5]
    scratch_refs = refs[k + 5:]
    t_idx = pl.program_id(1)
    stage = functools.partial(_ffn_stage, state_ref=state_ref, weight_refs=weight_refs, y_ref=y_ref,
                              tail_ref=tail_ref, scratch_refs=scratch_refs, n_seq=n_seq, tt=tt)
    pl.when(t_idx == 0)(lambda: stage('init', None))
    stage('body', x_ref[...].reshape(n_seq * tt, D_MODEL))
    pl.when(t_idx == pl.num_programs(1) - 1)(lambda: stage('final', None))


def _block_call(kern, name, x, states, state_shapes, weights, scratch, layer, *, n_seq, tt):
    nb, t_total, _ = x.shape
    assert nb % n_seq == 0 and t_total % tt == 0
    x_spec = pl.BlockSpec((n_seq, tt, D_MODEL), lambda b, t: (b, t, 0))
    in_specs, operands = [x_spec], [x]
    if states is not None:
        for arr, tail in zip(states, state_shapes):
            assert arr.shape[2:] == tail
            in_specs.append(pl.BlockSpec((None, n_seq) + tail, lambda b, t, z=(0,) * len(tail): (layer, b) + z))
            operands.append(arr)
    for a, per_layer in weights:
        if per_layer:
            in_specs.append(pl.BlockSpec((None,) + a.shape[1:], lambda b, t, z=(0,) * (a.ndim - 1): (layer,) + z))
        else:
            in_specs.append(pl.BlockSpec(a.shape, lambda b, t, z=(0,) * a.ndim: z))
        operands.append(a)
    out_shape = [jax.ShapeDtypeStruct((nb, t_total, D_MODEL), F32)]
    out_specs = [x_spec]
    for tail in state_shapes:
        out_shape.append(jax.ShapeDtypeStruct((nb,) + tail, F32))
        out_specs.append(pl.BlockSpec((n_seq,) + tail, lambda b, t, z=(0,) * len(tail): (b,) + z))
    return pl.pallas_call(
        kern,
        grid=(nb // n_seq, t_total // tt),
        in_specs=in_specs,
        out_specs=out_specs,
        out_shape=out_shape,
        scratch_shapes=scratch,
        compiler_params=pltpu.CompilerParams(
            dimension_semantics=("arbitrary", "arbitrary"), vmem_limit_bytes=VMEM_LIMIT_BYTES),
        name=f"{name}_nseq{n_seq}_tt{tt}",
    )(*operands)


def _mixer_call(x, states, w, layer, *, n_seq, tt, c):
    assert (tt % MASK_ROWS == 0 and c == GROUP_ROWS) or (tt == HIST and c == HIST and n_seq * tt <= MASK_ROWS)
    assert (n_seq * tt) % GROUP_ROWS == 0
    state_shapes = ((CONV_A - 1, CONV_A_WIDTH), (N_PAIRS, 2 * DK_A, DV_A), (CONV_B - 1, W_B), (1, W_B),
                    (QK_C, DV_C))
    weights = [(_tile_masks(min(n_seq * tt, MASK_ROWS), c), False)] + [
        (w[k], True) for k in ('w_in', 'mix_vec', 'wr', 'wi', 'w2', 'w_out')]
    kern = functools.partial(_mixer_kernel, n_seq=n_seq, tt=tt, c=c, has_state=states is not None)
    return _block_call(kern, 'mixer', x, states, state_shapes, weights, _mixer_scratch(n_seq, tt, c), layer,
                       n_seq=n_seq, tt=tt)


def _ffn_call(x, state, w, layer, *, n_seq, tt):
    weights = [(w[k], True) for k in ('w_up', 'ffn_vec', 'w_down')]
    scratch = [pltpu.VMEM((n_seq, HIST + tt, D_FF), F32),
               pltpu.VMEM((n_seq * tt, D_FF), BF16),
               pltpu.VMEM((n_seq * tt, D_MODEL), BF16)]
    kern = functools.partial(_ffn_kernel, n_seq=n_seq, tt=tt, has_state=state is not None)
    return _block_call(kern, 'ffn', x, None if state is None else (state,), ((CONV_F - 1, D_FF),), weights,
                       scratch, layer, n_seq=n_seq, tt=tt)


def _vec_table(pieces, width, n_rows):
    rows = []
    used = 0
    for arr, lane0 in pieces:
        arr = arr.astype(F32)
        rows.append(jnp.pad(arr, ((0, 0), (0, 0), (lane0, width - lane0 - arr.shape[2]))))
        used += arr.shape[1]
    depth = pieces[0][0].shape[0]
    rows.append(jnp.zeros((depth, n_rows - used, width), F32))
    return jnp.concatenate(rows, axis=1)


def _prep_weights(w_in, conv_a_w, a_log, dt_bias, norm_a_w, conv_b_w, conv_b_b, lru_w_r, lru_b_r, lru_w_i,
                  lru_b_i, lru_lambda, gla_w2, gla_b2, norm_c_w, w_out, ln1_g, ln1_b, ffn_w_up, ffn_conv_w,
                  ffn_conv_b, ffn_w_down, ln2_g, ln2_b):
    depth, d, _ = w_in.shape
    pts = [0]
    for s in (QK_A, QK_A, W_A, W_A, H_A, H_A, W_B, W_B, QK_C, QK_C, W_C, W_C, GLA_RANK):
        pts.append(pts[-1] + s)
    (qa, ka, va, za, ba, aa, xb, gb, qc, kc, vc, zc, lc) = [w_in[:, :, pts[i]:pts[i + 1]] for i in range(13)]
    z = lambda n: jnp.zeros((depth, d, n), w_in.dtype)
    small = jnp.concatenate([ba, z(LANE_DECAY - H_A), aa, z(LANE_LOWRANK - LANE_DECAY - H_A), lc,
                             z(LANES - LANE_LOWRANK - GLA_RANK)], axis=2)
    w_in_p = jnp.concatenate([qa, ka, va, za, xb, gb, qc, z(QKC_PAD - QK_C), kc, z(QKC_PAD - QK_C), vc, zc, small],
                             axis=2).astype(BF16)
    assert w_in_p.shape[2] == D_IN_PAD

    eye_blocks = jnp.eye(LRU_BLOCKS, dtype=F32)[None, :, None, :, None]

    def block_diag(wb):
        return (wb[:, :, :, None, :] * eye_blocks).reshape(depth, W_B, W_B).astype(BF16)

    w2 = jnp.pad(gla_w2, ((0, 0), (LANE_LOWRANK, LANES - LANE_LOWRANK - GLA_RANK), (0, QKC_PAD - QK_C))).astype(BF16)
    r1 = lambda v: v[:, None, :]
    dup = lambda v: jnp.concatenate([v, v], axis=1)[:, None, :]
    mix_vec = _vec_table(
        [(conv_a_w, 0), (conv_b_w, 0), (r1(conv_b_b), 0), (r1(a_log), LANE_DECAY), (r1(dt_bias), LANE_DECAY),
         (dup(norm_a_w), 0), (r1(lru_b_r), 0), (r1(lru_b_i), 0), (r1(lru_lambda), 0), (r1(gla_b2), 0),
         (dup(norm_c_w), 0), (r1(ln1_g), 0), (r1(ln1_b), 0)], VEC_W, MIX_VEC_ROWS)
    ffn_vec = _vec_table([(ffn_conv_w, 0), (r1(ffn_conv_b), 0), (r1(ln2_g), 0), (r1(ln2_b), 0)], D_FF, FFN_VEC_ROWS)
    return {
        'w_in': w_in_p, 'mix_vec': mix_vec, 'wr': block_diag(lru_w_r), 'wi': block_diag(lru_w_i), 'w2': w2,
        'w_out': w_out.astype(BF16), 'w_up': ffn_w_up.astype(BF16), 'ffn_vec': ffn_vec,
        'w_down': ffn_w_down.astype(BF16),
    }


def _trunk(x, states, w, *, n_seq, tt, c):
    nb = x.shape[0]
    if states is not None:
        st_dconv, st_delta, st_lconv, st_lru, st_gla, st_fconv = (s.astype(F32) for s in states)
        states = (st_dconv, st_delta.reshape(DEPTH, nb, N_PAIRS, 2 * DK_A, DV_A), st_lconv,
                  st_lru.reshape(DEPTH, nb, 1, W_B), st_gla.reshape(DEPTH, nb, QK_C, DV_C), st_fconv)
    outs = [[] for _ in range(6)]
    for l in range(DEPTH):
        x1, dconv, sd, lconv, h, sg = _mixer_call(x, None if states is None else states[:5], w, l,
                                                  n_seq=n_seq, tt=tt, c=c)
        x, fconv = _ffn_call(x1, None if states is None else states[5], w, l, n_seq=n_seq,
                             tt=tt if n_seq > 1 else min(FFN_PROMPT_TT, x.shape[1]))
        for o, v in zip(outs, (dconv, sd.reshape(nb, H_A, DK_A, DV_A), lconv, h.reshape(nb, W_B),
                               sg.reshape(nb, H_C, DK_C, DV_C), fconv)):
            o.append(v)
    return x, tuple(jnp.stack(o) for o in outs)


def kernel(x_prompt, x_sample, state_delta_conv, state_delta, state_lru_conv, state_lru, state_gla, state_ffn_conv,
           w_in, conv_a_w, a_log, dt_bias, norm_a_w, conv_b_w, conv_b_b, lru_w_r, lru_b_r, lru_w_i, lru_b_i,
           lru_lambda, gla_w2, gla_b2, norm_c_w, w_out, ln1_g, ln1_b, ffn_w_up, ffn_conv_w, ffn_conv_b, ffn_w_down,
           ln2_g, ln2_b):
    assert w_in.shape[0] == DEPTH
    w = _prep_weights(w_in, conv_a_w, a_log, dt_bias, norm_a_w, conv_b_w, conv_b_b, lru_w_r, lru_b_r, lru_w_i,
                      lru_b_i, lru_lambda, gla_w2, gla_b2, norm_c_w, w_out, ln1_g, ln1_b, ffn_w_up, ffn_conv_w,
                      ffn_conv_b, ffn_w_down, ln2_g, ln2_b)
    sample_states = (state_delta_conv, state_delta, state_lru_conv, state_lru, state_gla, state_ffn_conv)
    t_p = x_prompt.shape[1]
    t_s = x_sample.shape[1]
    n_p = PROMPT_NSEQ if x_prompt.shape[0] % PROMPT_NSEQ == 0 else 1
    y_p, p_st = _trunk(x_prompt, None, w, n_seq=n_p, tt=min(PROMPT_TT, t_p), c=CHUNK)
    y_s, s_st = _trunk(x_sample, sample_states, w, n_seq=min(SAMPLE_NSEQ, x_sample.shape[0]), tt=t_s, c=t_s)
    return (y_p, y_s) + p_st + s_st
```

```python
import functools
import math

import numpy as np
import jax
import jax.numpy as jnp
from jax import lax
from jax.experimental import pallas as pl
from jax.experimental.pallas import tpu as pltpu

F32 = jnp.float32
BF16 = jnp.bfloat16

D_MODEL = 1024
H_A, DK_A, DV_A = 6, 64, 64
QK_A = H_A * DK_A
W_A = H_A * DV_A
CONV_A = 4
W_B = 256
LRU_BLOCKS = 4
LRU_BLOCK = W_B // LRU_BLOCKS
CONV_B = 4
LRU_C = 8.0
H_C, DK_C, DV_C = 6, 32, 64
QK_C = H_C * DK_C
W_C = H_C * DV_C
GLA_RANK = 16
GLA_TAU = 16.0
D_MIX = W_A + W_B + W_C
CHUNK = 64
D_FF = 2816
CONV_F = 3
EPS = 1e-6
CONV_A_WIDTH = 2 * QK_A + W_A
DEPTH = 2
ALPHA = (2.0 * DEPTH) ** 0.25

LANES = 128
SUBLANES = 8
HALF = LANES // 2
GROUP_ROWS = 64
STACK = 2 * GROUP_ROWS
N_PAIRS = 3
HIST = SUBLANES
FF_CHUNK = 256
N_FF_CHUNKS = D_FF // FF_CHUNK
VMEM_LIMIT_BYTES = 56 * 1024 * 1024
PROMPT_TT = 512
PROMPT_NSEQ = 1
MASK_ROWS = 256
FFN_PROMPT_TT = 512
SAMPLE_NSEQ = 16

OFF_QA = 0
OFF_KA = OFF_QA + QK_A
OFF_VA = OFF_KA + QK_A
OFF_ZA = OFF_VA + W_A
OFF_XB = OFF_ZA + W_A
OFF_GB = OFF_XB + W_B
OFF_QC = OFF_GB + W_B
QKC_PAD = 2 * LANES
OFF_KC = OFF_QC + QKC_PAD
OFF_VC = OFF_KC + QKC_PAD
OFF_ZC = OFF_VC + W_C
OFF_SMALL = OFF_ZC + W_C
D_IN_PAD = OFF_SMALL + LANES
LANE_BETA = 0
LANE_DECAY = 8
LANE_LOWRANK = 16

VEC_W = CONV_A_WIDTH
(ROW_CONV_A, ROW_CONV_B, ROW_CONV_B_BIAS, ROW_ALOG, ROW_DTB, ROW_NORM_A, ROW_BR, ROW_BI, ROW_LAM, ROW_B2,
 ROW_NORM_C, ROW_LN1_G, ROW_LN1_B) = (0, 4, 8, 9, 10, 11, 12, 13, 14, 15, 16, 17, 18)
MIX_VEC_ROWS = 24
(ROW_CONV_F, ROW_CONV_F_BIAS, ROW_LN2_G, ROW_LN2_B) = (0, 3, 4, 5)
FFN_VEC_ROWS = 8


def _bdot(a, b):
    return jnp.dot(a, b, preferred_element_type=F32)


def _mm(a, b):
    return _bdot(a.astype(BF16), b.astype(BF16))


def _mm_nt(a, b):
    return lax.dot_general(a.astype(BF16), b.astype(BF16), (((1,), (1,)), ((), ())), preferred_element_type=F32)


def _split(a, n):
    parts = []
    r = a
    for i in range(n):
        p = r.astype(BF16)
        parts.append(p)
        if i + 1 < n:
            r = r - p.astype(F32)
    return parts


def _mm_mask_lhs(mask_bf16, a, n):
    out = None
    for p in _split(a, n):
        t = _bdot(mask_bf16, p)
        out = t if out is None else out + t
    return out


def _softplus(x):
    return jnp.maximum(x, 0.0) + jnp.log1p(jnp.exp(-jnp.abs(x)))


def _sigmoid(x):
    return 0.5 * jnp.tanh(0.5 * x) + 0.5


def _silu(x):
    h = 0.5 * x
    return h * jnp.tanh(h) + h


def _gelu_tanh(x):
    return 0.5 * x * (1.0 + jnp.tanh(math.sqrt(2.0 / math.pi) * (x + 0.044715 * (x * x * x))))


def _layer_norm(x, g, b):
    mu = jnp.mean(x, axis=-1, keepdims=True)
    xc = x - mu
    var = jnp.mean(xc * xc, axis=-1, keepdims=True)
    return xc * lax.rsqrt(var + EPS) * g + b


def _iota(shape, axis):
    return lax.broadcasted_iota(jnp.int32, shape, axis)


def _log2(n):
    l = int(math.log2(n))
    assert (1 << l) == n
    return l


def _conv_from_buffer(buf_ref, vec_ref, row0, width, tt, cols):
    y = None
    for d in range(width):
        tap = vec_ref[row0 + width - 1 - d:row0 + width - d, cols]
        term = buf_ref[:, HIST - d:HIST - d + tt, cols] * tap
        y = term if y is None else y + term
    return y


def _stack_rows(n_sb, c, pieces):
    if n_sb == 1:
        return jnp.concatenate([pieces[0], pieces[1]], axis=0)
    out = []
    for s in range(n_sb):
        for a in range(2):
            out.append(pieces[a][s * c:(s + 1) * c])
    return jnp.concatenate(out, axis=0)


def _unstack_rows(o, n_sb, c):
    lo = _iota((c, LANES), 1) < HALF
    out = []
    for s in range(n_sb):
        base = s * 2 * c
        out.append(jnp.where(lo, o[base:base + c], o[base + c:base + 2 * c]))
    return out[0] if n_sb == 1 else jnp.concatenate(out, axis=0)


def _col_bcast_stack(vals, lane0, n_sb, c):
    pieces = [jnp.broadcast_to(vals[:, lane0 + a:lane0 + a + 1], (GROUP_ROWS, LANES)) for a in range(2)]
    return _stack_rows(n_sb, c, pieces)


def _dup_values(v):
    lo = _iota(v.shape, 1) < HALF
    vr = pltpu.roll(v, HALF, axis=1)
    return jnp.where(lo, v, vr), jnp.where(lo, vr, v)


def _tile_masks(rows, c):
    lc = _log2(c)
    i = np.arange(rows)[:, None]
    j = np.arange(rows)[None, :]
    same = (i >> lc) == (j >> lc)
    out = [same & (j <= i), same]
    for lvl in range(lc):
        lh = lc - 1 - lvl
        hi = (i >> lh) & 1
        start = (i >> lh) << lh
        nxt = ((i >> lh) + 1) << lh
        out.append((hi == 1) & (j >= start) & (j <= i))
        out.append((hi == 0) & (j > i) & (j < nxt))
    return jnp.asarray(np.stack(out).astype(np.float32), dtype=BF16)


N_MIXER_WEIGHTS = 7
N_MIXER_OUTS = 5
N_MIXER_SCRATCH = 35


def _mixer_stage(stage, x_ref, state_refs, weight_refs, out_refs, scratch_refs, write_x1, *, n_seq, tt, c):
    has_state = len(state_refs) > 0
    (masks_ref, w_in_ref, vec_ref, wr_ref, wi_ref, w2_ref, w_out_ref) = weight_refs
    (dconv_out_ref, sd_out_ref, lconv_out_ref, h_out_ref, sg_out_ref) = out_refs
    (sd_s, sg_s, abuf_s, bbuf_s, h_s, heads_s, xb_s,
     qa_s, ka_s, va_s, beta_s, gcum_s, gtot_s,
     qc_s, kc_s, vc_s, qb_s, kdec_s, btot_s, qt_s, kt_s,
     t_s, p_s, rhs_s, qe_s, qkd_s, kdst_s, gtst_s, oprime_s, qprime_s, c_s, kw_s, av_s, qbst_s, gk_s,
     ) = scratch_refs

    rows = n_seq * tt
    n_groups = rows // GROUP_ROWS
    n_sb = GROUP_ROWS // c
    lc = _log2(c)
    n_levels = lc
    a_tail = slice(HIST - (CONV_A - 1), HIST)
    b_tail = slice(HIST - (CONV_B - 1), HIST)

    if stage == 'init':
        if has_state:
            dconv_ref, sd_ref, lconv_ref, h0_ref, sg_ref = state_refs
            abuf_s[:, a_tail, :] = dconv_ref[...]
            bbuf_s[:, b_tail, :] = lconv_ref[...]
            h_s[...] = h0_ref[...].reshape(n_seq, W_B)
            zeros = jnp.zeros((GROUP_ROWS, LANES), F32)
            for s in range(n_seq):
                for p in range(N_PAIRS):
                    sd = sd_ref[s, p]
                    sd_s[s, p] = jnp.concatenate([sd, sd], axis=-1)
                    sg = sg_ref[s, 2 * DK_C * p:2 * DK_C * (p + 1), :]
                    sg = jnp.concatenate([sg, sg], axis=-1)
                    sg_s[s, p] = jnp.concatenate([sg, zeros] if p % 2 == 0 else [zeros, sg], axis=0)
        else:
            abuf_s[:, a_tail, :] = jnp.zeros((n_seq, CONV_A - 1, CONV_A_WIDTH), F32)
            bbuf_s[:, b_tail, :] = jnp.zeros((n_seq, CONV_B - 1, W_B), F32)
            h_s[...] = jnp.zeros((n_seq, W_B), F32)
            sd_s[...] = jnp.zeros(sd_s.shape, F32)
            sg_s[...] = jnp.zeros(sg_s.shape, F32)
        return

    if stage == 'final':
        dconv_out_ref[...] = abuf_s[:, tt + HIST - (CONV_A - 1):tt + HIST, :]
        lconv_out_ref[...] = bbuf_s[:, tt + HIST - (CONV_B - 1):tt + HIST, :]
        h_out_ref[...] = h_s[...].reshape(n_seq, 1, W_B)
        for s in range(n_seq):
            for p in range(N_PAIRS):
                sd_out_ref[s, p] = sd_s[s, p][:, 0:HALF]
                r0 = GROUP_ROWS * (p % 2)
                sg_out_ref[s, 2 * DK_C * p:2 * DK_C * (p + 1), :] = sg_s[s, p][r0:r0 + GROUP_ROWS, 0:HALF]
        return

    def vec(row, width):
        return vec_ref[row:row + 1, 0:width]

    x = x_ref[...].reshape(rows, D_MODEL)
    xb_s[...] = x.astype(BF16)
    proj = _bdot(xb_s[...], w_in_ref[...])
    m_cum = masks_ref[0]
    m_tot = masks_ref[1]
    mask_rows = masks_ref.shape[1]

    def masked_sum(mask, a, n):
        out = [_mm_mask_lhs(mask, a[r:r + mask_rows], n) for r in range(0, rows, mask_rows)]
        return out[0] if len(out) == 1 else jnp.concatenate(out, axis=0)

    abuf_s[:, HIST:, :] = proj[:, OFF_QA:OFF_QA + CONV_A_WIDTH].reshape(n_seq, tt, CONV_A_WIDTH)
    conv_a = _conv_from_buffer(abuf_s, vec_ref, ROW_CONV_A, CONV_A, tt, slice(0, CONV_A_WIDTH))
    qkv = _silu(conv_a.reshape(rows, CONV_A_WIDTH))
    abuf_s[:, a_tail, :] = abuf_s[:, tt + HIST - (CONV_A - 1):tt + HIST, :]
    p_i = _iota((LANES, LANES), 0) >> _log2(DK_A)
    p_j = _iota((LANES, LANES), 1) >> _log2(DK_A)
    pair_ones = jnp.where(p_i == p_j, 1.0, 0.0).astype(BF16)
    for p in range(N_PAIRS):
        for off, dst, scale in ((0, qa_s, DK_A ** -0.5), (QK_A, ka_s, 1.0)):
            v = qkv[:, off + p * LANES:off + (p + 1) * LANES]
            ssq = _bdot((v * v).astype(BF16), pair_ones)
            dst[:, p * LANES:(p + 1) * LANES] = v * (lax.rsqrt(ssq + EPS) * scale)
    va_s[...] = qkv[:, 2 * QK_A:]

    small = proj[:, OFF_SMALL:OFF_SMALL + LANES]
    beta_s[...] = _sigmoid(small)
    g_full = -jnp.exp(vec(ROW_ALOG, LANES)) * _softplus(small + vec(ROW_DTB, LANES))
    gcum_s[...] = masked_sum(m_cum, g_full, 2)
    gtot_s[...] = masked_sum(m_tot, g_full, 2)

    def mixer_b():
        bbuf_s[:, HIST:, :] = proj[:, OFF_XB:OFF_XB + W_B].reshape(n_seq, tt, W_B)
        xc = (_conv_from_buffer(bbuf_s, vec_ref, ROW_CONV_B, CONV_B, tt, slice(0, W_B)).reshape(rows, W_B)
              + vec(ROW_CONV_B_BIAS, W_B))
        bbuf_s[:, b_tail, :] = bbuf_s[:, tt + HIST - (CONV_B - 1):tt + HIST, :]
        gate_r = _sigmoid(_mm(xc, wr_ref[...]) + vec(ROW_BR, W_B))
        gate_i = _sigmoid(_mm(xc, wi_ref[...]) + vec(ROW_BI, W_B))
        log_a = -LRU_C * gate_r * _softplus(-vec(ROW_LAM, W_B))
        a_t = jnp.exp(log_a)
        b_t = jnp.sqrt(-jnp.tanh(log_a) * (a_t * a_t + 1.0)) * (gate_i * xc)
        i_seq = _iota((rows, W_B), 0) & (tt - 1)
        h_prev = jnp.concatenate([jnp.broadcast_to(h_s[s:s + 1, :], (tt, W_B)) for s in range(n_seq)], axis=0)
        b_t = b_t + jnp.where(i_seq == 0, a_t * h_prev, 0.0)
        d = 1
        while d < tt:
            a_sh = pltpu.roll(a_t, d, axis=0)
            b_sh = pltpu.roll(b_t, d, axis=0)
            ok = i_seq >= d
            b_t = jnp.where(ok, a_t * b_sh + b_t, b_t)
            a_t = jnp.where(ok, a_t * a_sh, a_t)
            d *= 2
        h_s[...] = jnp.concatenate([b_t[(s + 1) * tt - 1:(s + 1) * tt] for s in range(n_seq)], axis=0)
        heads_s[:, W_A:W_A + W_B] = (b_t * _gelu_tanh(proj[:, OFF_GB:OFF_GB + W_B])).astype(BF16)

    qc = proj[:, OFF_QC:OFF_QC + QKC_PAD] * (DK_C ** -0.5)
    kc = proj[:, OFF_KC:OFF_KC + QKC_PAD]
    logf = -_softplus(-(_mm(small, w2_ref[...]) + vec(ROW_B2, QKC_PAD))) * (1.0 / GLA_TAU)
    b_cum = masked_sum(m_cum, logf, 2)
    b_tot = masked_sum(m_tot, logf, 2)
    qc_s[...] = qc
    kc_s[...] = kc
    vc_s[...] = proj[:, OFF_VC:OFF_VC + W_C]
    qb_s[...] = qc * jnp.exp(b_cum)
    kdec_s[...] = kc * jnp.exp(b_tot - b_cum)
    btot_s[...] = b_tot
    for lvl in range(n_levels):
        qt_s[lvl] = qc * jnp.exp(masked_sum(masks_ref[2 + 2 * lvl], logf, 1))
        kt_s[lvl] = kc * jnp.exp(masked_sum(masks_ref[3 + 2 * lvl], logf, 1))

    s_i = _iota((STACK, STACK), 0)
    s_j = _iota((STACK, STACK), 1)
    same_blk = (s_i >> lc) == (s_j >> lc)
    m_incl = same_blk & (s_j <= s_i)
    m_strict = same_blk & (s_j < s_i)
    eye = jnp.where(s_i == s_j, 1.0, 0.0)
    lane = _iota((GROUP_ROWS, LANES), 1)
    m0 = lane < HALF
    norm_a = vec(ROW_NORM_A, LANES)
    norm_c = vec(ROW_NORM_C, LANES)

    def delta_setup(gi):
        rs = pl.ds(gi * GROUP_ROWS, GROUP_ROWS)
        beta_g = beta_s[rs, :]
        gcum_g = gcum_s[rs, :]
        gtot_g = gtot_s[rs, :]
        for p in range(N_PAIRS):
            sl = pl.ds(p * LANES, LANES)
            q_p = qa_s[rs, sl]
            k_p = ka_s[rs, sl]
            v0, v1 = _dup_values(va_s[rs, sl])
            q_st = _stack_rows(n_sb, c, [jnp.where(m0, q_p, 0.0), jnp.where(m0, 0.0, q_p)])
            k_st = _stack_rows(n_sb, c, [jnp.where(m0, k_p, 0.0), jnp.where(m0, 0.0, k_p)])
            v_st = _stack_rows(n_sb, c, [v0, v1])
            beta_c = _col_bcast_stack(beta_g, LANE_BETA + 2 * p, n_sb, c)
            g_c = _col_bcast_stack(gcum_g, LANE_DECAY + 2 * p, n_sb, c)
            gt_c = _col_bcast_stack(gtot_g, LANE_DECAY + 2 * p, n_sb, c)
            dec = jnp.exp(jnp.where(m_incl, g_c - g_c.T, -1e30))
            kq = _mm_nt(jnp.concatenate([k_st, q_st], axis=0), k_st)
            kk, qk = kq[0:STACK], kq[STACK:]
            n_mat = -(beta_c * kk * jnp.where(m_strict, dec, 0.0))
            eg = jnp.exp(g_c)
            t_s[gi, p] = eye + n_mat
            p_s[gi, p] = n_mat.astype(BF16)
            rhs_s[gi, p] = jnp.concatenate([beta_c * v_st, beta_c * eg * k_st], axis=1).astype(BF16)
            qe_s[gi, p] = (eg * q_st).astype(BF16)
            qkd_s[gi, p] = (qk * dec).astype(BF16)
            kdst_s[gi, p] = k_st * jnp.exp(gt_c - g_c)
            gtst_s[gi, p] = gt_c

    def inverse_level(gi, j):
        for p in range(N_PAIRS):
            pw = p_s[gi, p]
            if j >= 2:
                t_old = t_s[gi, p]
                both = _bdot(jnp.concatenate([pw, t_old.astype(BF16)], axis=0), pw)
                t_s[gi, p] = t_old + both[STACK:]
                p_s[gi, p] = both[0:STACK].astype(BF16)
            else:
                p_s[gi, p] = _bdot(pw, pw).astype(BF16)

    def delta_solve(gi):
        for p in range(N_PAIRS):
            t_old = t_s[gi, p]
            t_fin = t_old + _bdot(t_old.astype(BF16), p_s[gi, p])
            uwk = _bdot(t_fin.astype(BF16), rhs_s[gi, p]).astype(BF16)
            k_dec = kdst_s[gi, p]
            if n_sb == 1:
                both = _bdot(jnp.concatenate([qkd_s[gi, p], k_dec.T.astype(BF16)], axis=0), uwk)
                o_qw, c_kw = both[0:STACK], [both[STACK:]]
            else:
                o_qw = _bdot(qkd_s[gi, p], uwk)
                c_kw = [_bdot(k_dec[sb * 2 * c:(sb + 1) * 2 * c].T.astype(BF16), uwk[sb * 2 * c:(sb + 1) * 2 * c])
                        for sb in range(n_sb)]
            oprime_s[gi, p] = o_qw[:, 0:LANES]
            qprime_s[gi, p] = (qe_s[gi, p].astype(F32) - o_qw[:, LANES:]).astype(BF16)
            for sb in range(n_sb):
                c_s[gi, p, sb] = c_kw[sb][:, 0:LANES]
                kw_s[gi, p, sb] = c_kw[sb][:, LANES:].astype(BF16)

    def gla_precompute(gi):
        rs = pl.ds(gi * GROUP_ROWS, GROUP_ROWS)
        for p in range(N_PAIRS):
            sl = pl.ds(p * LANES, LANES)
            slab = pl.ds((p // 2) * LANES, LANES)
            l0 = 2 * DK_C * (p % 2)
            mh0 = (lane >= l0) & (lane < l0 + DK_C)
            mh1 = (lane >= l0 + DK_C) & (lane < l0 + 2 * DK_C)

            def stack_c(ref, idx=None):
                val = ref[rs, slab] if idx is None else ref[idx, rs, slab]
                return _stack_rows(n_sb, c, [jnp.where(mh0, val, 0.0), jnp.where(mh1, val, 0.0)])

            att = jnp.where(s_i == s_j, _mm_nt(stack_c(qc_s), stack_c(kc_s)), 0.0)
            for lvl in range(n_levels):
                lh = lc - 1 - lvl
                valid = (((s_i >> (lh + 1)) == (s_j >> (lh + 1)))
                         & (((s_i >> lh) & 1) == 1) & (((s_j >> lh) & 1) == 0))
                att = att + jnp.where(valid, _mm_nt(stack_c(qt_s, lvl), stack_c(kt_s, lvl)), 0.0)
            vc0, vc1 = _dup_values(vc_s[rs, sl])
            vc_st = _stack_rows(n_sb, c, [vc0, vc1]).astype(BF16)
            qbst_s[gi, p] = stack_c(qb_s).astype(BF16)
            kd_st = stack_c(kdec_s)
            if n_sb == 1:
                both = _bdot(jnp.concatenate([att.astype(BF16), kd_st.T.astype(BF16)], axis=0), vc_st)
                av_s[gi, p] = both[0:STACK]
                gk_s[gi, p, 0] = both[STACK:]
            else:
                av_s[gi, p] = _bdot(att.astype(BF16), vc_st)
                for sb in range(n_sb):
                    blk = slice(sb * 2 * c, (sb + 1) * 2 * c)
                    gk_s[gi, p, sb] = _bdot(kd_st[blk].T.astype(BF16), vc_st[blk])

    for gi in range(n_groups):
        delta_setup(gi)
    for j in range(1, lc):
        for gi in range(n_groups):
            inverse_level(gi, j)
        if j <= n_groups:
            gla_precompute(j - 1)
        if j == 1:
            mixer_b()
    for gi in range(lc - 1, n_groups):
        gla_precompute(gi)
    for gi in range(n_groups):
        delta_solve(gi)

    def recurrence_body(gi):
        rs = pl.ds(gi * GROUP_ROWS, GROUP_ROWS)
        for p in range(N_PAIRS):
            sl = pl.ds(p * LANES, LANES)
            q_prime = qprime_s[gi, p]
            gt_c = gtst_s[gi, p]
            qs_parts = []
            for sb in range(n_sb):
                seq = (gi * GROUP_ROWS + sb * c) // tt
                blk = slice(sb * 2 * c, (sb + 1) * 2 * c)
                s_old = sd_s[seq, p]
                s_bf = s_old.astype(BF16)
                if n_sb == 1:
                    both = _bdot(jnp.concatenate([q_prime, kw_s[gi, p, 0]], axis=0), s_bf)
                    qs, kws = both[0:STACK], both[STACK:]
                else:
                    qs, kws = _bdot(q_prime[blk], s_bf), _bdot(kw_s[gi, p, sb], s_bf)
                qs_parts.append(qs)
                decay_rows = jnp.concatenate(
                    [jnp.broadcast_to(gt_c[sb * 2 * c + a * c:sb * 2 * c + a * c + 1, :], (GROUP_ROWS, LANES))
                     for a in range(2)], axis=0)
                sd_s[seq, p] = jnp.exp(decay_rows) * s_old + (c_s[gi, p, sb] - kws)
            qs_all = qs_parts[0] if n_sb == 1 else jnp.concatenate(qs_parts, axis=0)
            o_st = qs_all + oprime_s[gi, p]
            o_st = o_st * lax.rsqrt(jnp.mean(o_st * o_st, axis=-1, keepdims=True) + EPS) * norm_a
            z_a = proj[gi * GROUP_ROWS:(gi + 1) * GROUP_ROWS, OFF_ZA + p * LANES:OFF_ZA + (p + 1) * LANES]
            heads_s[rs, sl] = (_unstack_rows(o_st, n_sb, c) * _silu(z_a)).astype(BF16)
            slab = pl.ds((p // 2) * LANES, LANES)
            qb_st = qbst_s[gi, p]
            btot_t = btot_s[rs, slab].T
            oi_parts = []
            for sb in range(n_sb):
                seq = (gi * GROUP_ROWS + sb * c) // tt
                blk = slice(sb * 2 * c, (sb + 1) * 2 * c)
                s_old = sg_s[seq, p]
                oi_parts.append(_bdot(qb_st[blk], s_old.astype(BF16)))
                decay_col = jnp.broadcast_to(btot_t[:, sb * c:sb * c + 1], (LANES, LANES))
                sg_s[seq, p] = jnp.exp(decay_col) * s_old + gk_s[gi, p, sb]
            oi_all = oi_parts[0] if n_sb == 1 else jnp.concatenate(oi_parts, axis=0)
            oc_st = oi_all + av_s[gi, p]
            oc_st = oc_st * lax.rsqrt(jnp.mean(oc_st * oc_st, axis=-1, keepdims=True) + EPS) * norm_c
            z_c = proj[gi * GROUP_ROWS:(gi + 1) * GROUP_ROWS, OFF_ZC + p * LANES:OFF_ZC + (p + 1) * LANES]
            heads_s[rs, pl.ds(W_A + W_B + p * LANES, LANES)] = (
                _unstack_rows(oc_st, n_sb, c) * _silu(z_c)).astype(BF16)

    groups_per_seq = max(tt // GROUP_ROWS, 1)
    for k in range(groups_per_seq):
        for gi in range(k, n_groups, groups_per_seq):
            recurrence_body(gi)

    y = ALPHA * x + _bdot(heads_s[...], w_out_ref[...])
    write_x1(_layer_norm(y, vec(ROW_LN1_G, D_MODEL), vec(ROW_LN1_B, D_MODEL)))


def _mixer_scratch(n_seq, tt, c):
    rows = n_seq * tt
    n_levels = _log2(c)
    n_groups = rows // GROUP_ROWS
    scratch = [
        pltpu.VMEM((n_seq, N_PAIRS, STACK, LANES), F32),
        pltpu.VMEM((n_seq, N_PAIRS, LANES, LANES), F32),
        pltpu.VMEM((n_seq, HIST + tt, CONV_A_WIDTH), F32),
        pltpu.VMEM((n_seq, HIST + tt, W_B), F32),
        pltpu.VMEM((n_seq, W_B), F32),
        pltpu.VMEM((rows, D_MIX), BF16),
        pltpu.VMEM((rows, D_MODEL), BF16),
        pltpu.VMEM((rows, QK_A), F32),
        pltpu.VMEM((rows, QK_A), F32),
        pltpu.VMEM((rows, W_A), F32),
        pltpu.VMEM((rows, LANES), F32),
        pltpu.VMEM((rows, LANES), F32),
        pltpu.VMEM((rows, LANES), F32),
        pltpu.VMEM((rows, QKC_PAD), F32),
        pltpu.VMEM((rows, QKC_PAD), F32),
        pltpu.VMEM((rows, W_C), F32),
        pltpu.VMEM((rows, QKC_PAD), F32),
        pltpu.VMEM((rows, QKC_PAD), F32),
        pltpu.VMEM((rows, QKC_PAD), F32),
        pltpu.VMEM((n_levels, rows, QKC_PAD), F32),
        pltpu.VMEM((n_levels, rows, QKC_PAD), F32),
        pltpu.VMEM((n_groups, N_PAIRS, STACK, STACK), F32),
        pltpu.VMEM((n_groups, N_PAIRS, STACK, STACK), BF16),
        pltpu.VMEM((n_groups, N_PAIRS, STACK, 2 * LANES), BF16),
        pltpu.VMEM((n_groups, N_PAIRS, STACK, LANES), BF16),
        pltpu.VMEM((n_groups, N_PAIRS, STACK, STACK), BF16),
        pltpu.VMEM((n_groups, N_PAIRS, STACK, LANES), F32),
        pltpu.VMEM((n_groups, N_PAIRS, STACK, LANES), F32),
        pltpu.VMEM((n_groups, N_PAIRS, STACK, LANES), F32),
        pltpu.VMEM((n_groups, N_PAIRS, STACK, LANES), BF16),
        pltpu.VMEM((n_groups, N_PAIRS, GROUP_ROWS // c, LANES, LANES), F32),
        pltpu.VMEM((n_groups, N_PAIRS, GROUP_ROWS // c, LANES, LANES), BF16),
        pltpu.VMEM((n_groups, N_PAIRS, STACK, LANES), F32),
        pltpu.VMEM((n_groups, N_PAIRS, STACK, LANES), BF16),
        pltpu.VMEM((n_groups, N_PAIRS, GROUP_ROWS // c, LANES, LANES), F32),
    ]
    assert len(scratch) == N_MIXER_SCRATCH
    return scratch


def _ffn_stage(stage, x, state_ref, weight_refs, y_ref, tail_ref, scratch_refs, *, n_seq, tt):
    (w_up_ref, vec_ref, wd_ref) = weight_refs
    (gbuf_s, h_s, xb_s) = scratch_refs
    rows = n_seq * tt
    tail = slice(HIST - (CONV_F - 1), HIST)

    if stage == 'init':
        if state_ref is not None:
            gbuf_s[:, tail, :] = state_ref[...]
        else:
            gbuf_s[:, tail, :] = jnp.zeros((n_seq, CONV_F - 1, D_FF), F32)
        return
    if stage == 'final':
        tail_ref[...] = gbuf_s[:, tt + HIST - (CONV_F - 1):tt + HIST, :]
        return

    xb_s[...] = x.astype(BF16)
    for j in range(N_FF_CHUNKS):
        cols = slice(j * FF_CHUNK, (j + 1) * FF_CHUNK)
        gate = _bdot(xb_s[...], w_up_ref[:, cols])
        val = _bdot(xb_s[...], w_up_ref[:, D_FF + j * FF_CHUNK:D_FF + (j + 1) * FF_CHUNK])
        gbuf_s[:, HIST:, cols] = gate.reshape(n_seq, tt, FF_CHUNK)
        conv = _conv_from_buffer(gbuf_s, vec_ref, ROW_CONV_F, CONV_F, tt, cols).reshape(rows, FF_CHUNK)
        h = _gelu_tanh(conv + vec_ref[ROW_CONV_F_BIAS:ROW_CONV_F_BIAS + 1, cols]) * val
        h_s[:, cols] = h.astype(BF16)
    gbuf_s[:, tail, :] = gbuf_s[:, tt + HIST - (CONV_F - 1):tt + HIST, :]
    y = ALPHA * x + _bdot(h_s[...], wd_ref[...])
    ln_g = vec_ref[ROW_LN2_G:ROW_LN2_G + 1, 0:D_MODEL]
    ln_b = vec_ref[ROW_LN2_B:ROW_LN2_B + 1, 0:D_MODEL]
    y_ref[...] = _layer_norm(y, ln_g, ln_b).reshape(n_seq, tt, D_MODEL)


def _mixer_kernel(*refs, n_seq, tt, c, has_state, n_alias):
    n_state = N_MIXER_OUTS if has_state else 0
    k = 1 + n_state
    x_ref, state_refs = refs[0], refs[1:k]
    weight_refs = refs[k:k + N_MIXER_WEIGHTS]
    k += N_MIXER_WEIGHTS + n_alias
    x1_ref = refs[k]
    out_refs = refs[k + 1:k + 1 + N_MIXER_OUTS]
    scratch_refs = refs[k + 1 + N_MIXER_OUTS:]
    t_idx = pl.program_id(1)

    def write_x1(val):
        x1_ref[...] = val.reshape(n_seq, tt, D_MODEL)

    stage = functools.partial(_mixer_stage, x_ref=x_ref, state_refs=state_refs, weight_refs=weight_refs,
                              out_refs=out_refs, scratch_refs=scratch_refs, write_x1=write_x1,
                              n_seq=n_seq, tt=tt, c=c)
    pl.when(t_idx == 0)(lambda: stage('init'))
    stage('body')
    pl.when(t_idx == pl.num_programs(1) - 1)(lambda: stage('final'))


def _ffn_kernel(*refs, n_seq, tt, has_state, n_alias):
    k = 2 if has_state else 1
    x_ref = refs[0]
    state_ref = refs[1] if has_state else None
    weight_refs = refs[k:k + 3]
    k += 3 + n_alias
    y_ref, tail_ref = refs[k:k + 2]
    scratch_refs = refs[k + 2:]
    t_idx = pl.program_id(1)
    stage = functools.partial(_ffn_stage, state_ref=state_ref, weight_refs=weight_refs, y_ref=y_ref,
                              tail_ref=tail_ref, scratch_refs=scratch_refs, n_seq=n_seq, tt=tt)
    pl.when(t_idx == 0)(lambda: stage('init', None))
    stage('body', x_ref[...].reshape(n_seq * tt, D_MODEL))
    pl.when(t_idx == pl.num_programs(1) - 1)(lambda: stage('final', None))


def _block_call(kern, name, x, states, state_shapes, weights, scratch, layer, prev_outs, *, n_seq, tt):
    nb, t_total, _ = x.shape
    assert nb % n_seq == 0 and t_total % tt == 0
    x_spec = pl.BlockSpec((n_seq, tt, D_MODEL), lambda b, t: (b, t, 0))
    in_specs, operands = [x_spec], [x]
    if states is not None:
        for arr, tail in zip(states, state_shapes):
            assert arr.shape[2:] == tail
            in_specs.append(pl.BlockSpec((None, n_seq) + tail, lambda b, t, z=(0,) * len(tail): (layer, b) + z))
            operands.append(arr)
    for a, per_layer in weights:
        if per_layer:
            in_specs.append(pl.BlockSpec((None,) + a.shape[1:], lambda b, t, z=(0,) * (a.ndim - 1): (layer,) + z))
        else:
            in_specs.append(pl.BlockSpec(a.shape, lambda b, t, z=(0,) * a.ndim: z))
        operands.append(a)
    aliases = {}
    if prev_outs is not None:
        for i, arr in enumerate(prev_outs):
            aliases[len(operands)] = 1 + i
            in_specs.append(pl.BlockSpec(memory_space=pl.ANY))
            operands.append(arr)
    out_shape = [jax.ShapeDtypeStruct((nb, t_total, D_MODEL), F32)]
    out_specs = [x_spec]
    for tail in state_shapes:
        out_shape.append(jax.ShapeDtypeStruct((DEPTH, nb) + tail, F32))
        out_specs.append(pl.BlockSpec((None, n_seq) + tail, lambda b, t, z=(0,) * len(tail): (layer, b) + z))
    return pl.pallas_call(
        functools.partial(kern, n_alias=len(aliases)),
        grid=(nb // n_seq, t_total // tt),
        in_specs=in_specs,
        out_specs=out_specs,
        out_shape=out_shape,
        input_output_aliases=aliases,
        scratch_shapes=scratch,
        compiler_params=pltpu.CompilerParams(
            dimension_semantics=("arbitrary", "arbitrary"), vmem_limit_bytes=VMEM_LIMIT_BYTES),
        name=f"{name}_nseq{n_seq}_tt{tt}",
    )(*operands)


def _mixer_call(x, states, w, layer, prev_outs, *, n_seq, tt, c):
    assert (tt % MASK_ROWS == 0 and c == GROUP_ROWS) or (tt == HIST and c == HIST and n_seq * tt <= MASK_ROWS)
    assert (n_seq * tt) % GROUP_ROWS == 0
    state_shapes = ((CONV_A - 1, CONV_A_WIDTH), (N_PAIRS, 2 * DK_A, DV_A), (CONV_B - 1, W_B), (1, W_B),
                    (QK_C, DV_C))
    weights = [(_tile_masks(min(n_seq * tt, MASK_ROWS), c), False)] + [
        (w[k], True) for k in ('w_in', 'mix_vec', 'wr', 'wi', 'w2', 'w_out')]
    kern = functools.partial(_mixer_kernel, n_seq=n_seq, tt=tt, c=c, has_state=states is not None)
    return _block_call(kern, 'mixer', x, states, state_shapes, weights, _mixer_scratch(n_seq, tt, c), layer,
                       prev_outs, n_seq=n_seq, tt=tt)


def _ffn_call(x, state, w, layer, prev_out, *, n_seq, tt):
    weights = [(w[k], True) for k in ('w_up', 'ffn_vec', 'w_down')]
    scratch = [pltpu.VMEM((n_seq, HIST + tt, D_FF), F32),
               pltpu.VMEM((n_seq * tt, D_FF), BF16),
               pltpu.VMEM((n_seq * tt, D_MODEL), BF16)]
    kern = functools.partial(_ffn_kernel, n_seq=n_seq, tt=tt, has_state=state is not None)
    return _block_call(kern, 'ffn', x, None if state is None else (state,), ((CONV_F - 1, D_FF),), weights,
                       scratch, layer, None if prev_out is None else (prev_out,), n_seq=n_seq, tt=tt)


def _vec_table(pieces, width, n_rows):
    rows = []
    used = 0
    for arr, lane0 in pieces:
        arr = arr.astype(F32)
        rows.append(jnp.pad(arr, ((0, 0), (0, 0), (lane0, width - lane0 - arr.shape[2]))))
        used += arr.shape[1]
    depth = pieces[0][0].shape[0]
    rows.append(jnp.zeros((depth, n_rows - used, width), F32))
    return jnp.concatenate(rows, axis=1)


def _prep_weights(w_in, conv_a_w, a_log, dt_bias, norm_a_w, conv_b_w, conv_b_b, lru_w_r, lru_b_r, lru_w_i,
                  lru_b_i, lru_lambda, gla_w2, gla_b2, norm_c_w, w_out, ln1_g, ln1_b, ffn_w_up, ffn_conv_w,
                  ffn_conv_b, ffn_w_down, ln2_g, ln2_b):
    depth, d, _ = w_in.shape
    pts = [0]
    for s in (QK_A, QK_A, W_A, W_A, H_A, H_A, W_B, W_B, QK_C, QK_C, W_C, W_C, GLA_RANK):
        pts.append(pts[-1] + s)
    (qa, ka, va, za, ba, aa, xb, gb, qc, kc, vc, zc, lc) = [w_in[:, :, pts[i]:pts[i + 1]] for i in range(13)]
    z = lambda n: jnp.zeros((depth, d, n), w_in.dtype)
    small = jnp.concatenate([ba, z(LANE_DECAY - H_A), aa, z(LANE_LOWRANK - LANE_DECAY - H_A), lc,
                             z(LANES - LANE_LOWRANK - GLA_RANK)], axis=2)
    w_in_p = jnp.concatenate([qa, ka, va, za, xb, gb, qc, z(QKC_PAD - QK_C), kc, z(QKC_PAD - QK_C), vc, zc, small],
                             axis=2).astype(BF16)
    assert w_in_p.shape[2] == D_IN_PAD

    eye_blocks = jnp.eye(LRU_BLOCKS, dtype=F32)[None, :, None, :, None]

    def block_diag(wb):
        return (wb[:, :, :, None, :] * eye_blocks).reshape(depth, W_B, W_B).astype(BF16)

    w2 = jnp.pad(gla_w2, ((0, 0), (LANE_LOWRANK, LANES - LANE_LOWRANK - GLA_RANK), (0, QKC_PAD - QK_C))).astype(BF16)
    r1 = lambda v: v[:, None, :]
    dup = lambda v: jnp.concatenate([v, v], axis=1)[:, None, :]
    mix_vec = _vec_table(
        [(conv_a_w, 0), (conv_b_w, 0), (r1(conv_b_b), 0), (r1(a_log), LANE_DECAY), (r1(dt_bias), LANE_DECAY),
         (dup(norm_a_w), 0), (r1(lru_b_r), 0), (r1(lru_b_i), 0), (r1(lru_lambda), 0), (r1(gla_b2), 0),
         (dup(norm_c_w), 0), (r1(ln1_g), 0), (r1(ln1_b), 0)], VEC_W, MIX_VEC_ROWS)
    ffn_vec = _vec_table([(ffn_conv_w, 0), (r1(ffn_conv_b), 0), (r1(ln2_g), 0), (r1(ln2_b), 0)], D_FF, FFN_VEC_ROWS)
    return {
        'w_in': w_in_p, 'mix_vec': mix_vec, 'wr': block_diag(lru_w_r), 'wi': block_diag(lru_w_i), 'w2': w2,
        'w_out': w_out.astype(BF16), 'w_up': ffn_w_up.astype(BF16), 'ffn_vec': ffn_vec,
        'w_down': ffn_w_down.astype(BF16),
    }


def _trunk(x, states, w, *, n_seq, tt, c):
    nb = x.shape[0]
    if states is not None:
        st_dconv, st_delta, st_lconv, st_lru, st_gla, st_fconv = (s.astype(F32) for s in states)
        states = (st_dconv, st_delta.reshape(DEPTH, nb, N_PAIRS, 2 * DK_A, DV_A), st_lconv,
                  st_lru.reshape(DEPTH, nb, 1, W_B), st_gla.reshape(DEPTH, nb, QK_C, DV_C), st_fconv)
    mix_outs = ffn_out = None
    for l in range(DEPTH):
        x1, *mix_outs = _mixer_call(x, None if states is None else states[:5], w, l, mix_outs,
                                    n_seq=n_seq, tt=tt, c=c)
        x, ffn_out = _ffn_call(x1, None if states is None else states[5], w, l, ffn_out, n_seq=n_seq,
                               tt=tt if n_seq > 1 else min(FFN_PROMPT_TT, x.shape[1]))
    dconv, sd, lconv, h, sg = mix_outs
    return x, (dconv, sd.reshape(DEPTH, nb, H_A, DK_A, DV_A), lconv, h.reshape(DEPTH, nb, W_B),
               sg.reshape(DEPTH, nb, H_C, DK_C, DV_C), ffn_out)


def kernel(x_prompt, x_sample, state_delta_conv, state_delta, state_lru_conv, state_lru, state_gla, state_ffn_conv,
           w_in, conv_a_w, a_log, dt_bias, norm_a_w, conv_b_w, conv_b_b, lru_w_r, lru_b_r, lru_w_i, lru_b_i,
           lru_lambda, gla_w2, gla_b2, norm_c_w, w_out, ln1_g, ln1_b, ffn_w_up, ffn_conv_w, ffn_conv_b, ffn_w_down,
           ln2_g, ln2_b):
    assert w_in.shape[0] == DEPTH
    w = _prep_weights(w_in, conv_a_w, a_log, dt_bias, norm_a_w, conv_b_w, conv_b_b, lru_w_r, lru_b_r, lru_w_i,
                      lru_b_i, lru_lambda, gla_w2, gla_b2, norm_c_w, w_out, ln1_g, ln1_b, ffn_w_up, ffn_conv_w,
                      ffn_conv_b, ffn_w_down, ln2_g, ln2_b)
    sample_states = (state_delta_conv, state_delta, state_lru_conv, state_lru, state_gla, state_ffn_conv)
    t_p = x_prompt.shape[1]
    t_s = x_sample.shape[1]
    n_p = PROMPT_NSEQ if x_prompt.shape[0] % PROMPT_NSEQ == 0 else 1
    y_p, p_st = _trunk(x_prompt, None, w, n_seq=n_p, tt=min(PROMPT_TT, t_p), c=CHUNK)
    y_s, s_st = _trunk(x_sample, sample_states, w, n_seq=min(SAMPLE_NSEQ, x_sample.shape[0]), tt=t_s, c=t_s)
    return (y_p, y_s) + p_st + s_st
```

```python
import functools
import math

import numpy as np
import jax
import jax.numpy as jnp
from jax import lax
from jax.experimental import pallas as pl
from jax.experimental.pallas import tpu as pltpu

F32 = jnp.float32
BF16 = jnp.bfloat16

D_MODEL = 1024
H_A, DK_A, DV_A = 6, 64, 64
QK_A = H_A * DK_A
W_A = H_A * DV_A
CONV_A = 4
W_B = 256
LRU_BLOCKS = 4
LRU_BLOCK = W_B // LRU_BLOCKS
CONV_B = 4
LRU_C = 8.0
H_C, DK_C, DV_C = 6, 32, 64
QK_C = H_C * DK_C
W_C = H_C * DV_C
GLA_RANK = 16
GLA_TAU = 16.0
D_MIX = W_A + W_B + W_C
CHUNK = 64
D_FF = 2816
CONV_F = 3
EPS = 1e-6
CONV_A_WIDTH = 2 * QK_A + W_A
DEPTH = 2
ALPHA = (2.0 * DEPTH) ** 0.25

LANES = 128
SUBLANES = 8
HALF = LANES // 2
GROUP_ROWS = 64
STACK = 2 * GROUP_ROWS
N_PAIRS = 3
HIST = SUBLANES
FF_CHUNK = 256
N_FF_CHUNKS = D_FF // FF_CHUNK
VMEM_LIMIT_BYTES = 56 * 1024 * 1024
PROMPT_TT = 512
PROMPT_NSEQ = 1
MASK_ROWS = 256
FFN_TILE_ROWS = 512
SAMPLE_NSEQ = 16

OFF_QA = 0
OFF_KA = OFF_QA + QK_A
OFF_VA = OFF_KA + QK_A
OFF_ZA = OFF_VA + W_A
OFF_XB = OFF_ZA + W_A
OFF_GB = OFF_XB + W_B
OFF_QC = OFF_GB + W_B
QKC_PAD = 2 * LANES
OFF_KC = OFF_QC + QKC_PAD
OFF_VC = OFF_KC + QKC_PAD
OFF_ZC = OFF_VC + W_C
OFF_SMALL = OFF_ZC + W_C
D_IN_PAD = OFF_SMALL + LANES
LANE_BETA = 0
LANE_DECAY = 8
LANE_LOWRANK = 16

VEC_W = CONV_A_WIDTH
(ROW_CONV_A, ROW_CONV_B, ROW_CONV_B_BIAS, ROW_ALOG, ROW_DTB, ROW_NORM_A, ROW_BR, ROW_BI, ROW_LAM, ROW_B2,
 ROW_NORM_C, ROW_LN1_G, ROW_LN1_B) = (0, 4, 8, 9, 10, 11, 12, 13, 14, 15, 16, 17, 18)
MIX_VEC_ROWS = 24
(ROW_CONV_F, ROW_CONV_F_BIAS, ROW_LN2_G, ROW_LN2_B) = (0, 3, 4, 5)
FFN_VEC_ROWS = 8


def _bdot(a, b):
    return jnp.dot(a, b, preferred_element_type=F32)


def _mm(a, b):
    return _bdot(a.astype(BF16), b.astype(BF16))


def _mm_nt(a, b):
    return lax.dot_general(a.astype(BF16), b.astype(BF16), (((1,), (1,)), ((), ())), preferred_element_type=F32)


def _split(a, n):
    parts = []
    r = a
    for i in range(n):
        p = r.astype(BF16)
        parts.append(p)
        if i + 1 < n:
            r = r - p.astype(F32)
    return parts


def _mm_mask_lhs(mask_bf16, a, n):
    out = None
    for p in _split(a, n):
        t = _bdot(mask_bf16, p)
        out = t if out is None else out + t
    return out


def _softplus(x):
    return jnp.maximum(x, 0.0) + jnp.log1p(jnp.exp(-jnp.abs(x)))


def _sigmoid(x):
    return 0.5 * jnp.tanh(0.5 * x) + 0.5


def _silu(x):
    h = 0.5 * x
    return h * jnp.tanh(h) + h


def _gelu_tanh(x):
    return 0.5 * x * (1.0 + jnp.tanh(math.sqrt(2.0 / math.pi) * (x + 0.044715 * (x * x * x))))


def _layer_norm(x, g, b):
    mu = jnp.mean(x, axis=-1, keepdims=True)
    xc = x - mu
    var = jnp.mean(xc * xc, axis=-1, keepdims=True)
    return xc * lax.rsqrt(var + EPS) * g + b


def _iota(shape, axis):
    return lax.broadcasted_iota(jnp.int32, shape, axis)


def _log2(n):
    l = int(math.log2(n))
    assert (1 << l) == n
    return l


def _conv_from_buffer(buf_ref, vec_ref, row0, width, tt, cols):
    y = None
    for d in range(width):
        tap = vec_ref[row0 + width - 1 - d:row0 + width - d, cols]
        term = buf_ref[:, HIST - d:HIST - d + tt, cols] * tap
        y = term if y is None else y + term
    return y


def _stack_rows(n_sb, c, pieces):
    if n_sb == 1:
        return jnp.concatenate([pieces[0], pieces[1]], axis=0)
    out = []
    for s in range(n_sb):
        for a in range(2):
            out.append(pieces[a][s * c:(s + 1) * c])
    return jnp.concatenate(out, axis=0)


def _unstack_rows(o, n_sb, c):
    lo = _iota((c, LANES), 1) < HALF
    out = []
    for s in range(n_sb):
        base = s * 2 * c
        out.append(jnp.where(lo, o[base:base + c], o[base + c:base + 2 * c]))
    return out[0] if n_sb == 1 else jnp.concatenate(out, axis=0)


def _col_bcast_stack(vals, lane0, n_sb, c):
    pieces = [jnp.broadcast_to(vals[:, lane0 + a:lane0 + a + 1], (GROUP_ROWS, LANES)) for a in range(2)]
    return _stack_rows(n_sb, c, pieces)


def _dup_values(v):
    lo = _iota(v.shape, 1) < HALF
    vr = pltpu.roll(v, HALF, axis=1)
    return jnp.where(lo, v, vr), jnp.where(lo, vr, v)


def _tile_masks(rows, c):
    lc = _log2(c)
    i = np.arange(rows)[:, None]
    j = np.arange(rows)[None, :]
    same = (i >> lc) == (j >> lc)
    out = [same & (j <= i), same]
    for lvl in range(lc):
        lh = lc - 1 - lvl
        hi = (i >> lh) & 1
        start = (i >> lh) << lh
        nxt = ((i >> lh) + 1) << lh
        out.append((hi == 1) & (j >= start) & (j <= i))
        out.append((hi == 0) & (j > i) & (j < nxt))
    return jnp.asarray(np.stack(out).astype(np.float32), dtype=BF16)


N_MIXER_WEIGHTS = 7
N_MIXER_OUTS = 5
N_MIXER_SCRATCH = 35


def _mixer_stage(stage, x_ref, state_refs, weight_refs, out_refs, scratch_refs, write_x1, *, n_seq, tt, c):
    has_state = len(state_refs) > 0
    (masks_ref, w_in_ref, vec_ref, wr_ref, wi_ref, w2_ref, w_out_ref) = weight_refs
    (dconv_out_ref, sd_out_ref, lconv_out_ref, h_out_ref, sg_out_ref) = out_refs
    (sd_s, sg_s, abuf_s, bbuf_s, h_s, heads_s, xb_s,
     qa_s, ka_s, va_s, beta_s, gcum_s, gtot_s,
     qc_s, kc_s, vc_s, qb_s, kdec_s, btot_s, qt_s, kt_s,
     t_s, p_s, rhs_s, qe_s, qkd_s, kdst_s, gtst_s, oprime_s, qprime_s, c_s, kw_s, av_s, qbst_s, gk_s,
     ) = scratch_refs

    rows = n_seq * tt
    n_groups = rows // GROUP_ROWS
    n_sb = GROUP_ROWS // c
    lc = _log2(c)
    n_levels = lc
    a_tail = slice(HIST - (CONV_A - 1), HIST)
    b_tail = slice(HIST - (CONV_B - 1), HIST)

    if stage == 'init':
        if has_state:
            dconv_ref, sd_ref, lconv_ref, h0_ref, sg_ref = state_refs
            abuf_s[:, a_tail, :] = dconv_ref[...]
            bbuf_s[:, b_tail, :] = lconv_ref[...]
            h_s[...] = h0_ref[...].reshape(n_seq, W_B)
            zeros = jnp.zeros((GROUP_ROWS, LANES), F32)
            for s in range(n_seq):
                for p in range(N_PAIRS):
                    sd = sd_ref[s, p]
                    sd_s[s, p] = jnp.concatenate([sd, sd], axis=-1)
                    sg = sg_ref[s, 2 * DK_C * p:2 * DK_C * (p + 1), :]
                    sg = jnp.concatenate([sg, sg], axis=-1)
                    sg_s[s, p] = jnp.concatenate([sg, zeros] if p % 2 == 0 else [zeros, sg], axis=0)
        else:
            abuf_s[:, a_tail, :] = jnp.zeros((n_seq, CONV_A - 1, CONV_A_WIDTH), F32)
            bbuf_s[:, b_tail, :] = jnp.zeros((n_seq, CONV_B - 1, W_B), F32)
            h_s[...] = jnp.zeros((n_seq, W_B), F32)
            sd_s[...] = jnp.zeros(sd_s.shape, F32)
            sg_s[...] = jnp.zeros(sg_s.shape, F32)
        return

    if stage == 'final':
        dconv_out_ref[...] = abuf_s[:, tt + HIST - (CONV_A - 1):tt + HIST, :]
        lconv_out_ref[...] = bbuf_s[:, tt + HIST - (CONV_B - 1):tt + HIST, :]
        h_out_ref[...] = h_s[...].reshape(n_seq, 1, W_B)
        for s in range(n_seq):
            for p in range(N_PAIRS):
                sd_out_ref[s, p] = sd_s[s, p][:, 0:HALF]
                r0 = GROUP_ROWS * (p % 2)
                sg_out_ref[s, 2 * DK_C * p:2 * DK_C * (p + 1), :] = sg_s[s, p][r0:r0 + GROUP_ROWS, 0:HALF]
        return

    def vec(row, width):
        return vec_ref[row:row + 1, 0:width]

    x = x_ref[...].reshape(rows, D_MODEL)
    xb_s[...] = x.astype(BF16)
    proj = _bdot(xb_s[...], w_in_ref[...])
    m_cum = masks_ref[0]
    m_tot = masks_ref[1]
    mask_rows = masks_ref.shape[1]

    def masked_sum(mask, a, n):
        out = [_mm_mask_lhs(mask, a[r:r + mask_rows], n) for r in range(0, rows, mask_rows)]
        return out[0] if len(out) == 1 else jnp.concatenate(out, axis=0)

    abuf_s[:, HIST:, :] = proj[:, OFF_QA:OFF_QA + CONV_A_WIDTH].reshape(n_seq, tt, CONV_A_WIDTH)
    conv_a = _conv_from_buffer(abuf_s, vec_ref, ROW_CONV_A, CONV_A, tt, slice(0, CONV_A_WIDTH))
    qkv = _silu(conv_a.reshape(rows, CONV_A_WIDTH))
    abuf_s[:, a_tail, :] = abuf_s[:, tt + HIST - (CONV_A - 1):tt + HIST, :]
    p_i = _iota((LANES, LANES), 0) >> _log2(DK_A)
    p_j = _iota((LANES, LANES), 1) >> _log2(DK_A)
    pair_ones = jnp.where(p_i == p_j, 1.0, 0.0).astype(BF16)
    for p in range(N_PAIRS):
        for off, dst, scale in ((0, qa_s, DK_A ** -0.5), (QK_A, ka_s, 1.0)):
            v = qkv[:, off + p * LANES:off + (p + 1) * LANES]
            ssq = _bdot((v * v).astype(BF16), pair_ones)
            dst[:, p * LANES:(p + 1) * LANES] = v * (lax.rsqrt(ssq + EPS) * scale)
    va_s[...] = qkv[:, 2 * QK_A:]

    small = proj[:, OFF_SMALL:OFF_SMALL + LANES]
    beta_s[...] = _sigmoid(small)
    g_full = -jnp.exp(vec(ROW_ALOG, LANES)) * _softplus(small + vec(ROW_DTB, LANES))
    gcum_s[...] = masked_sum(m_cum, g_full, 2)
    gtot_s[...] = masked_sum(m_tot, g_full, 2)

    def mixer_b():
        bbuf_s[:, HIST:, :] = proj[:, OFF_XB:OFF_XB + W_B].reshape(n_seq, tt, W_B)
        xc = (_conv_from_buffer(bbuf_s, vec_ref, ROW_CONV_B, CONV_B, tt, slice(0, W_B)).reshape(rows, W_B)
              + vec(ROW_CONV_B_BIAS, W_B))
        bbuf_s[:, b_tail, :] = bbuf_s[:, tt + HIST - (CONV_B - 1):tt + HIST, :]
        gate_r = _sigmoid(_mm(xc, wr_ref[...]) + vec(ROW_BR, W_B))
        gate_i = _sigmoid(_mm(xc, wi_ref[...]) + vec(ROW_BI, W_B))
        log_a = -LRU_C * gate_r * _softplus(-vec(ROW_LAM, W_B))
        a_t = jnp.exp(log_a)
        b_t = jnp.sqrt(-jnp.tanh(log_a) * (a_t * a_t + 1.0)) * (gate_i * xc)
        i_seq = _iota((rows, W_B), 0) & (tt - 1)
        h_prev = jnp.concatenate([jnp.broadcast_to(h_s[s:s + 1, :], (tt, W_B)) for s in range(n_seq)], axis=0)
        b_t = b_t + jnp.where(i_seq == 0, a_t * h_prev, 0.0)
        d = 1
        while d < tt:
            a_sh = pltpu.roll(a_t, d, axis=0)
            b_sh = pltpu.roll(b_t, d, axis=0)
            ok = i_seq >= d
            b_t = jnp.where(ok, a_t * b_sh + b_t, b_t)
            a_t = jnp.where(ok, a_t * a_sh, a_t)
            d *= 2
        h_s[...] = jnp.concatenate([b_t[(s + 1) * tt - 1:(s + 1) * tt] for s in range(n_seq)], axis=0)
        heads_s[:, W_A:W_A + W_B] = (b_t * _gelu_tanh(proj[:, OFF_GB:OFF_GB + W_B])).astype(BF16)

    qc = proj[:, OFF_QC:OFF_QC + QKC_PAD] * (DK_C ** -0.5)
    kc = proj[:, OFF_KC:OFF_KC + QKC_PAD]
    logf = -_softplus(-(_mm(small, w2_ref[...]) + vec(ROW_B2, QKC_PAD))) * (1.0 / GLA_TAU)
    b_cum = masked_sum(m_cum, logf, 2)
    b_tot = masked_sum(m_tot, logf, 2)
    qc_s[...] = qc
    kc_s[...] = kc
    vc_s[...] = proj[:, OFF_VC:OFF_VC + W_C]
    qb_s[...] = qc * jnp.exp(b_cum)
    kdec_s[...] = kc * jnp.exp(b_tot - b_cum)
    btot_s[...] = b_tot
    for lvl in range(n_levels):
        qt_s[lvl] = qc * jnp.exp(masked_sum(masks_ref[2 + 2 * lvl], logf, 1))
        kt_s[lvl] = kc * jnp.exp(masked_sum(masks_ref[3 + 2 * lvl], logf, 1))

    s_i = _iota((STACK, STACK), 0)
    s_j = _iota((STACK, STACK), 1)
    same_blk = (s_i >> lc) == (s_j >> lc)
    m_incl = same_blk & (s_j <= s_i)
    m_strict = same_blk & (s_j < s_i)
    eye = jnp.where(s_i == s_j, 1.0, 0.0)
    lane = _iota((GROUP_ROWS, LANES), 1)
    m0 = lane < HALF
    norm_a = vec(ROW_NORM_A, LANES)
    norm_c = vec(ROW_NORM_C, LANES)

    def delta_setup(gi):
        rs = pl.ds(gi * GROUP_ROWS, GROUP_ROWS)
        beta_g = beta_s[rs, :]
        gcum_g = gcum_s[rs, :]
        gtot_g = gtot_s[rs, :]
        for p in range(N_PAIRS):
            sl = pl.ds(p * LANES, LANES)
            q_p = qa_s[rs, sl]
            k_p = ka_s[rs, sl]
            v0, v1 = _dup_values(va_s[rs, sl])
            q_st = _stack_rows(n_sb, c, [jnp.where(m0, q_p, 0.0), jnp.where(m0, 0.0, q_p)])
            k_st = _stack_rows(n_sb, c, [jnp.where(m0, k_p, 0.0), jnp.where(m0, 0.0, k_p)])
            v_st = _stack_rows(n_sb, c, [v0, v1])
            beta_c = _col_bcast_stack(beta_g, LANE_BETA + 2 * p, n_sb, c)
            g_c = _col_bcast_stack(gcum_g, LANE_DECAY + 2 * p, n_sb, c)
            gt_c = _col_bcast_stack(gtot_g, LANE_DECAY + 2 * p, n_sb, c)
            dec = jnp.exp(jnp.where(m_incl, g_c - g_c.T, -1e30))
            kq = _mm_nt(jnp.concatenate([k_st, q_st], axis=0), k_st)
            kk, qk = kq[0:STACK], kq[STACK:]
            n_mat = -(beta_c * kk * jnp.where(m_strict, dec, 0.0))
            eg = jnp.exp(g_c)
            t_s[gi, p] = eye + n_mat
            p_s[gi, p] = n_mat.astype(BF16)
            rhs_s[gi, p] = jnp.concatenate([beta_c * v_st, beta_c * eg * k_st], axis=1).astype(BF16)
            qe_s[gi, p] = (eg * q_st).astype(BF16)
            qkd_s[gi, p] = (qk * dec).astype(BF16)
            kdst_s[gi, p] = k_st * jnp.exp(gt_c - g_c)
            gtst_s[gi, p] = gt_c

    def inverse_level(gi, j):
        for p in range(N_PAIRS):
            pw = p_s[gi, p]
            if j >= 2:
                t_old = t_s[gi, p]
                both = _bdot(jnp.concatenate([pw, t_old.astype(BF16)], axis=0), pw)
                t_s[gi, p] = t_old + both[STACK:]
                p_s[gi, p] = both[0:STACK].astype(BF16)
            else:
                p_s[gi, p] = _bdot(pw, pw).astype(BF16)

    def delta_solve(gi):
        for p in range(N_PAIRS):
            t_old = t_s[gi, p]
            t_fin = t_old + _bdot(t_old.astype(BF16), p_s[gi, p])
            uwk = _bdot(t_fin.astype(BF16), rhs_s[gi, p]).astype(BF16)
            k_dec = kdst_s[gi, p]
            if n_sb == 1:
                both = _bdot(jnp.concatenate([qkd_s[gi, p], k_dec.T.astype(BF16)], axis=0), uwk)
                o_qw, c_kw = both[0:STACK], [both[STACK:]]
            else:
                o_qw = _bdot(qkd_s[gi, p], uwk)
                c_kw = [_bdot(k_dec[sb * 2 * c:(sb + 1) * 2 * c].T.astype(BF16), uwk[sb * 2 * c:(sb + 1) * 2 * c])
                        for sb in range(n_sb)]
            oprime_s[gi, p] = o_qw[:, 0:LANES]
            qprime_s[gi, p] = (qe_s[gi, p].astype(F32) - o_qw[:, LANES:]).astype(BF16)
            for sb in range(n_sb):
                c_s[gi, p, sb] = c_kw[sb][:, 0:LANES]
                kw_s[gi, p, sb] = c_kw[sb][:, LANES:].astype(BF16)

    def gla_precompute(gi):
        rs = pl.ds(gi * GROUP_ROWS, GROUP_ROWS)
        for p in range(N_PAIRS):
            sl = pl.ds(p * LANES, LANES)
            slab = pl.ds((p // 2) * LANES, LANES)
            l0 = 2 * DK_C * (p % 2)
            mh0 = (lane >= l0) & (lane < l0 + DK_C)
            mh1 = (lane >= l0 + DK_C) & (lane < l0 + 2 * DK_C)

            def stack_c(ref, idx=None):
                val = ref[rs, slab] if idx is None else ref[idx, rs, slab]
                return _stack_rows(n_sb, c, [jnp.where(mh0, val, 0.0), jnp.where(mh1, val, 0.0)])

            def stack_k(ref, idx=None):
                val = ref[rs, slab] if idx is None else ref[idx, rs, slab]
                return _stack_rows(n_sb, c, [val, val])

            att = jnp.where(s_i == s_j, _mm_nt(stack_c(qc_s), stack_k(kc_s)), 0.0)
            for lvl in range(n_levels):
                lh = lc - 1 - lvl
                half = 1 << lh
                q_l = stack_c(qt_s, lvl)
                k_l = stack_k(kt_s, lvl)
                if half >= SUBLANES:
                    q_up = jnp.concatenate([q_l[b0 + half:b0 + 2 * half] for b0 in range(0, STACK, 2 * half)], axis=0)
                    prod = _mm_nt(q_up, k_l)
                    u_i = _iota(prod.shape, 0)
                    u_j = _iota(prod.shape, 1)
                    ok = ((u_i >> lh) == (u_j >> (lh + 1))) & (((u_j >> lh) & 1) == 0)
                    prod = jnp.where(ok, prod, 0.0)
                    zeros = jnp.zeros((half, STACK), F32)
                    pieces = []
                    for n_blk in range(STACK // (2 * half)):
                        pieces += [zeros, prod[n_blk * half:(n_blk + 1) * half]]
                    att = att + jnp.concatenate(pieces, axis=0)
                else:
                    valid = (((s_i >> (lh + 1)) == (s_j >> (lh + 1)))
                             & (((s_i >> lh) & 1) == 1) & (((s_j >> lh) & 1) == 0))
                    att = att + jnp.where(valid, _mm_nt(q_l, k_l), 0.0)
            vc0, vc1 = _dup_values(vc_s[rs, sl])
            vc_st = _stack_rows(n_sb, c, [vc0, vc1]).astype(BF16)
            qbst_s[gi, p] = stack_c(qb_s).astype(BF16)
            kd_st = stack_c(kdec_s)
            if n_sb == 1:
                both = _bdot(jnp.concatenate([att.astype(BF16), kd_st.T.astype(BF16)], axis=0), vc_st)
                av_s[gi, p] = both[0:STACK]
                gk_s[gi, p, 0] = both[STACK:]
            else:
                av_s[gi, p] = _bdot(att.astype(BF16), vc_st)
                for sb in range(n_sb):
                    blk = slice(sb * 2 * c, (sb + 1) * 2 * c)
                    gk_s[gi, p, sb] = _bdot(kd_st[blk].T.astype(BF16), vc_st[blk])

    for gi in range(n_groups):
        delta_setup(gi)
    for j in range(1, lc):
        for gi in range(n_groups):
            inverse_level(gi, j)
        if j <= n_groups:
            gla_precompute(j - 1)
        if j == 1:
            mixer_b()
    for gi in range(lc - 1, n_groups):
        gla_precompute(gi)
    for gi in range(n_groups):
        delta_solve(gi)

    def recurrence_body(gi):
        rs = pl.ds(gi * GROUP_ROWS, GROUP_ROWS)
        for p in range(N_PAIRS):
            sl = pl.ds(p * LANES, LANES)
            q_prime = qprime_s[gi, p]
            gt_c = gtst_s[gi, p]
            qs_parts = []
            for sb in range(n_sb):
                seq = (gi * GROUP_ROWS + sb * c) // tt
                blk = slice(sb * 2 * c, (sb + 1) * 2 * c)
                s_old = sd_s[seq, p]
                s_bf = s_old.astype(BF16)
                if n_sb == 1:
                    both = _bdot(jnp.concatenate([q_prime, kw_s[gi, p, 0]], axis=0), s_bf)
                    qs, kws = both[0:STACK], both[STACK:]
                else:
                    qs, kws = _bdot(q_prime[blk], s_bf), _bdot(kw_s[gi, p, sb], s_bf)
                qs_parts.append(qs)
                decay_rows = jnp.concatenate(
                    [jnp.broadcast_to(gt_c[sb * 2 * c + a * c:sb * 2 * c + a * c + 1, :], (GROUP_ROWS, LANES))
                     for a in range(2)], axis=0)
                sd_s[seq, p] = jnp.exp(decay_rows) * s_old + (c_s[gi, p, sb] - kws)
            qs_all = qs_parts[0] if n_sb == 1 else jnp.concatenate(qs_parts, axis=0)
            o_st = qs_all + oprime_s[gi, p]
            o_st = o_st * lax.rsqrt(jnp.mean(o_st * o_st, axis=-1, keepdims=True) + EPS) * norm_a
            z_a = proj[gi * GROUP_ROWS:(gi + 1) * GROUP_ROWS, OFF_ZA + p * LANES:OFF_ZA + (p + 1) * LANES]
            heads_s[rs, sl] = (_unstack_rows(o_st, n_sb, c) * _silu(z_a)).astype(BF16)
            slab = pl.ds((p // 2) * LANES, LANES)
            qb_st = qbst_s[gi, p]
            btot_t = btot_s[rs, slab].T
            oi_parts = []
            for sb in range(n_sb):
                seq = (gi * GROUP_ROWS + sb * c) // tt
                blk = slice(sb * 2 * c, (sb + 1) * 2 * c)
                s_old = sg_s[seq, p]
                oi_parts.append(_bdot(qb_st[blk], s_old.astype(BF16)))
                decay_col = jnp.broadcast_to(btot_t[:, sb * c:sb * c + 1], (LANES, LANES))
                sg_s[seq, p] = jnp.exp(decay_col) * s_old + gk_s[gi, p, sb]
            oi_all = oi_parts[0] if n_sb == 1 else jnp.concatenate(oi_parts, axis=0)
            oc_st = oi_all + av_s[gi, p]
            oc_st = oc_st * lax.rsqrt(jnp.mean(oc_st * oc_st, axis=-1, keepdims=True) + EPS) * norm_c
            z_c = proj[gi * GROUP_ROWS:(gi + 1) * GROUP_ROWS, OFF_ZC + p * LANES:OFF_ZC + (p + 1) * LANES]
            heads_s[rs, pl.ds(W_A + W_B + p * LANES, LANES)] = (
                _unstack_rows(oc_st, n_sb, c) * _silu(z_c)).astype(BF16)

    groups_per_seq = max(tt // GROUP_ROWS, 1)
    for k in range(groups_per_seq):
        for gi in range(k, n_groups, groups_per_seq):
            recurrence_body(gi)

    y = ALPHA * x + _bdot(heads_s[...], w_out_ref[...])
    write_x1(_layer_norm(y, vec(ROW_LN1_G, D_MODEL), vec(ROW_LN1_B, D_MODEL)))


def _mixer_scratch(n_seq, tt, c):
    rows = n_seq * tt
    n_levels = _log2(c)
    n_groups = rows // GROUP_ROWS
    scratch = [
        pltpu.VMEM((n_seq, N_PAIRS, STACK, LANES), F32),
        pltpu.VMEM((n_seq, N_PAIRS, LANES, LANES), F32),
        pltpu.VMEM((n_seq, HIST + tt, CONV_A_WIDTH), F32),
        pltpu.VMEM((n_seq, HIST + tt, W_B), F32),
        pltpu.VMEM((n_seq, W_B), F32),
        pltpu.VMEM((rows, D_MIX), BF16),
        pltpu.VMEM((rows, D_MODEL), BF16),
        pltpu.VMEM((rows, QK_A), F32),
        pltpu.VMEM((rows, QK_A), F32),
        pltpu.VMEM((rows, W_A), F32),
        pltpu.VMEM((rows, LANES), F32),
        pltpu.VMEM((rows, LANES), F32),
        pltpu.VMEM((rows, LANES), F32),
        pltpu.VMEM((rows, QKC_PAD), F32),
        pltpu.VMEM((rows, QKC_PAD), F32),
        pltpu.VMEM((rows, W_C), F32),
        pltpu.VMEM((rows, QKC_PAD), F32),
        pltpu.VMEM((rows, QKC_PAD), F32),
        pltpu.VMEM((rows, QKC_PAD), F32),
        pltpu.VMEM((n_levels, rows, QKC_PAD), F32),
        pltpu.VMEM((n_levels, rows, QKC_PAD), F32),
        pltpu.VMEM((n_groups, N_PAIRS, STACK, STACK), F32),
        pltpu.VMEM((n_groups, N_PAIRS, STACK, STACK), BF16),
        pltpu.VMEM((n_groups, N_PAIRS, STACK, 2 * LANES), BF16),
        pltpu.VMEM((n_groups, N_PAIRS, STACK, LANES), BF16),
        pltpu.VMEM((n_groups, N_PAIRS, STACK, STACK), BF16),
        pltpu.VMEM((n_groups, N_PAIRS, STACK, LANES), F32),
        pltpu.VMEM((n_groups, N_PAIRS, STACK, LANES), F32),
        pltpu.VMEM((n_groups, N_PAIRS, STACK, LANES), F32),
        pltpu.VMEM((n_groups, N_PAIRS, STACK, LANES), BF16),
        pltpu.VMEM((n_groups, N_PAIRS, GROUP_ROWS // c, LANES, LANES), F32),
        pltpu.VMEM((n_groups, N_PAIRS, GROUP_ROWS // c, LANES, LANES), BF16),
        pltpu.VMEM((n_groups, N_PAIRS, STACK, LANES), F32),
        pltpu.VMEM((n_groups, N_PAIRS, STACK, LANES), BF16),
        pltpu.VMEM((n_groups, N_PAIRS, GROUP_ROWS // c, LANES, LANES), F32),
    ]
    assert len(scratch) == N_MIXER_SCRATCH
    return scratch


def _ffn_stage(stage, x, state_ref, weight_refs, y_ref, tail_ref, scratch_refs, *, n_seq, tt):
    (w_up_ref, vec_ref, wd_ref) = weight_refs
    (gbuf_s, h_s, xb_s) = scratch_refs
    rows = n_seq * tt
    tail = slice(HIST - (CONV_F - 1), HIST)

    if stage == 'init':
        if state_ref is not None:
            gbuf_s[:, tail, :] = state_ref[...]
        else:
            gbuf_s[:, tail, :] = jnp.zeros((n_seq, CONV_F - 1, D_FF), F32)
        return
    if stage == 'final':
        tail_ref[...] = gbuf_s[:, tt + HIST - (CONV_F - 1):tt + HIST, :]
        return

    xb_s[...] = x.astype(BF16)
    for j in range(N_FF_CHUNKS):
        cols = slice(j * FF_CHUNK, (j + 1) * FF_CHUNK)
        gate = _bdot(xb_s[...], w_up_ref[:, cols])
        val = _bdot(xb_s[...], w_up_ref[:, D_FF + j * FF_CHUNK:D_FF + (j + 1) * FF_CHUNK])
        gbuf_s[:, HIST:, cols] = gate.reshape(n_seq, tt, FF_CHUNK)
        conv = _conv_from_buffer(gbuf_s, vec_ref, ROW_CONV_F, CONV_F, tt, cols).reshape(rows, FF_CHUNK)
        h = _gelu_tanh(conv + vec_ref[ROW_CONV_F_BIAS:ROW_CONV_F_BIAS + 1, cols]) * val
        h_s[:, cols] = h.astype(BF16)
    gbuf_s[:, tail, :] = gbuf_s[:, tt + HIST - (CONV_F - 1):tt + HIST, :]
    y = ALPHA * x + _bdot(h_s[...], wd_ref[...])
    ln_g = vec_ref[ROW_LN2_G:ROW_LN2_G + 1, 0:D_MODEL]
    ln_b = vec_ref[ROW_LN2_B:ROW_LN2_B + 1, 0:D_MODEL]
    y_ref[...] = _layer_norm(y, ln_g, ln_b).reshape(n_seq, tt, D_MODEL)


def _mixer_kernel(*refs, n_seq, tt, c, has_state, n_alias):
    n_state = N_MIXER_OUTS if has_state else 0
    k = 1 + n_state
    x_ref, state_refs = refs[0], refs[1:k]
    weight_refs = refs[k:k + N_MIXER_WEIGHTS]
    k += N_MIXER_WEIGHTS + n_alias
    x1_ref = refs[k]
    out_refs = refs[k + 1:k + 1 + N_MIXER_OUTS]
    scratch_refs = refs[k + 1 + N_MIXER_OUTS:]
    t_idx = pl.program_id(1)

    def write_x1(val):
        x1_ref[...] = val.reshape(n_seq, tt, D_MODEL)

    stage = functools.partial(_mixer_stage, x_ref=x_ref, state_refs=state_refs, weight_refs=weight_refs,
                              out_refs=out_refs, scratch_refs=scratch_refs, write_x1=write_x1,
                              n_seq=n_seq, tt=tt, c=c)
    pl.when(t_idx == 0)(lambda: stage('init'))
    stage('body')
    pl.when(t_idx == pl.num_programs(1) - 1)(lambda: stage('final'))


def _ffn_kernel(*refs, n_seq, tt, has_state, n_alias):
    k = 2 if has_state else 1
    x_ref = refs[0]
    state_ref = refs[1] if has_state else None
    weight_refs = refs[k:k + 3]
    k += 3 + n_alias
    y_ref, tail_ref = refs[k:k + 2]
    scratch_refs = refs[k + 2:]
    t_idx = pl.program_id(1)
    stage = functools.partial(_ffn_stage, state_ref=state_ref, weight_refs=weight_refs, y_ref=y_ref,
                              tail_ref=tail_ref, scratch_refs=scratch_refs, n_seq=n_seq, tt=tt)
    pl.when(t_idx == 0)(lambda: stage('init', None))
    stage('body', x_ref[...].reshape(n_seq * tt, D_MODEL))
    pl.when(t_idx == pl.num_programs(1) - 1)(lambda: stage('final', None))


def _block_call(kern, name, x, states, state_shapes, weights, scratch, layer, prev_outs, *, n_seq, tt):
    nb, t_total, _ = x.shape
    assert nb % n_seq == 0 and t_total % tt == 0
    x_spec = pl.BlockSpec((n_seq, tt, D_MODEL), lambda b, t: (b, t, 0))
    in_specs, operands = [x_spec], [x]
    if states is not None:
        for arr, tail in zip(states, state_shapes):
            assert arr.shape[2:] == tail
            in_specs.append(pl.BlockSpec((None, n_seq) + tail, lambda b, t, z=(0,) * len(tail): (layer, b) + z))
            operands.append(arr)
    for a, per_layer in weights:
        if per_layer:
            in_specs.append(pl.BlockSpec((None,) + a.shape[1:], lambda b, t, z=(0,) * (a.ndim - 1): (layer,) + z))
        else:
            in_specs.append(pl.BlockSpec(a.shape, lambda b, t, z=(0,) * a.ndim: z))
        operands.append(a)
    aliases = {}
    if prev_outs is not None:
        for i, arr in enumerate(prev_outs):
            aliases[len(operands)] = 1 + i
            in_specs.append(pl.BlockSpec(memory_space=pl.ANY))
            operands.append(arr)
    out_shape = [jax.ShapeDtypeStruct((nb, t_total, D_MODEL), F32)]
    out_specs = [x_spec]
    for tail in state_shapes:
        out_shape.append(jax.ShapeDtypeStruct((DEPTH, nb) + tail, F32))
        out_specs.append(pl.BlockSpec((None, n_seq) + tail, lambda b, t, z=(0,) * len(tail): (layer, b) + z))
    return pl.pallas_call(
        functools.partial(kern, n_alias=len(aliases)),
        grid=(nb // n_seq, t_total // tt),
        in_specs=in_specs,
        out_specs=out_specs,
        out_shape=out_shape,
        input_output_aliases=aliases,
        scratch_shapes=scratch,
        compiler_params=pltpu.CompilerParams(
            dimension_semantics=("arbitrary", "arbitrary"), vmem_limit_bytes=VMEM_LIMIT_BYTES),
        name=f"{name}_nseq{n_seq}_tt{tt}",
    )(*operands)


def _mixer_call(x, states, w, layer, prev_outs, *, n_seq, tt, c):
    assert (tt % MASK_ROWS == 0 and c == GROUP_ROWS) or (tt == HIST and c == HIST and n_seq * tt <= MASK_ROWS)
    assert (n_seq * tt) % GROUP_ROWS == 0
    state_shapes = ((CONV_A - 1, CONV_A_WIDTH), (N_PAIRS, 2 * DK_A, DV_A), (CONV_B - 1, W_B), (1, W_B),
                    (QK_C, DV_C))
    weights = [(_tile_masks(min(n_seq * tt, MASK_ROWS), c), False)] + [
        (w[k], True) for k in ('w_in', 'mix_vec', 'wr', 'wi', 'w2', 'w_out')]
    kern = functools.partial(_mixer_kernel, n_seq=n_seq, tt=tt, c=c, has_state=states is not None)
    return _block_call(kern, 'mixer', x, states, state_shapes, weights, _mixer_scratch(n_seq, tt, c), layer,
                       prev_outs, n_seq=n_seq, tt=tt)


def _ffn_call(x, state, w, layer, prev_out, *, n_seq, tt):
    weights = [(w[k], True) for k in ('w_up', 'ffn_vec', 'w_down')]
    scratch = [pltpu.VMEM((n_seq, HIST + tt, D_FF), F32),
               pltpu.VMEM((n_seq * tt, D_FF), BF16),
               pltpu.VMEM((n_seq * tt, D_MODEL), BF16)]
    kern = functools.partial(_ffn_kernel, n_seq=n_seq, tt=tt, has_state=state is not None)
    return _block_call(kern, 'ffn', x, None if state is None else (state,), ((CONV_F - 1, D_FF),), weights,
                       scratch, layer, None if prev_out is None else (prev_out,), n_seq=n_seq, tt=tt)


def _vec_table(pieces, width, n_rows):
    rows = []
    used = 0
    for arr, lane0 in pieces:
        arr = arr.astype(F32)
        rows.append(jnp.pad(arr, ((0, 0), (0, 0), (lane0, width - lane0 - arr.shape[2]))))
        used += arr.shape[1]
    depth = pieces[0][0].shape[0]
    rows.append(jnp.zeros((depth, n_rows - used, width), F32))
    return jnp.concatenate(rows, axis=1)


def _prep_weights(w_in, conv_a_w, a_log, dt_bias, norm_a_w, conv_b_w, conv_b_b, lru_w_r, lru_b_r, lru_w_i,
                  lru_b_i, lru_lambda, gla_w2, gla_b2, norm_c_w, w_out, ln1_g, ln1_b, ffn_w_up, ffn_conv_w,
                  ffn_conv_b, ffn_w_down, ln2_g, ln2_b):
    depth, d, _ = w_in.shape
    pts = [0]
    for s in (QK_A, QK_A, W_A, W_A, H_A, H_A, W_B, W_B, QK_C, QK_C, W_C, W_C, GLA_RANK):
        pts.append(pts[-1] + s)
    (qa, ka, va, za, ba, aa, xb, gb, qc, kc, vc, zc, lc) = [w_in[:, :, pts[i]:pts[i + 1]] for i in range(13)]
    z = lambda n: jnp.zeros((depth, d, n), w_in.dtype)
    small = jnp.concatenate([ba, z(LANE_DECAY - H_A), aa, z(LANE_LOWRANK - LANE_DECAY - H_A), lc,
                             z(LANES - LANE_LOWRANK - GLA_RANK)], axis=2)
    w_in_p = jnp.concatenate([qa, ka, va, za, xb, gb, qc, z(QKC_PAD - QK_C), kc, z(QKC_PAD - QK_C), vc, zc, small],
                             axis=2).astype(BF16)
    assert w_in_p.shape[2] == D_IN_PAD

    eye_blocks = jnp.eye(LRU_BLOCKS, dtype=F32)[None, :, None, :, None]

    def block_diag(wb):
        return (wb[:, :, :, None, :] * eye_blocks).reshape(depth, W_B, W_B).astype(BF16)

    w2 = jnp.pad(gla_w2, ((0, 0), (LANE_LOWRANK, LANES - LANE_LOWRANK - GLA_RANK), (0, QKC_PAD - QK_C))).astype(BF16)
    r1 = lambda v: v[:, None, :]
    dup = lambda v: jnp.concatenate([v, v], axis=1)[:, None, :]
    mix_vec = _vec_table(
        [(conv_a_w, 0), (conv_b_w, 0), (r1(conv_b_b), 0), (r1(a_log), LANE_DECAY), (r1(dt_bias), LANE_DECAY),
         (dup(norm_a_w), 0), (r1(lru_b_r), 0), (r1(lru_b_i), 0), (r1(lru_lambda), 0), (r1(gla_b2), 0),
         (dup(norm_c_w), 0), (r1(ln1_g), 0), (r1(ln1_b), 0)], VEC_W, MIX_VEC_ROWS)
    ffn_vec = _vec_table([(ffn_conv_w, 0), (r1(ffn_conv_b), 0), (r1(ln2_g), 0), (r1(ln2_b), 0)], D_FF, FFN_VEC_ROWS)
    return {
        'w_in': w_in_p, 'mix_vec': mix_vec, 'wr': block_diag(lru_w_r), 'wi': block_diag(lru_w_i), 'w2': w2,
        'w_out': w_out.astype(BF16), 'w_up': ffn_w_up.astype(BF16), 'ffn_vec': ffn_vec,
        'w_down': ffn_w_down.astype(BF16),
    }


def _trunk(x, states, w, *, n_seq, tt, c):
    nb = x.shape[0]
    if states is not None:
        st_dconv, st_delta, st_lconv, st_lru, st_gla, st_fconv = (s.astype(F32) for s in states)
        states = (st_dconv, st_delta.reshape(DEPTH, nb, N_PAIRS, 2 * DK_A, DV_A), st_lconv,
                  st_lru.reshape(DEPTH, nb, 1, W_B), st_gla.reshape(DEPTH, nb, QK_C, DV_C), st_fconv)
    mix_outs = ffn_out = None
    for l in range(DEPTH):
        x1, *mix_outs = _mixer_call(x, None if states is None else states[:5], w, l, mix_outs,
                                    n_seq=n_seq, tt=tt, c=c)
        if n_seq > 1:
            ffn_tile = dict(n_seq=math.gcd(nb, max(n_seq, FFN_TILE_ROWS // tt)), tt=tt)
        else:
            ffn_tile = dict(n_seq=1, tt=min(FFN_TILE_ROWS, x.shape[1]))
        x, ffn_out = _ffn_call(x1, None if states is None else states[5], w, l, ffn_out, **ffn_tile)
    dconv, sd, lconv, h, sg = mix_outs
    return x, (dconv, sd.reshape(DEPTH, nb, H_A, DK_A, DV_A), lconv, h.reshape(DEPTH, nb, W_B),
               sg.reshape(DEPTH, nb, H_C, DK_C, DV_C), ffn_out)


def kernel(x_prompt, x_sample, state_delta_conv, state_delta, state_lru_conv, state_lru, state_gla, state_ffn_conv,
           w_in, conv_a_w, a_log, dt_bias, norm_a_w, conv_b_w, conv_b_b, lru_w_r, lru_b_r, lru_w_i, lru_b_i,
           lru_lambda, gla_w2, gla_b2, norm_c_w, w_out, ln1_g, ln1_b, ffn_w_up, ffn_conv_w, ffn_conv_b, ffn_w_down,
           ln2_g, ln2_b):
    assert w_in.shape[0] == DEPTH
    w = _prep_weights(w_in, conv_a_w, a_log, dt_bias, norm_a_w, conv_b_w, conv_b_b, lru_w_r, lru_b_r, lru_w_i,
                      lru_b_i, lru_lambda, gla_w2, gla_b2, norm_c_w, w_out, ln1_g, ln1_b, ffn_w_up, ffn_conv_w,
                      ffn_conv_b, ffn_w_down, ln2_g, ln2_b)
    sample_states = (state_delta_conv, state_delta, state_lru_conv, state_lru, state_gla, state_ffn_conv)
    t_p = x_prompt.shape[1]
    t_s = x_sample.shape[1]
    n_p = PROMPT_NSEQ if x_prompt.shape[0] % PROMPT_NSEQ == 0 else 1
    y_p, p_st = _trunk(x_prompt, None, w, n_seq=n_p, tt=min(PROMPT_TT, t_p), c=CHUNK)
    y_s, s_st = _trunk(x_sample, sample_states, w, n_seq=min(SAMPLE_NSEQ, x_sample.shape[0]), tt=t_s, c=t_s)
    return (y_p, y_s) + p_st + s_st
```

```python
import functools
import math

import numpy as np
import jax
import jax.numpy as jnp
from jax import lax
from jax.experimental import pallas as pl
from jax.experimental.pallas import tpu as pltpu

F32 = jnp.float32
BF16 = jnp.bfloat16

D_MODEL = 1024
H_A, DK_A, DV_A = 6, 64, 64
QK_A = H_A * DK_A
W_A = H_A * DV_A
CONV_A = 4
W_B = 256
LRU_BLOCKS = 4
LRU_BLOCK = W_B // LRU_BLOCKS
CONV_B = 4
LRU_C = 8.0
H_C, DK_C, DV_C = 6, 32, 64
QK_C = H_C * DK_C
W_C = H_C * DV_C
GLA_RANK = 16
GLA_TAU = 16.0
D_MIX = W_A + W_B + W_C
CHUNK = 64
D_FF = 2816
CONV_F = 3
EPS = 1e-6
CONV_A_WIDTH = 2 * QK_A + W_A
DEPTH = 2
ALPHA = (2.0 * DEPTH) ** 0.25

LANES = 128
SUBLANES = 8
HALF = LANES // 2
GROUP_ROWS = 64
STACK = 2 * GROUP_ROWS
N_PAIRS = 3
HIST = SUBLANES
FF_CHUNK = 256
N_FF_CHUNKS = D_FF // FF_CHUNK
VMEM_LIMIT_BYTES = 56 * 1024 * 1024
PROMPT_TT = 512
PROMPT_NSEQ = 1
MASK_ROWS = 256
FFN_TILE_ROWS = 512
SAMPLE_NSEQ = 16

OFF_QA = 0
OFF_KA = OFF_QA + QK_A
OFF_VA = OFF_KA + QK_A
OFF_ZA = OFF_VA + W_A
OFF_XB = OFF_ZA + W_A
OFF_GB = OFF_XB + W_B
OFF_QC = OFF_GB + W_B
QKC_PAD = 2 * LANES
OFF_KC = OFF_QC + QKC_PAD
OFF_VC = OFF_KC + QKC_PAD
OFF_ZC = OFF_VC + W_C
OFF_SMALL = OFF_ZC + W_C
D_IN_PAD = OFF_SMALL + LANES
LANE_BETA = 0
LANE_DECAY = 8
LANE_LOWRANK = 16

VEC_W = CONV_A_WIDTH
(ROW_CONV_A, ROW_CONV_B, ROW_CONV_B_BIAS, ROW_ALOG, ROW_DTB, ROW_NORM_A, ROW_BR, ROW_BI, ROW_LAM, ROW_B2,
 ROW_NORM_C, ROW_LN1_G, ROW_LN1_B) = (0, 4, 8, 9, 10, 11, 12, 13, 14, 15, 16, 17, 18)
MIX_VEC_ROWS = 24
(ROW_CONV_F, ROW_CONV_F_BIAS, ROW_LN2_G, ROW_LN2_B) = (0, 3, 4, 5)
FFN_VEC_ROWS = 8


def _bdot(a, b):
    return jnp.dot(a, b, preferred_element_type=F32)


def _mm(a, b):
    return _bdot(a.astype(BF16), b.astype(BF16))


def _mm_nt(a, b):
    return lax.dot_general(a.astype(BF16), b.astype(BF16), (((1,), (1,)), ((), ())), preferred_element_type=F32)


def _split(a, n):
    parts = []
    r = a
    for i in range(n):
        p = r.astype(BF16)
        parts.append(p)
        if i + 1 < n:
            r = r - p.astype(F32)
    return parts


def _mm_mask_lhs(mask_bf16, a, n):
    out = None
    for p in _split(a, n):
        t = _bdot(mask_bf16, p)
        out = t if out is None else out + t
    return out


def _softplus(x):
    return jnp.maximum(x, 0.0) + jnp.log1p(jnp.exp(-jnp.abs(x)))


def _sigmoid(x):
    return 0.5 * jnp.tanh(0.5 * x) + 0.5


def _silu(x):
    h = 0.5 * x
    return h * jnp.tanh(h) + h


def _gelu_tanh(x):
    return 0.5 * x * (1.0 + jnp.tanh(math.sqrt(2.0 / math.pi) * (x + 0.044715 * (x * x * x))))


def _layer_norm(x, g, b):
    mu = jnp.mean(x, axis=-1, keepdims=True)
    xc = x - mu
    var = jnp.mean(xc * xc, axis=-1, keepdims=True)
    return xc * lax.rsqrt(var + EPS) * g + b


def _iota(shape, axis):
    return lax.broadcasted_iota(jnp.int32, shape, axis)


def _log2(n):
    l = int(math.log2(n))
    assert (1 << l) == n
    return l


def _conv_from_buffer(buf_ref, vec_ref, row0, width, tt, cols):
    y = None
    for d in range(width):
        tap = vec_ref[row0 + width - 1 - d:row0 + width - d, cols]
        term = buf_ref[:, HIST - d:HIST - d + tt, cols] * tap
        y = term if y is None else y + term
    return y


def _stack_rows(n_sb, c, pieces):
    if n_sb == 1:
        return jnp.concatenate([pieces[0], pieces[1]], axis=0)
    out = []
    for s in range(n_sb):
        for a in range(2):
            out.append(pieces[a][s * c:(s + 1) * c])
    return jnp.concatenate(out, axis=0)


def _unstack_rows(o, n_sb, c):
    lo = _iota((c, LANES), 1) < HALF
    out = []
    for s in range(n_sb):
        base = s * 2 * c
        out.append(jnp.where(lo, o[base:base + c], o[base + c:base + 2 * c]))
    return out[0] if n_sb == 1 else jnp.concatenate(out, axis=0)


def _col_bcast_stack(vals, lane0, n_sb, c):
    pieces = [jnp.broadcast_to(vals[:, lane0 + a:lane0 + a + 1], (GROUP_ROWS, LANES)) for a in range(2)]
    return _stack_rows(n_sb, c, pieces)


def _dup_values(v):
    lo = _iota(v.shape, 1) < HALF
    vr = pltpu.roll(v, HALF, axis=1)
    return jnp.where(lo, v, vr), jnp.where(lo, vr, v)


def _tile_masks(rows, c):
    lc = _log2(c)
    i = np.arange(rows)[:, None]
    j = np.arange(rows)[None, :]
    same = (i >> lc) == (j >> lc)
    out = [same & (j <= i), same]
    for lvl in range(lc):
        lh = lc - 1 - lvl
        hi = (i >> lh) & 1
        start = (i >> lh) << lh
        nxt = ((i >> lh) + 1) << lh
        out.append((hi == 1) & (j >= start) & (j <= i))
        out.append((hi == 0) & (j > i) & (j < nxt))
    return jnp.asarray(np.stack(out).astype(np.float32), dtype=BF16)


MIXER_STATE_SHAPES = ((CONV_A - 1, CONV_A_WIDTH), (N_PAIRS, 2 * DK_A, DV_A), (CONV_B - 1, W_B), (1, W_B),
                      (QK_C, DV_C))
FFN_STATE_SHAPE = (CONV_F - 1, D_FF)
N_MIXER_WEIGHTS = 7
N_MIXER_OUTS = 5
N_MIXER_SCRATCH = 35


def _mixer_stage(stage, x_ref, state_refs, weight_refs, out_refs, scratch_refs, write_x1, *, n_seq, tt, c):
    has_state = len(state_refs) > 0
    (masks_ref, w_in_ref, vec_ref, wr_ref, wi_ref, w2_ref, w_out_ref) = weight_refs
    (dconv_out_ref, sd_out_ref, lconv_out_ref, h_out_ref, sg_out_ref) = out_refs
    (sd_s, sg_s, abuf_s, bbuf_s, h_s, heads_s, xb_s,
     qa_s, ka_s, va_s, beta_s, gcum_s, gtot_s,
     qc_s, kc_s, vc_s, qb_s, kdec_s, btot_s, qt_s, kt_s,
     t_s, p_s, rhs_s, qe_s, qkd_s, kdst_s, gtst_s, oprime_s, qprime_s, c_s, kw_s, av_s, qbst_s, gk_s,
     ) = scratch_refs

    rows = n_seq * tt
    n_groups = rows // GROUP_ROWS
    n_sb = GROUP_ROWS // c
    lc = _log2(c)
    n_levels = lc
    a_tail = slice(HIST - (CONV_A - 1), HIST)
    b_tail = slice(HIST - (CONV_B - 1), HIST)

    if stage == 'init':
        if has_state:
            dconv_ref, sd_ref, lconv_ref, h0_ref, sg_ref = state_refs
            abuf_s[:, a_tail, :] = dconv_ref[...]
            bbuf_s[:, b_tail, :] = lconv_ref[...]
            h_s[...] = h0_ref[...].reshape(n_seq, W_B)
            zeros = jnp.zeros((GROUP_ROWS, LANES), F32)
            for s in range(n_seq):
                for p in range(N_PAIRS):
                    sd = sd_ref[s, p]
                    sd_s[s, p] = jnp.concatenate([sd, sd], axis=-1)
                    sg = sg_ref[s, 2 * DK_C * p:2 * DK_C * (p + 1), :]
                    sg = jnp.concatenate([sg, sg], axis=-1)
                    sg_s[s, p] = jnp.concatenate([sg, zeros] if p % 2 == 0 else [zeros, sg], axis=0)
        else:
            abuf_s[:, a_tail, :] = jnp.zeros((n_seq, CONV_A - 1, CONV_A_WIDTH), F32)
            bbuf_s[:, b_tail, :] = jnp.zeros((n_seq, CONV_B - 1, W_B), F32)
            h_s[...] = jnp.zeros((n_seq, W_B), F32)
            sd_s[...] = jnp.zeros(sd_s.shape, F32)
            sg_s[...] = jnp.zeros(sg_s.shape, F32)
        return

    if stage == 'final':
        dconv_out_ref[...] = abuf_s[:, tt + HIST - (CONV_A - 1):tt + HIST, :]
        lconv_out_ref[...] = bbuf_s[:, tt + HIST - (CONV_B - 1):tt + HIST, :]
        h_out_ref[...] = h_s[...].reshape(n_seq, 1, W_B)
        for s in range(n_seq):
            for p in range(N_PAIRS):
                sd_out_ref[s, p] = sd_s[s, p][:, 0:HALF]
                r0 = GROUP_ROWS * (p % 2)
                sg_out_ref[s, 2 * DK_C * p:2 * DK_C * (p + 1), :] = sg_s[s, p][r0:r0 + GROUP_ROWS, 0:HALF]
        return

    def vec(row, width):
        return vec_ref[row:row + 1, 0:width]

    x = x_ref[...].reshape(rows, D_MODEL)
    xb_s[...] = x.astype(BF16)
    proj = _bdot(xb_s[...], w_in_ref[...])
    m_cum = masks_ref[0]
    m_tot = masks_ref[1]
    mask_rows = masks_ref.shape[1]

    def masked_sum(mask, a, n):
        out = [_mm_mask_lhs(mask, a[r:r + mask_rows], n) for r in range(0, rows, mask_rows)]
        return out[0] if len(out) == 1 else jnp.concatenate(out, axis=0)

    abuf_s[:, HIST:, :] = proj[:, OFF_QA:OFF_QA + CONV_A_WIDTH].reshape(n_seq, tt, CONV_A_WIDTH)
    conv_a = _conv_from_buffer(abuf_s, vec_ref, ROW_CONV_A, CONV_A, tt, slice(0, CONV_A_WIDTH))
    qkv = _silu(conv_a.reshape(rows, CONV_A_WIDTH))
    abuf_s[:, a_tail, :] = abuf_s[:, tt + HIST - (CONV_A - 1):tt + HIST, :]
    p_i = _iota((LANES, LANES), 0) >> _log2(DK_A)
    p_j = _iota((LANES, LANES), 1) >> _log2(DK_A)
    pair_ones = jnp.where(p_i == p_j, 1.0, 0.0).astype(BF16)
    for p in range(N_PAIRS):
        for off, dst, scale in ((0, qa_s, DK_A ** -0.5), (QK_A, ka_s, 1.0)):
            v = qkv[:, off + p * LANES:off + (p + 1) * LANES]
            ssq = _bdot((v * v).astype(BF16), pair_ones)
            dst[:, p * LANES:(p + 1) * LANES] = v * (lax.rsqrt(ssq + EPS) * scale)
    va_s[...] = qkv[:, 2 * QK_A:]

    small = proj[:, OFF_SMALL:OFF_SMALL + LANES]
    beta_s[...] = _sigmoid(small)
    g_full = -jnp.exp(vec(ROW_ALOG, LANES)) * _softplus(small + vec(ROW_DTB, LANES))
    gcum_s[...] = masked_sum(m_cum, g_full, 2)
    gtot_s[...] = masked_sum(m_tot, g_full, 2)

    def mixer_b():
        bbuf_s[:, HIST:, :] = proj[:, OFF_XB:OFF_XB + W_B].reshape(n_seq, tt, W_B)
        xc = (_conv_from_buffer(bbuf_s, vec_ref, ROW_CONV_B, CONV_B, tt, slice(0, W_B)).reshape(rows, W_B)
              + vec(ROW_CONV_B_BIAS, W_B))
        bbuf_s[:, b_tail, :] = bbuf_s[:, tt + HIST - (CONV_B - 1):tt + HIST, :]
        gate_r = _sigmoid(_mm(xc, wr_ref[...]) + vec(ROW_BR, W_B))
        gate_i = _sigmoid(_mm(xc, wi_ref[...]) + vec(ROW_BI, W_B))
        log_a = -LRU_C * gate_r * _softplus(-vec(ROW_LAM, W_B))
        a_t = jnp.exp(log_a)
        b_t = jnp.sqrt(-jnp.tanh(log_a) * (a_t * a_t + 1.0)) * (gate_i * xc)
        i_seq = _iota((rows, W_B), 0) & (tt - 1)
        h_prev = jnp.concatenate([jnp.broadcast_to(h_s[s:s + 1, :], (tt, W_B)) for s in range(n_seq)], axis=0)
        b_t = b_t + jnp.where(i_seq == 0, a_t * h_prev, 0.0)
        d = 1
        while d < tt:
            a_sh = pltpu.roll(a_t, d, axis=0)
            b_sh = pltpu.roll(b_t, d, axis=0)
            ok = i_seq >= d
            b_t = jnp.where(ok, a_t * b_sh + b_t, b_t)
            a_t = jnp.where(ok, a_t * a_sh, a_t)
            d *= 2
        h_s[...] = jnp.concatenate([b_t[(s + 1) * tt - 1:(s + 1) * tt] for s in range(n_seq)], axis=0)
        heads_s[:, W_A:W_A + W_B] = (b_t * _gelu_tanh(proj[:, OFF_GB:OFF_GB + W_B])).astype(BF16)

    qc = proj[:, OFF_QC:OFF_QC + QKC_PAD] * (DK_C ** -0.5)
    kc = proj[:, OFF_KC:OFF_KC + QKC_PAD]
    logf = -_softplus(-(_mm(small, w2_ref[...]) + vec(ROW_B2, QKC_PAD))) * (1.0 / GLA_TAU)
    b_cum = masked_sum(m_cum, logf, 2)
    b_tot = masked_sum(m_tot, logf, 2)
    qc_s[...] = qc
    kc_s[...] = kc
    vc_s[...] = proj[:, OFF_VC:OFF_VC + W_C]
    qb_s[...] = qc * jnp.exp(b_cum)
    kdec_s[...] = kc * jnp.exp(b_tot - b_cum)
    btot_s[...] = b_tot
    for lvl in range(n_levels):
        qt_s[lvl] = qc * jnp.exp(masked_sum(masks_ref[2 + 2 * lvl], logf, 1))
        kt_s[lvl] = kc * jnp.exp(masked_sum(masks_ref[3 + 2 * lvl], logf, 1))

    s_i = _iota((STACK, STACK), 0)
    s_j = _iota((STACK, STACK), 1)
    same_blk = (s_i >> lc) == (s_j >> lc)
    m_incl = same_blk & (s_j <= s_i)
    m_strict = same_blk & (s_j < s_i)
    eye = jnp.where(s_i == s_j, 1.0, 0.0)
    lane = _iota((GROUP_ROWS, LANES), 1)
    m0 = lane < HALF
    norm_a = vec(ROW_NORM_A, LANES)
    norm_c = vec(ROW_NORM_C, LANES)

    def delta_setup(gi):
        rs = pl.ds(gi * GROUP_ROWS, GROUP_ROWS)
        beta_g = beta_s[rs, :]
        gcum_g = gcum_s[rs, :]
        gtot_g = gtot_s[rs, :]
        for p in range(N_PAIRS):
            sl = pl.ds(p * LANES, LANES)
            q_p = qa_s[rs, sl]
            k_p = ka_s[rs, sl]
            v0, v1 = _dup_values(va_s[rs, sl])
            q_st = _stack_rows(n_sb, c, [jnp.where(m0, q_p, 0.0), jnp.where(m0, 0.0, q_p)])
            k_st = _stack_rows(n_sb, c, [jnp.where(m0, k_p, 0.0), jnp.where(m0, 0.0, k_p)])
            v_st = _stack_rows(n_sb, c, [v0, v1])
            beta_c = _col_bcast_stack(beta_g, LANE_BETA + 2 * p, n_sb, c)
            g_c = _col_bcast_stack(gcum_g, LANE_DECAY + 2 * p, n_sb, c)
            gt_c = _col_bcast_stack(gtot_g, LANE_DECAY + 2 * p, n_sb, c)
            dec = jnp.exp(jnp.where(m_incl, g_c - g_c.T, -1e30))
            kq = _mm_nt(jnp.concatenate([k_st, q_st], axis=0), k_st)
            kk, qk = kq[0:STACK], kq[STACK:]
            n_mat = -(beta_c * kk * jnp.where(m_strict, dec, 0.0))
            eg = jnp.exp(g_c)
            t_s[gi, p] = eye + n_mat
            p_s[gi, p] = n_mat.astype(BF16)
            rhs_s[gi, p] = jnp.concatenate([beta_c * v_st, beta_c * eg * k_st], axis=1).astype(BF16)
            qe_s[gi, p] = (eg * q_st).astype(BF16)
            qkd_s[gi, p] = (qk * dec).astype(BF16)
            kdst_s[gi, p] = k_st * jnp.exp(gt_c - g_c)
            gtst_s[gi, p] = gt_c

    def inverse_level(gi, j):
        for p in range(N_PAIRS):
            pw = p_s[gi, p]
            if j >= 2:
                t_old = t_s[gi, p]
                both = _bdot(jnp.concatenate([pw, t_old.astype(BF16)], axis=0), pw)
                t_s[gi, p] = t_old + both[STACK:]
                p_s[gi, p] = both[0:STACK].astype(BF16)
            else:
                p_s[gi, p] = _bdot(pw, pw).astype(BF16)

    def delta_solve(gi):
        for p in range(N_PAIRS):
            t_old = t_s[gi, p]
            t_fin = t_old + _bdot(t_old.astype(BF16), p_s[gi, p])
            uwk = _bdot(t_fin.astype(BF16), rhs_s[gi, p]).astype(BF16)
            k_dec = kdst_s[gi, p]
            if n_sb == 1:
                both = _bdot(jnp.concatenate([qkd_s[gi, p], k_dec.T.astype(BF16)], axis=0), uwk)
                o_qw, c_kw = both[0:STACK], [both[STACK:]]
            else:
                o_qw = _bdot(qkd_s[gi, p], uwk)
                c_kw = [_bdot(k_dec[sb * 2 * c:(sb + 1) * 2 * c].T.astype(BF16), uwk[sb * 2 * c:(sb + 1) * 2 * c])
                        for sb in range(n_sb)]
            oprime_s[gi, p] = o_qw[:, 0:LANES]
            qprime_s[gi, p] = (qe_s[gi, p].astype(F32) - o_qw[:, LANES:]).astype(BF16)
            for sb in range(n_sb):
                c_s[gi, p, sb] = c_kw[sb][:, 0:LANES]
                kw_s[gi, p, sb] = c_kw[sb][:, LANES:].astype(BF16)

    def gla_precompute(gi):
        rs = pl.ds(gi * GROUP_ROWS, GROUP_ROWS)
        for p in range(N_PAIRS):
            sl = pl.ds(p * LANES, LANES)
            slab = pl.ds((p // 2) * LANES, LANES)
            l0 = 2 * DK_C * (p % 2)
            mh0 = (lane >= l0) & (lane < l0 + DK_C)
            mh1 = (lane >= l0 + DK_C) & (lane < l0 + 2 * DK_C)

            def stack_c(ref, idx=None):
                val = ref[rs, slab] if idx is None else ref[idx, rs, slab]
                return _stack_rows(n_sb, c, [jnp.where(mh0, val, 0.0), jnp.where(mh1, val, 0.0)])

            def stack_k(ref, idx=None):
                val = ref[rs, slab] if idx is None else ref[idx, rs, slab]
                return _stack_rows(n_sb, c, [val, val])

            att = jnp.where(s_i == s_j, _mm_nt(stack_c(qc_s), stack_k(kc_s)), 0.0)
            for lvl in range(n_levels):
                lh = lc - 1 - lvl
                half = 1 << lh
                q_l = stack_c(qt_s, lvl)
                k_l = stack_k(kt_s, lvl)
                if half >= SUBLANES:
                    q_up = jnp.concatenate([q_l[b0 + half:b0 + 2 * half] for b0 in range(0, STACK, 2 * half)], axis=0)
                    prod = _mm_nt(q_up, k_l)
                    u_i = _iota(prod.shape, 0)
                    u_j = _iota(prod.shape, 1)
                    ok = ((u_i >> lh) == (u_j >> (lh + 1))) & (((u_j >> lh) & 1) == 0)
                    prod = jnp.where(ok, prod, 0.0)
                    zeros = jnp.zeros((half, STACK), F32)
                    pieces = []
                    for n_blk in range(STACK // (2 * half)):
                        pieces += [zeros, prod[n_blk * half:(n_blk + 1) * half]]
                    att = att + jnp.concatenate(pieces, axis=0)
                else:
                    valid = (((s_i >> (lh + 1)) == (s_j >> (lh + 1)))
                             & (((s_i >> lh) & 1) == 1) & (((s_j >> lh) & 1) == 0))
                    att = att + jnp.where(valid, _mm_nt(q_l, k_l), 0.0)
            vc0, vc1 = _dup_values(vc_s[rs, sl])
            vc_st = _stack_rows(n_sb, c, [vc0, vc1]).astype(BF16)
            qbst_s[gi, p] = stack_c(qb_s).astype(BF16)
            kd_st = stack_c(kdec_s)
            if n_sb == 1:
                both = _bdot(jnp.concatenate([att.astype(BF16), kd_st.T.astype(BF16)], axis=0), vc_st)
                av_s[gi, p] = both[0:STACK]
                gk_s[gi, p, 0] = both[STACK:]
            else:
                av_s[gi, p] = _bdot(att.astype(BF16), vc_st)
                for sb in range(n_sb):
                    blk = slice(sb * 2 * c, (sb + 1) * 2 * c)
                    gk_s[gi, p, sb] = _bdot(kd_st[blk].T.astype(BF16), vc_st[blk])

    for gi in range(n_groups):
        delta_setup(gi)
    for j in range(1, lc):
        for gi in range(n_groups):
            inverse_level(gi, j)
        if j <= n_groups:
            gla_precompute(j - 1)
        if j == 1:
            mixer_b()
    for gi in range(lc - 1, n_groups):
        gla_precompute(gi)
    for gi in range(n_groups):
        delta_solve(gi)

    def recurrence_body(gi):
        rs = pl.ds(gi * GROUP_ROWS, GROUP_ROWS)
        for p in range(N_PAIRS):
            sl = pl.ds(p * LANES, LANES)
            q_prime = qprime_s[gi, p]
            gt_c = gtst_s[gi, p]
            qs_parts = []
            for sb in range(n_sb):
                seq = (gi * GROUP_ROWS + sb * c) // tt
                blk = slice(sb * 2 * c, (sb + 1) * 2 * c)
                s_old = sd_s[seq, p]
                s_bf = s_old.astype(BF16)
                if n_sb == 1:
                    both = _bdot(jnp.concatenate([q_prime, kw_s[gi, p, 0]], axis=0), s_bf)
                    qs, kws = both[0:STACK], both[STACK:]
                else:
                    qs, kws = _bdot(q_prime[blk], s_bf), _bdot(kw_s[gi, p, sb], s_bf)
                qs_parts.append(qs)
                decay_rows = jnp.concatenate(
                    [jnp.broadcast_to(gt_c[sb * 2 * c + a * c:sb * 2 * c + a * c + 1, :], (GROUP_ROWS, LANES))
                     for a in range(2)], axis=0)
                sd_s[seq, p] = jnp.exp(decay_rows) * s_old + (c_s[gi, p, sb] - kws)
            qs_all = qs_parts[0] if n_sb == 1 else jnp.concatenate(qs_parts, axis=0)
            o_st = qs_all + oprime_s[gi, p]
            o_st = o_st * lax.rsqrt(jnp.mean(o_st * o_st, axis=-1, keepdims=True) + EPS) * norm_a
            z_a = proj[gi * GROUP_ROWS:(gi + 1) * GROUP_ROWS, OFF_ZA + p * LANES:OFF_ZA + (p + 1) * LANES]
            heads_s[rs, sl] = (_unstack_rows(o_st, n_sb, c) * _silu(z_a)).astype(BF16)
            slab = pl.ds((p // 2) * LANES, LANES)
            qb_st = qbst_s[gi, p]
            btot_t = btot_s[rs, slab].T
            oi_parts = []
            for sb in range(n_sb):
                seq = (gi * GROUP_ROWS + sb * c) // tt
                blk = slice(sb * 2 * c, (sb + 1) * 2 * c)
                s_old = sg_s[seq, p]
                oi_parts.append(_bdot(qb_st[blk], s_old.astype(BF16)))
                decay_col = jnp.broadcast_to(btot_t[:, sb * c:sb * c + 1], (LANES, LANES))
                sg_s[seq, p] = jnp.exp(decay_col) * s_old + gk_s[gi, p, sb]
            oi_all = oi_parts[0] if n_sb == 1 else jnp.concatenate(oi_parts, axis=0)
            oc_st = oi_all + av_s[gi, p]
            oc_st = oc_st * lax.rsqrt(jnp.mean(oc_st * oc_st, axis=-1, keepdims=True) + EPS) * norm_c
            z_c = proj[gi * GROUP_ROWS:(gi + 1) * GROUP_ROWS, OFF_ZC + p * LANES:OFF_ZC + (p + 1) * LANES]
            heads_s[rs, pl.ds(W_A + W_B + p * LANES, LANES)] = (
                _unstack_rows(oc_st, n_sb, c) * _silu(z_c)).astype(BF16)

    groups_per_seq = max(tt // GROUP_ROWS, 1)
    for k in range(groups_per_seq):
        for gi in range(k, n_groups, groups_per_seq):
            recurrence_body(gi)

    y = ALPHA * x + _bdot(heads_s[...], w_out_ref[...])
    write_x1(_layer_norm(y, vec(ROW_LN1_G, D_MODEL), vec(ROW_LN1_B, D_MODEL)))


def _mixer_scratch(n_seq, tt, c):
    rows = n_seq * tt
    n_levels = _log2(c)
    n_groups = rows // GROUP_ROWS
    scratch = [
        pltpu.VMEM((n_seq, N_PAIRS, STACK, LANES), F32),
        pltpu.VMEM((n_seq, N_PAIRS, LANES, LANES), F32),
        pltpu.VMEM((n_seq, HIST + tt, CONV_A_WIDTH), F32),
        pltpu.VMEM((n_seq, HIST + tt, W_B), F32),
        pltpu.VMEM((n_seq, W_B), F32),
        pltpu.VMEM((rows, D_MIX), BF16),
        pltpu.VMEM((rows, D_MODEL), BF16),
        pltpu.VMEM((rows, QK_A), F32),
        pltpu.VMEM((rows, QK_A), F32),
        pltpu.VMEM((rows, W_A), F32),
        pltpu.VMEM((rows, LANES), F32),
        pltpu.VMEM((rows, LANES), F32),
        pltpu.VMEM((rows, LANES), F32),
        pltpu.VMEM((rows, QKC_PAD), F32),
        pltpu.VMEM((rows, QKC_PAD), F32),
        pltpu.VMEM((rows, W_C), F32),
        pltpu.VMEM((rows, QKC_PAD), F32),
        pltpu.VMEM((rows, QKC_PAD), F32),
        pltpu.VMEM((rows, QKC_PAD), F32),
        pltpu.VMEM((n_levels, rows, QKC_PAD), F32),
        pltpu.VMEM((n_levels, rows, QKC_PAD), F32),
        pltpu.VMEM((n_groups, N_PAIRS, STACK, STACK), F32),
        pltpu.VMEM((n_groups, N_PAIRS, STACK, STACK), BF16),
        pltpu.VMEM((n_groups, N_PAIRS, STACK, 2 * LANES), BF16),
        pltpu.VMEM((n_groups, N_PAIRS, STACK, LANES), BF16),
        pltpu.VMEM((n_groups, N_PAIRS, STACK, STACK), BF16),
        pltpu.VMEM((n_groups, N_PAIRS, STACK, LANES), F32),
        pltpu.VMEM((n_groups, N_PAIRS, STACK, LANES), F32),
        pltpu.VMEM((n_groups, N_PAIRS, STACK, LANES), F32),
        pltpu.VMEM((n_groups, N_PAIRS, STACK, LANES), BF16),
        pltpu.VMEM((n_groups, N_PAIRS, GROUP_ROWS // c, LANES, LANES), F32),
        pltpu.VMEM((n_groups, N_PAIRS, GROUP_ROWS // c, LANES, LANES), BF16),
        pltpu.VMEM((n_groups, N_PAIRS, STACK, LANES), F32),
        pltpu.VMEM((n_groups, N_PAIRS, STACK, LANES), BF16),
        pltpu.VMEM((n_groups, N_PAIRS, GROUP_ROWS // c, LANES, LANES), F32),
    ]
    assert len(scratch) == N_MIXER_SCRATCH
    return scratch


def _ffn_stage(stage, x, state_ref, weight_refs, y_ref, tail_ref, scratch_refs, *, n_seq, tt):
    (w_up_ref, vec_ref, wd_ref) = weight_refs
    (gbuf_s, h_s, xb_s) = scratch_refs
    rows = n_seq * tt
    tail = slice(HIST - (CONV_F - 1), HIST)

    if stage == 'init':
        if state_ref is not None:
            gbuf_s[:, tail, :] = state_ref[...]
        else:
            gbuf_s[:, tail, :] = jnp.zeros((n_seq, CONV_F - 1, D_FF), F32)
        return
    if stage == 'final':
        tail_ref[...] = gbuf_s[:, tt + HIST - (CONV_F - 1):tt + HIST, :]
        return

    xb_s[...] = x.astype(BF16)
    for j in range(N_FF_CHUNKS):
        cols = slice(j * FF_CHUNK, (j + 1) * FF_CHUNK)
        gate = _bdot(xb_s[...], w_up_ref[:, cols])
        val = _bdot(xb_s[...], w_up_ref[:, D_FF + j * FF_CHUNK:D_FF + (j + 1) * FF_CHUNK])
        gbuf_s[:, HIST:, cols] = gate.reshape(n_seq, tt, FF_CHUNK)
        conv = _conv_from_buffer(gbuf_s, vec_ref, ROW_CONV_F, CONV_F, tt, cols).reshape(rows, FF_CHUNK)
        h = _gelu_tanh(conv + vec_ref[ROW_CONV_F_BIAS:ROW_CONV_F_BIAS + 1, cols]) * val
        h_s[:, cols] = h.astype(BF16)
    gbuf_s[:, tail, :] = gbuf_s[:, tt + HIST - (CONV_F - 1):tt + HIST, :]
    y = ALPHA * x + _bdot(h_s[...], wd_ref[...])
    ln_g = vec_ref[ROW_LN2_G:ROW_LN2_G + 1, 0:D_MODEL]
    ln_b = vec_ref[ROW_LN2_B:ROW_LN2_B + 1, 0:D_MODEL]
    y_ref[...] = _layer_norm(y, ln_g, ln_b).reshape(n_seq, tt, D_MODEL)


def _mixer_kernel(*refs, n_seq, tt, c, has_state, n_alias):
    n_state = N_MIXER_OUTS if has_state else 0
    k = 1 + n_state
    x_ref, state_refs = refs[0], refs[1:k]
    weight_refs = refs[k:k + N_MIXER_WEIGHTS]
    k += N_MIXER_WEIGHTS + n_alias
    x1_ref = refs[k]
    out_refs = refs[k + 1:k + 1 + N_MIXER_OUTS]
    scratch_refs = refs[k + 1 + N_MIXER_OUTS:]
    t_idx = pl.program_id(1)

    def write_x1(val):
        x1_ref[...] = val.reshape(n_seq, tt, D_MODEL)

    stage = functools.partial(_mixer_stage, x_ref=x_ref, state_refs=state_refs, weight_refs=weight_refs,
                              out_refs=out_refs, scratch_refs=scratch_refs, write_x1=write_x1,
                              n_seq=n_seq, tt=tt, c=c)
    pl.when(t_idx == 0)(lambda: stage('init'))
    stage('body')
    pl.when(t_idx == pl.num_programs(1) - 1)(lambda: stage('final'))


def _ffn_kernel(*refs, n_seq, tt, has_state, n_alias):
    k = 2 if has_state else 1
    x_ref = refs[0]
    state_ref = refs[1] if has_state else None
    weight_refs = refs[k:k + 3]
    k += 3 + n_alias
    y_ref, tail_ref = refs[k:k + 2]
    scratch_refs = refs[k + 2:]
    t_idx = pl.program_id(1)
    stage = functools.partial(_ffn_stage, state_ref=state_ref, weight_refs=weight_refs, y_ref=y_ref,
                              tail_ref=tail_ref, scratch_refs=scratch_refs, n_seq=n_seq, tt=tt)
    pl.when(t_idx == 0)(lambda: stage('init', None))
    stage('body', x_ref[...].reshape(n_seq * tt, D_MODEL))
    pl.when(t_idx == pl.num_programs(1) - 1)(lambda: stage('final', None))


def _block_call(kern, name, x, states, state_shapes, weights, scratch, layer, prev_outs, *, n_seq, tt):
    nb, t_total, _ = x.shape
    assert nb % n_seq == 0 and t_total % tt == 0
    x_spec = pl.BlockSpec((n_seq, tt, D_MODEL), lambda b, t: (b, t, 0))
    in_specs, operands = [x_spec], [x]
    if states is not None:
        for arr, tail in zip(states, state_shapes):
            assert arr.shape[2:] == tail
            in_specs.append(pl.BlockSpec((None, n_seq) + tail, lambda b, t, z=(0,) * len(tail): (layer, b) + z))
            operands.append(arr)
    for a, per_layer in weights:
        if per_layer:
            in_specs.append(pl.BlockSpec((None,) + a.shape[1:], lambda b, t, z=(0,) * (a.ndim - 1): (layer,) + z))
        else:
            in_specs.append(pl.BlockSpec(a.shape, lambda b, t, z=(0,) * a.ndim: z))
        operands.append(a)
    aliases = {}
    if prev_outs is not None:
        for i, arr in enumerate(prev_outs):
            aliases[len(operands)] = 1 + i
            in_specs.append(pl.BlockSpec(memory_space=pl.ANY))
            operands.append(arr)
    out_shape = [jax.ShapeDtypeStruct((nb, t_total, D_MODEL), F32)]
    out_specs = [x_spec]
    for tail in state_shapes:
        out_shape.append(jax.ShapeDtypeStruct((DEPTH, nb) + tail, F32))
        out_specs.append(pl.BlockSpec((None, n_seq) + tail, lambda b, t, z=(0,) * len(tail): (layer, b) + z))
    return pl.pallas_call(
        functools.partial(kern, n_alias=len(aliases)),
        grid=(nb // n_seq, t_total // tt),
        in_specs=in_specs,
        out_specs=out_specs,
        out_shape=out_shape,
        input_output_aliases=aliases,
        scratch_shapes=scratch,
        compiler_params=pltpu.CompilerParams(
            dimension_semantics=("arbitrary", "arbitrary"), vmem_limit_bytes=VMEM_LIMIT_BYTES),
        name=f"{name}_nseq{n_seq}_tt{tt}",
    )(*operands)


def _mixer_call(x, states, w, layer, prev_outs, *, n_seq, tt, c):
    assert (tt % MASK_ROWS == 0 and c == GROUP_ROWS) or (tt == HIST and c == HIST and n_seq * tt <= MASK_ROWS)
    assert (n_seq * tt) % GROUP_ROWS == 0
    state_shapes = MIXER_STATE_SHAPES
    weights = [(_tile_masks(min(n_seq * tt, MASK_ROWS), c), False)] + [
        (w[k], True) for k in ('w_in', 'mix_vec', 'wr', 'wi', 'w2', 'w_out')]
    kern = functools.partial(_mixer_kernel, n_seq=n_seq, tt=tt, c=c, has_state=states is not None)
    return _block_call(kern, 'mixer', x, states, state_shapes, weights, _mixer_scratch(n_seq, tt, c), layer,
                       prev_outs, n_seq=n_seq, tt=tt)


def _ffn_call(x, state, w, layer, prev_out, *, n_seq, tt):
    weights = [(w[k], True) for k in ('w_up', 'ffn_vec', 'w_down')]
    scratch = [pltpu.VMEM((n_seq, HIST + tt, D_FF), F32),
               pltpu.VMEM((n_seq * tt, D_FF), BF16),
               pltpu.VMEM((n_seq * tt, D_MODEL), BF16)]
    kern = functools.partial(_ffn_kernel, n_seq=n_seq, tt=tt, has_state=state is not None)
    return _block_call(kern, 'ffn', x, None if state is None else (state,), (FFN_STATE_SHAPE,), weights,
                       scratch, layer, (prev_out,), n_seq=n_seq, tt=tt)


def _vec_table(pieces, width, n_rows):
    rows = []
    used = 0
    for arr, lane0 in pieces:
        arr = arr.astype(F32)
        rows.append(jnp.pad(arr, ((0, 0), (0, 0), (lane0, width - lane0 - arr.shape[2]))))
        used += arr.shape[1]
    depth = pieces[0][0].shape[0]
    rows.append(jnp.zeros((depth, n_rows - used, width), F32))
    return jnp.concatenate(rows, axis=1)


def _prep_weights(w_in, conv_a_w, a_log, dt_bias, norm_a_w, conv_b_w, conv_b_b, lru_w_r, lru_b_r, lru_w_i,
                  lru_b_i, lru_lambda, gla_w2, gla_b2, norm_c_w, w_out, ln1_g, ln1_b, ffn_w_up, ffn_conv_w,
                  ffn_conv_b, ffn_w_down, ln2_g, ln2_b):
    depth, d, _ = w_in.shape
    pts = [0]
    for s in (QK_A, QK_A, W_A, W_A, H_A, H_A, W_B, W_B, QK_C, QK_C, W_C, W_C, GLA_RANK):
        pts.append(pts[-1] + s)
    (qa, ka, va, za, ba, aa, xb, gb, qc, kc, vc, zc, lc) = [w_in[:, :, pts[i]:pts[i + 1]] for i in range(13)]
    z = lambda n: jnp.zeros((depth, d, n), w_in.dtype)
    small = jnp.concatenate([ba, z(LANE_DECAY - H_A), aa, z(LANE_LOWRANK - LANE_DECAY - H_A), lc,
                             z(LANES - LANE_LOWRANK - GLA_RANK)], axis=2)
    w_in_p = jnp.concatenate([qa, ka, va, za, xb, gb, qc, z(QKC_PAD - QK_C), kc, z(QKC_PAD - QK_C), vc, zc, small],
                             axis=2).astype(BF16)
    assert w_in_p.shape[2] == D_IN_PAD

    eye_blocks = jnp.eye(LRU_BLOCKS, dtype=F32)[None, :, None, :, None]

    def block_diag(wb):
        return (wb[:, :, :, None, :] * eye_blocks).reshape(depth, W_B, W_B).astype(BF16)

    w2 = jnp.pad(gla_w2, ((0, 0), (LANE_LOWRANK, LANES - LANE_LOWRANK - GLA_RANK), (0, QKC_PAD - QK_C))).astype(BF16)
    r1 = lambda v: v[:, None, :]
    dup = lambda v: jnp.concatenate([v, v], axis=1)[:, None, :]
    mix_vec = _vec_table(
        [(conv_a_w, 0), (conv_b_w, 0), (r1(conv_b_b), 0), (r1(a_log), LANE_DECAY), (r1(dt_bias), LANE_DECAY),
         (dup(norm_a_w), 0), (r1(lru_b_r), 0), (r1(lru_b_i), 0), (r1(lru_lambda), 0), (r1(gla_b2), 0),
         (dup(norm_c_w), 0), (r1(ln1_g), 0), (r1(ln1_b), 0)], VEC_W, MIX_VEC_ROWS)
    ffn_vec = _vec_table([(ffn_conv_w, 0), (r1(ffn_conv_b), 0), (r1(ln2_g), 0), (r1(ln2_b), 0)], D_FF, FFN_VEC_ROWS)
    return {
        'w_in': w_in_p, 'mix_vec': mix_vec, 'wr': block_diag(lru_w_r), 'wi': block_diag(lru_w_i), 'w2': w2,
        'w_out': w_out.astype(BF16), 'w_up': ffn_w_up.astype(BF16), 'ffn_vec': ffn_vec,
        'w_down': ffn_w_down.astype(BF16),
    }


def _trunk(x, states, w, *, n_seq, tt, c):
    nb = x.shape[0]
    if states is not None:
        st_dconv, st_delta, st_lconv, st_lru, st_gla, st_fconv = (s.astype(F32) for s in states)
        states = (st_dconv, st_delta.reshape(DEPTH, nb, N_PAIRS, 2 * DK_A, DV_A), st_lconv,
                  st_lru.reshape(DEPTH, nb, 1, W_B), st_gla.reshape(DEPTH, nb, QK_C, DV_C), st_fconv)
    mix_outs = [jnp.zeros((DEPTH, nb) + tail, F32) for tail in MIXER_STATE_SHAPES]
    ffn_out = jnp.zeros((DEPTH, nb) + FFN_STATE_SHAPE, F32)
    for l in range(DEPTH):
        x1, *mix_outs = _mixer_call(x, None if states is None else states[:5], w, l, mix_outs,
                                    n_seq=n_seq, tt=tt, c=c)
        if n_seq > 1:
            ffn_tile = dict(n_seq=math.gcd(nb, max(n_seq, FFN_TILE_ROWS // tt)), tt=tt)
        else:
            ffn_tile = dict(n_seq=1, tt=min(FFN_TILE_ROWS, x.shape[1]))
        x, ffn_out = _ffn_call(x1, None if states is None else states[5], w, l, ffn_out, **ffn_tile)
    dconv, sd, lconv, h, sg = mix_outs
    return x, (dconv, sd.reshape(DEPTH, nb, H_A, DK_A, DV_A), lconv, h.reshape(DEPTH, nb, W_B),
               sg.reshape(DEPTH, nb, H_C, DK_C, DV_C), ffn_out)


def kernel(x_prompt, x_sample, state_delta_conv, state_delta, state_lru_conv, state_lru, state_gla, state_ffn_conv,
           w_in, conv_a_w, a_log, dt_bias, norm_a_w, conv_b_w, conv_b_b, lru_w_r, lru_b_r, lru_w_i, lru_b_i,
           lru_lambda, gla_w2, gla_b2, norm_c_w, w_out, ln1_g, ln1_b, ffn_w_up, ffn_conv_w, ffn_conv_b, ffn_w_down,
           ln2_g, ln2_b):
    assert w_in.shape[0] == DEPTH
    w = _prep_weights(w_in, conv_a_w, a_log, dt_bias, norm_a_w, conv_b_w, conv_b_b, lru_w_r, lru_b_r, lru_w_i,
                      lru_b_i, lru_lambda, gla_w2, gla_b2, norm_c_w, w_out, ln1_g, ln1_b, ffn_w_up, ffn_conv_w,
                      ffn_conv_b, ffn_w_down, ln2_g, ln2_b)
    sample_states = (state_delta_conv, state_delta, state_lru_conv, state_lru, state_gla, state_ffn_conv)
    t_p = x_prompt.shape[1]
    t_s = x_sample.shape[1]
    n_p = PROMPT_NSEQ if x_prompt.shape[0] % PROMPT_NSEQ == 0 else 1
    y_p, p_st = _trunk(x_prompt, None, w, n_seq=n_p, tt=min(PROMPT_TT, t_p), c=CHUNK)
    y_s, s_st = _trunk(x_sample, sample_states, w, n_seq=min(SAMPLE_NSEQ, x_sample.shape[0]), tt=t_s, c=t_s)
    return (y_p, y_s) + p_st + s_st
```

```python
import functools
import math

import numpy as np
import jax
import jax.numpy as jnp
from jax import lax
from jax.experimental import pallas as pl
from jax.experimental.pallas import tpu as pltpu

F32 = jnp.float32
BF16 = jnp.bfloat16

D_MODEL = 1024
H_A, DK_A, DV_A = 6, 64, 64
QK_A = H_A * DK_A
W_A = H_A * DV_A
CONV_A = 4
W_B = 256
LRU_BLOCKS = 4
LRU_BLOCK = W_B // LRU_BLOCKS
CONV_B = 4
LRU_C = 8.0
H_C, DK_C, DV_C = 6, 32, 64
QK_C = H_C * DK_C
W_C = H_C * DV_C
GLA_RANK = 16
GLA_TAU = 16.0
D_MIX = W_A + W_B + W_C
CHUNK = 64
D_FF = 2816
CONV_F = 3
EPS = 1e-6
CONV_A_WIDTH = 2 * QK_A + W_A
DEPTH = 2
ALPHA = (2.0 * DEPTH) ** 0.25

LANES = 128
SUBLANES = 8
HALF = LANES // 2
GROUP_ROWS = 64
STACK = 2 * GROUP_ROWS
N_PAIRS = 3
HIST = SUBLANES
FF_CHUNK = 256
N_FF_CHUNKS = D_FF // FF_CHUNK
VMEM_LIMIT_BYTES = 56 * 1024 * 1024
PROMPT_TT = 512
PROMPT_NSEQ = 1
MASK_ROWS = 256
FFN_TILE_ROWS = 512
SAMPLE_NSEQ = 16

OFF_QA = 0
OFF_KA = OFF_QA + QK_A
OFF_VA = OFF_KA + QK_A
OFF_ZA = OFF_VA + W_A
OFF_XB = OFF_ZA + W_A
OFF_GB = OFF_XB + W_B
OFF_QC = OFF_GB + W_B
QKC_PAD = 2 * LANES
OFF_KC = OFF_QC + QKC_PAD
OFF_VC = OFF_KC + QKC_PAD
OFF_ZC = OFF_VC + W_C
OFF_SMALL = OFF_ZC + W_C
D_IN_PAD = OFF_SMALL + LANES
LANE_BETA = 0
LANE_DECAY = 8
LANE_LOWRANK = 16

VEC_W = CONV_A_WIDTH
(ROW_CONV_A, ROW_CONV_B, ROW_CONV_B_BIAS, ROW_ALOG, ROW_DTB, ROW_NORM_A, ROW_BR, ROW_BI, ROW_LAM, ROW_B2,
 ROW_NORM_C, ROW_LN1_G, ROW_LN1_B) = (0, 4, 8, 9, 10, 11, 12, 13, 14, 15, 16, 17, 18)
MIX_VEC_ROWS = 24
(ROW_CONV_F, ROW_CONV_F_BIAS, ROW_LN2_G, ROW_LN2_B) = (0, 3, 4, 5)
FFN_VEC_ROWS = 8


def _bdot(a, b):
    return jnp.dot(a, b, preferred_element_type=F32)


def _mm(a, b):
    return _bdot(a.astype(BF16), b.astype(BF16))


def _mm_nt(a, b):
    return lax.dot_general(a.astype(BF16), b.astype(BF16), (((1,), (1,)), ((), ())), preferred_element_type=F32)


def _split(a, n):
    parts = []
    r = a
    for i in range(n):
        p = r.astype(BF16)
        parts.append(p)
        if i + 1 < n:
            r = r - p.astype(F32)
    return parts


def _mm_mask_lhs(mask_bf16, a, n):
    out = None
    for p in _split(a, n):
        t = _bdot(mask_bf16, p)
        out = t if out is None else out + t
    return out


def _softplus(x):
    return jnp.maximum(x, 0.0) + jnp.log1p(jnp.exp(-jnp.abs(x)))


def _sigmoid(x):
    return 0.5 * jnp.tanh(0.5 * x) + 0.5


def _silu(x):
    h = 0.5 * x
    return h * jnp.tanh(h) + h


def _gelu_tanh(x):
    return 0.5 * x * (1.0 + jnp.tanh(math.sqrt(2.0 / math.pi) * (x + 0.044715 * (x * x * x))))


def _layer_norm(x, g, b):
    mu = jnp.mean(x, axis=-1, keepdims=True)
    xc = x - mu
    var = jnp.mean(xc * xc, axis=-1, keepdims=True)
    return xc * lax.rsqrt(var + EPS) * g + b


def _iota(shape, axis):
    return lax.broadcasted_iota(jnp.int32, shape, axis)


def _log2(n):
    l = int(math.log2(n))
    assert (1 << l) == n
    return l


def _conv_from_buffer(buf_ref, vec_ref, row0, width, tt, cols):
    y = None
    for d in range(width):
        tap = vec_ref[row0 + width - 1 - d:row0 + width - d, cols]
        term = buf_ref[:, HIST - d:HIST - d + tt, cols] * tap
        y = term if y is None else y + term
    return y


def _stack_rows(n_sb, c, pieces):
    if n_sb == 1:
        return jnp.concatenate([pieces[0], pieces[1]], axis=0)
    out = []
    for s in range(n_sb):
        for a in range(2):
            out.append(pieces[a][s * c:(s + 1) * c])
    return jnp.concatenate(out, axis=0)


def _unstack_rows(o, n_sb, c):
    lo = _iota((c, LANES), 1) < HALF
    out = []
    for s in range(n_sb):
        base = s * 2 * c
        out.append(jnp.where(lo, o[base:base + c], o[base + c:base + 2 * c]))
    return out[0] if n_sb == 1 else jnp.concatenate(out, axis=0)


def _col_bcast_stack(vals, lane0, n_sb, c):
    pieces = [jnp.broadcast_to(vals[:, lane0 + a:lane0 + a + 1], (GROUP_ROWS, LANES)) for a in range(2)]
    return _stack_rows(n_sb, c, pieces)


def _dup_values(v):
    lo = _iota(v.shape, 1) < HALF
    vr = pltpu.roll(v, HALF, axis=1)
    return jnp.where(lo, v, vr), jnp.where(lo, vr, v)


def _tile_masks(rows, c):
    lc = _log2(c)
    i = np.arange(rows)[:, None]
    j = np.arange(rows)[None, :]
    same = (i >> lc) == (j >> lc)
    out = [same & (j <= i), same]
    for lvl in range(lc):
        lh = lc - 1 - lvl
        hi = (i >> lh) & 1
        start = (i >> lh) << lh
        nxt = ((i >> lh) + 1) << lh
        out.append((hi == 1) & (j >= start) & (j <= i))
        out.append((hi == 0) & (j > i) & (j < nxt))
    return jnp.asarray(np.stack(out).astype(np.float32), dtype=BF16)


MIXER_STATE_SHAPES = ((CONV_A - 1, CONV_A_WIDTH), (N_PAIRS, 2 * DK_A, DV_A), (CONV_B - 1, W_B), (1, W_B),
                      (QK_C, DV_C))
FFN_STATE_SHAPE = (CONV_F - 1, D_FF)
N_MIXER_WEIGHTS = 7
N_MIXER_OUTS = 5
N_MIXER_SCRATCH = 35


def _mixer_stage(stage, x_ref, state_refs, weight_refs, out_refs, scratch_refs, write_x1, *, n_seq, tt, c):
    has_state = len(state_refs) > 0
    (masks_ref, w_in_ref, vec_ref, wr_ref, wi_ref, w2_ref, w_out_ref) = weight_refs
    (dconv_out_ref, sd_out_ref, lconv_out_ref, h_out_ref, sg_out_ref) = out_refs
    (sd_s, sg_s, abuf_s, bbuf_s, h_s, heads_s, xb_s,
     qa_s, ka_s, va_s, beta_s, gcum_s, gtot_s,
     qc_s, kc_s, vc_s, qb_s, kdec_s, btot_s, qt_s, kt_s,
     t_s, p_s, rhs_s, qe_s, qkd_s, kdst_s, gtst_s, oprime_s, qprime_s, c_s, kw_s, av_s, qbst_s, gk_s,
     ) = scratch_refs

    rows = n_seq * tt
    n_groups = rows // GROUP_ROWS
    n_sb = GROUP_ROWS // c
    lc = _log2(c)
    n_levels = lc
    a_tail = slice(HIST - (CONV_A - 1), HIST)
    b_tail = slice(HIST - (CONV_B - 1), HIST)

    if stage == 'init':
        if has_state:
            dconv_ref, sd_ref, lconv_ref, h0_ref, sg_ref = state_refs
            abuf_s[:, a_tail, :] = dconv_ref[...]
            bbuf_s[:, b_tail, :] = lconv_ref[...]
            h_s[...] = h0_ref[...].reshape(n_seq, W_B)
            zeros = jnp.zeros((GROUP_ROWS, LANES), F32)
            for s in range(n_seq):
                for p in range(N_PAIRS):
                    sd = sd_ref[s, p]
                    sd_s[s, p] = jnp.concatenate([sd, sd], axis=-1)
                    sg = sg_ref[s, 2 * DK_C * p:2 * DK_C * (p + 1), :]
                    sg = jnp.concatenate([sg, sg], axis=-1)
                    sg_s[s, p] = jnp.concatenate([sg, zeros] if p % 2 == 0 else [zeros, sg], axis=0)
        else:
            abuf_s[:, a_tail, :] = jnp.zeros((n_seq, CONV_A - 1, CONV_A_WIDTH), F32)
            bbuf_s[:, b_tail, :] = jnp.zeros((n_seq, CONV_B - 1, W_B), F32)
            h_s[...] = jnp.zeros((n_seq, W_B), F32)
            sd_s[...] = jnp.zeros(sd_s.shape, F32)
            sg_s[...] = jnp.zeros(sg_s.shape, F32)
        return

    if stage == 'final':
        dconv_out_ref[...] = abuf_s[:, tt + HIST - (CONV_A - 1):tt + HIST, :]
        lconv_out_ref[...] = bbuf_s[:, tt + HIST - (CONV_B - 1):tt + HIST, :]
        h_out_ref[...] = h_s[...].reshape(n_seq, 1, W_B)
        for s in range(n_seq):
            for p in range(N_PAIRS):
                sd_out_ref[s, p] = sd_s[s, p][:, 0:HALF]
                r0 = GROUP_ROWS * (p % 2)
                sg_out_ref[s, 2 * DK_C * p:2 * DK_C * (p + 1), :] = sg_s[s, p][r0:r0 + GROUP_ROWS, 0:HALF]
        return

    def vec(row, width):
        return vec_ref[row:row + 1, 0:width]

    x = x_ref[...].reshape(rows, D_MODEL)
    xb_s[...] = x.astype(BF16)
    proj = _bdot(xb_s[...], w_in_ref[...])
    m_cum = masks_ref[0]
    m_tot = masks_ref[1]
    mask_rows = masks_ref.shape[1]

    def masked_sum(mask, a, n):
        out = [_mm_mask_lhs(mask, a[r:r + mask_rows], n) for r in range(0, rows, mask_rows)]
        return out[0] if len(out) == 1 else jnp.concatenate(out, axis=0)

    abuf_s[:, HIST:, :] = proj[:, OFF_QA:OFF_QA + CONV_A_WIDTH].reshape(n_seq, tt, CONV_A_WIDTH)
    conv_a = _conv_from_buffer(abuf_s, vec_ref, ROW_CONV_A, CONV_A, tt, slice(0, CONV_A_WIDTH))
    qkv = _silu(conv_a.reshape(rows, CONV_A_WIDTH))
    abuf_s[:, a_tail, :] = abuf_s[:, tt + HIST - (CONV_A - 1):tt + HIST, :]
    p_i = _iota((LANES, LANES), 0) >> _log2(DK_A)
    p_j = _iota((LANES, LANES), 1) >> _log2(DK_A)
    pair_ones = jnp.where(p_i == p_j, 1.0, 0.0).astype(BF16)
    for p in range(N_PAIRS):
        for off, dst, scale in ((0, qa_s, DK_A ** -0.5), (QK_A, ka_s, 1.0)):
            v = qkv[:, off + p * LANES:off + (p + 1) * LANES]
            ssq = _bdot((v * v).astype(BF16), pair_ones)
            dst[:, p * LANES:(p + 1) * LANES] = v * (lax.rsqrt(ssq + EPS) * scale)
    va_s[...] = qkv[:, 2 * QK_A:]

    small = proj[:, OFF_SMALL:OFF_SMALL + LANES]
    beta_s[...] = _sigmoid(small)
    g_full = -jnp.exp(vec(ROW_ALOG, LANES)) * _softplus(small + vec(ROW_DTB, LANES))
    gcum_s[...] = masked_sum(m_cum, g_full, 2)
    gtot_s[...] = masked_sum(m_tot, g_full, 2)

    def mixer_b():
        bbuf_s[:, HIST:, :] = proj[:, OFF_XB:OFF_XB + W_B].reshape(n_seq, tt, W_B)
        xc = (_conv_from_buffer(bbuf_s, vec_ref, ROW_CONV_B, CONV_B, tt, slice(0, W_B)).reshape(rows, W_B)
              + vec(ROW_CONV_B_BIAS, W_B))
        bbuf_s[:, b_tail, :] = bbuf_s[:, tt + HIST - (CONV_B - 1):tt + HIST, :]
        gate_r = _sigmoid(_mm(xc, wr_ref[...]) + vec(ROW_BR, W_B))
        gate_i = _sigmoid(_mm(xc, wi_ref[...]) + vec(ROW_BI, W_B))
        log_a = -LRU_C * gate_r * _softplus(-vec(ROW_LAM, W_B))
        a_t = jnp.exp(log_a)
        b_t = jnp.sqrt(-jnp.tanh(log_a) * (a_t * a_t + 1.0)) * (gate_i * xc)
        i_seq = _iota((rows, W_B), 0) & (tt - 1)
        h_prev = jnp.concatenate([jnp.broadcast_to(h_s[s:s + 1, :], (tt, W_B)) for s in range(n_seq)], axis=0)
        b_t = b_t + jnp.where(i_seq == 0, a_t * h_prev, 0.0)
        d = 1
        while d < tt:
            a_sh = pltpu.roll(a_t, d, axis=0)
            b_sh = pltpu.roll(b_t, d, axis=0)
            ok = i_seq >= d
            b_t = jnp.where(ok, a_t * b_sh + b_t, b_t)
            a_t = jnp.where(ok, a_t * a_sh, a_t)
            d *= 2
        h_s[...] = jnp.concatenate([b_t[(s + 1) * tt - 1:(s + 1) * tt] for s in range(n_seq)], axis=0)
        heads_s[:, W_A:W_A + W_B] = (b_t * _gelu_tanh(proj[:, OFF_GB:OFF_GB + W_B])).astype(BF16)

    qc = proj[:, OFF_QC:OFF_QC + QKC_PAD] * (DK_C ** -0.5)
    kc = proj[:, OFF_KC:OFF_KC + QKC_PAD]
    logf = -_softplus(-(_mm(small, w2_ref[...]) + vec(ROW_B2, QKC_PAD))) * (1.0 / GLA_TAU)
    b_cum = masked_sum(m_cum, logf, 2)
    b_tot = masked_sum(m_tot, logf, 2)
    qc_s[...] = qc
    kc_s[...] = kc
    vc_s[...] = proj[:, OFF_VC:OFF_VC + W_C]
    qb_s[...] = qc * jnp.exp(b_cum)
    kdec_s[...] = kc * jnp.exp(b_tot - b_cum)
    btot_s[...] = b_tot
    for lvl in range(n_levels):
        qt_s[lvl] = qc * jnp.exp(masked_sum(masks_ref[2 + 2 * lvl], logf, 1))
        kt_s[lvl] = kc * jnp.exp(masked_sum(masks_ref[3 + 2 * lvl], logf, 1))

    s_i = _iota((STACK, STACK), 0)
    s_j = _iota((STACK, STACK), 1)
    same_blk = (s_i >> lc) == (s_j >> lc)
    m_incl = same_blk & (s_j <= s_i)
    m_strict = same_blk & (s_j < s_i)
    eye = jnp.where(s_i == s_j, 1.0, 0.0)
    lane = _iota((GROUP_ROWS, LANES), 1)
    m0 = lane < HALF
    norm_a = vec(ROW_NORM_A, LANES)
    norm_c = vec(ROW_NORM_C, LANES)

    def delta_setup(gi):
        rs = pl.ds(gi * GROUP_ROWS, GROUP_ROWS)
        beta_g = beta_s[rs, :]
        gcum_g = gcum_s[rs, :]
        gtot_g = gtot_s[rs, :]
        for p in range(N_PAIRS):
            sl = pl.ds(p * LANES, LANES)
            q_p = qa_s[rs, sl]
            k_p = ka_s[rs, sl]
            v0, v1 = _dup_values(va_s[rs, sl])
            q_st = _stack_rows(n_sb, c, [jnp.where(m0, q_p, 0.0), jnp.where(m0, 0.0, q_p)])
            k_st = _stack_rows(n_sb, c, [jnp.where(m0, k_p, 0.0), jnp.where(m0, 0.0, k_p)])
            v_st = _stack_rows(n_sb, c, [v0, v1])
            beta_c = _col_bcast_stack(beta_g, LANE_BETA + 2 * p, n_sb, c)
            g_c = _col_bcast_stack(gcum_g, LANE_DECAY + 2 * p, n_sb, c)
            gt_c = _col_bcast_stack(gtot_g, LANE_DECAY + 2 * p, n_sb, c)
            dec = jnp.exp(jnp.where(m_incl, g_c - g_c.T, -1e30))
            kq = _mm_nt(jnp.concatenate([k_st, q_st], axis=0), k_st)
            kk, qk = kq[0:STACK], kq[STACK:]
            n_mat = -(beta_c * kk * jnp.where(m_strict, dec, 0.0))
            eg = jnp.exp(g_c)
            t_s[gi, p] = eye + n_mat
            p_s[gi, p] = n_mat.astype(BF16)
            rhs_s[gi, p] = jnp.concatenate([beta_c * v_st, beta_c * eg * k_st], axis=1).astype(BF16)
            qe_s[gi, p] = (eg * q_st).astype(BF16)
            qkd_s[gi, p] = (qk * dec).astype(BF16)
            kdst_s[gi, p] = k_st * jnp.exp(gt_c - g_c)
            gtst_s[gi, p] = gt_c

    def inverse_level(gi, j):
        for p in range(N_PAIRS):
            pw = p_s[gi, p]
            if j >= 2:
                t_old = t_s[gi, p]
                both = _bdot(jnp.concatenate([pw, t_old.astype(BF16)], axis=0), pw)
                t_s[gi, p] = t_old + both[STACK:]
                p_s[gi, p] = both[0:STACK].astype(BF16)
            else:
                p_s[gi, p] = _bdot(pw, pw).astype(BF16)

    def delta_solve(gi):
        for p in range(N_PAIRS):
            t_old = t_s[gi, p]
            t_fin = t_old + _bdot(t_old.astype(BF16), p_s[gi, p])
            uwk = _bdot(t_fin.astype(BF16), rhs_s[gi, p]).astype(BF16)
            k_dec = kdst_s[gi, p]
            if n_sb == 1:
                both = _bdot(jnp.concatenate([qkd_s[gi, p], k_dec.T.astype(BF16)], axis=0), uwk)
                o_qw, c_kw = both[0:STACK], [both[STACK:]]
            else:
                o_qw = _bdot(qkd_s[gi, p], uwk)
                c_kw = [_bdot(k_dec[sb * 2 * c:(sb + 1) * 2 * c].T.astype(BF16), uwk[sb * 2 * c:(sb + 1) * 2 * c])
                        for sb in range(n_sb)]
            oprime_s[gi, p] = o_qw[:, 0:LANES]
            qprime_s[gi, p] = (qe_s[gi, p].astype(F32) - o_qw[:, LANES:]).astype(BF16)
            for sb in range(n_sb):
                c_s[gi, p, sb] = c_kw[sb][:, 0:LANES]
                kw_s[gi, p, sb] = c_kw[sb][:, LANES:].astype(BF16)

    def gla_precompute(gi):
        rs = pl.ds(gi * GROUP_ROWS, GROUP_ROWS)
        for p in range(N_PAIRS):
            sl = pl.ds(p * LANES, LANES)
            slab = pl.ds((p // 2) * LANES, LANES)
            l0 = 2 * DK_C * (p % 2)
            mh0 = (lane >= l0) & (lane < l0 + DK_C)
            mh1 = (lane >= l0 + DK_C) & (lane < l0 + 2 * DK_C)

            def stack_c(ref, idx=None):
                val = ref[rs, slab] if idx is None else ref[idx, rs, slab]
                return _stack_rows(n_sb, c, [jnp.where(mh0, val, 0.0), jnp.where(mh1, val, 0.0)])

            def stack_k(ref, idx=None):
                val = ref[rs, slab] if idx is None else ref[idx, rs, slab]
                return _stack_rows(n_sb, c, [val, val])

            att = jnp.where(s_i == s_j, _mm_nt(stack_c(qc_s), stack_k(kc_s)), 0.0)
            for lvl in range(n_levels):
                lh = lc - 1 - lvl
                half = 1 << lh
                q_l = stack_c(qt_s, lvl)
                k_l = stack_k(kt_s, lvl)
                if half >= SUBLANES:
                    q_up = jnp.concatenate([q_l[b0 + half:b0 + 2 * half] for b0 in range(0, STACK, 2 * half)], axis=0)
                    prod = _mm_nt(q_up, k_l)
                    u_i = _iota(prod.shape, 0)
                    u_j = _iota(prod.shape, 1)
                    ok = ((u_i >> lh) == (u_j >> (lh + 1))) & (((u_j >> lh) & 1) == 0)
                    prod = jnp.where(ok, prod, 0.0)
                    zeros = jnp.zeros((half, STACK), F32)
                    pieces = []
                    for n_blk in range(STACK // (2 * half)):
                        pieces += [zeros, prod[n_blk * half:(n_blk + 1) * half]]
                    att = att + jnp.concatenate(pieces, axis=0)
                else:
                    valid = (((s_i >> (lh + 1)) == (s_j >> (lh + 1)))
                             & (((s_i >> lh) & 1) == 1) & (((s_j >> lh) & 1) == 0))
                    att = att + jnp.where(valid, _mm_nt(q_l, k_l), 0.0)
            vc0, vc1 = _dup_values(vc_s[rs, sl])
            vc_st = _stack_rows(n_sb, c, [vc0, vc1]).astype(BF16)
            qbst_s[gi, p] = stack_c(qb_s).astype(BF16)
            kd_st = stack_c(kdec_s)
            if n_sb == 1:
                both = _bdot(jnp.concatenate([att.astype(BF16), kd_st.T.astype(BF16)], axis=0), vc_st)
                av_s[gi, p] = both[0:STACK]
                gk_s[gi, p, 0] = both[STACK:]
            else:
                av_s[gi, p] = _bdot(att.astype(BF16), vc_st)
                for sb in range(n_sb):
                    blk = slice(sb * 2 * c, (sb + 1) * 2 * c)
                    gk_s[gi, p, sb] = _bdot(kd_st[blk].T.astype(BF16), vc_st[blk])

    for gi in range(n_groups):
        delta_setup(gi)
    for j in range(1, lc):
        for gi in range(n_groups):
            inverse_level(gi, j)
        if j <= n_groups:
            gla_precompute(j - 1)
        if j == 1:
            mixer_b()
    for gi in range(lc - 1, n_groups):
        gla_precompute(gi)
    for gi in range(n_groups):
        delta_solve(gi)

    def recurrence_body(gi):
        rs = pl.ds(gi * GROUP_ROWS, GROUP_ROWS)
        for p in range(N_PAIRS):
            sl = pl.ds(p * LANES, LANES)
            q_prime = qprime_s[gi, p]
            gt_c = gtst_s[gi, p]
            qs_parts = []
            for sb in range(n_sb):
                seq = (gi * GROUP_ROWS + sb * c) // tt
                blk = slice(sb * 2 * c, (sb + 1) * 2 * c)
                s_old = sd_s[seq, p]
                s_bf = s_old.astype(BF16)
                if n_sb == 1:
                    both = _bdot(jnp.concatenate([q_prime, kw_s[gi, p, 0]], axis=0), s_bf)
                    qs, kws = both[0:STACK], both[STACK:]
                else:
                    qs, kws = _bdot(q_prime[blk], s_bf), _bdot(kw_s[gi, p, sb], s_bf)
                qs_parts.append(qs)
                decay_rows = jnp.concatenate(
                    [jnp.broadcast_to(gt_c[sb * 2 * c + a * c:sb * 2 * c + a * c + 1, :], (GROUP_ROWS, LANES))
                     for a in range(2)], axis=0)
                sd_s[seq, p] = jnp.exp(decay_rows) * s_old + (c_s[gi, p, sb] - kws)
            qs_all = qs_parts[0] if n_sb == 1 else jnp.concatenate(qs_parts, axis=0)
            o_st = qs_all + oprime_s[gi, p]
            o_st = o_st * lax.rsqrt(jnp.mean(o_st * o_st, axis=-1, keepdims=True) + EPS) * norm_a
            z_a = proj[gi * GROUP_ROWS:(gi + 1) * GROUP_ROWS, OFF_ZA + p * LANES:OFF_ZA + (p + 1) * LANES]
            heads_s[rs, sl] = (_unstack_rows(o_st, n_sb, c) * _silu(z_a)).astype(BF16)
            slab = pl.ds((p // 2) * LANES, LANES)
            qb_st = qbst_s[gi, p]
            btot_t = btot_s[rs, slab].T
            oi_parts = []
            for sb in range(n_sb):
                seq = (gi * GROUP_ROWS + sb * c) // tt
                blk = slice(sb * 2 * c, (sb + 1) * 2 * c)
                s_old = sg_s[seq, p]
                oi_parts.append(_bdot(qb_st[blk], s_old.astype(BF16)))
                decay_col = jnp.broadcast_to(btot_t[:, sb * c:sb * c + 1], (LANES, LANES))
                sg_s[seq, p] = jnp.exp(decay_col) * s_old + gk_s[gi, p, sb]
            oi_all = oi_parts[0] if n_sb == 1 else jnp.concatenate(oi_parts, axis=0)
            oc_st = oi_all + av_s[gi, p]
            oc_st = oc_st * lax.rsqrt(jnp.mean(oc_st * oc_st, axis=-1, keepdims=True) + EPS) * norm_c
            z_c = proj[gi * GROUP_ROWS:(gi + 1) * GROUP_ROWS, OFF_ZC + p * LANES:OFF_ZC + (p + 1) * LANES]
            heads_s[rs, pl.ds(W_A + W_B + p * LANES, LANES)] = (
                _unstack_rows(oc_st, n_sb, c) * _silu(z_c)).astype(BF16)

    groups_per_seq = max(tt // GROUP_ROWS, 1)
    for k in range(groups_per_seq):
        for gi in range(k, n_groups, groups_per_seq):
            recurrence_body(gi)

    y = ALPHA * x + _bdot(heads_s[...], w_out_ref[...])
    write_x1(_layer_norm(y, vec(ROW_LN1_G, D_MODEL), vec(ROW_LN1_B, D_MODEL)))


def _mixer_scratch(n_seq, tt, c):
    rows = n_seq * tt
    n_levels = _log2(c)
    n_groups = rows // GROUP_ROWS
    scratch = [
        pltpu.VMEM((n_seq, N_PAIRS, STACK, LANES), F32),
        pltpu.VMEM((n_seq, N_PAIRS, LANES, LANES), F32),
        pltpu.VMEM((n_seq, HIST + tt, CONV_A_WIDTH), F32),
        pltpu.VMEM((n_seq, HIST + tt, W_B), F32),
        pltpu.VMEM((n_seq, W_B), F32),
        pltpu.VMEM((rows, D_MIX), BF16),
        pltpu.VMEM((rows, D_MODEL), BF16),
        pltpu.VMEM((rows, QK_A), F32),
        pltpu.VMEM((rows, QK_A), F32),
        pltpu.VMEM((rows, W_A), F32),
        pltpu.VMEM((rows, LANES), F32),
        pltpu.VMEM((rows, LANES), F32),
        pltpu.VMEM((rows, LANES), F32),
        pltpu.VMEM((rows, QKC_PAD), F32),
        pltpu.VMEM((rows, QKC_PAD), F32),
        pltpu.VMEM((rows, W_C), F32),
        pltpu.VMEM((rows, QKC_PAD), F32),
        pltpu.VMEM((rows, QKC_PAD), F32),
        pltpu.VMEM((rows, QKC_PAD), F32),
        pltpu.VMEM((n_levels, rows, QKC_PAD), F32),
        pltpu.VMEM((n_levels, rows, QKC_PAD), F32),
        pltpu.VMEM((n_groups, N_PAIRS, STACK, STACK), F32),
        pltpu.VMEM((n_groups, N_PAIRS, STACK, STACK), BF16),
        pltpu.VMEM((n_groups, N_PAIRS, STACK, 2 * LANES), BF16),
        pltpu.VMEM((n_groups, N_PAIRS, STACK, LANES), BF16),
        pltpu.VMEM((n_groups, N_PAIRS, STACK, STACK), BF16),
        pltpu.VMEM((n_groups, N_PAIRS, STACK, LANES), F32),
        pltpu.VMEM((n_groups, N_PAIRS, STACK, LANES), F32),
        pltpu.VMEM((n_groups, N_PAIRS, STACK, LANES), F32),
        pltpu.VMEM((n_groups, N_PAIRS, STACK, LANES), BF16),
        pltpu.VMEM((n_groups, N_PAIRS, GROUP_ROWS // c, LANES, LANES), F32),
        pltpu.VMEM((n_groups, N_PAIRS, GROUP_ROWS // c, LANES, LANES), BF16),
        pltpu.VMEM((n_groups, N_PAIRS, STACK, LANES), F32),
        pltpu.VMEM((n_groups, N_PAIRS, STACK, LANES), BF16),
        pltpu.VMEM((n_groups, N_PAIRS, GROUP_ROWS // c, LANES, LANES), F32),
    ]
    assert len(scratch) == N_MIXER_SCRATCH
    return scratch


def _ffn_stage(stage, x, state_ref, weight_refs, y_ref, tail_ref, scratch_refs, *, n_seq, tt):
    (w_up_ref, vec_ref, wd_ref) = weight_refs
    (gbuf_s, h_s, xb_s) = scratch_refs
    rows = n_seq * tt
    tail = slice(HIST - (CONV_F - 1), HIST)

    if stage == 'init':
        if state_ref is not None:
            gbuf_s[:, tail, :] = state_ref[...]
        else:
            gbuf_s[:, tail, :] = jnp.zeros((n_seq, CONV_F - 1, D_FF), F32)
        return
    if stage == 'final':
        tail_ref[...] = gbuf_s[:, tt + HIST - (CONV_F - 1):tt + HIST, :]
        return

    xb_s[...] = x.astype(BF16)
    for j in range(N_FF_CHUNKS):
        cols = slice(j * FF_CHUNK, (j + 1) * FF_CHUNK)
        gate = _bdot(xb_s[...], w_up_ref[:, cols])
        val = _bdot(xb_s[...], w_up_ref[:, D_FF + j * FF_CHUNK:D_FF + (j + 1) * FF_CHUNK])
        gbuf_s[:, HIST:, cols] = gate.reshape(n_seq, tt, FF_CHUNK)
        conv = _conv_from_buffer(gbuf_s, vec_ref, ROW_CONV_F, CONV_F, tt, cols).reshape(rows, FF_CHUNK)
        h = _gelu_tanh(conv + vec_ref[ROW_CONV_F_BIAS:ROW_CONV_F_BIAS + 1, cols]) * val
        h_s[:, cols] = h.astype(BF16)
    gbuf_s[:, tail, :] = gbuf_s[:, tt + HIST - (CONV_F - 1):tt + HIST, :]
    y = ALPHA * x + _bdot(h_s[...], wd_ref[...])
    ln_g = vec_ref[ROW_LN2_G:ROW_LN2_G + 1, 0:D_MODEL]
    ln_b = vec_ref[ROW_LN2_B:ROW_LN2_B + 1, 0:D_MODEL]
    y_ref[...] = _layer_norm(y, ln_g, ln_b).reshape(n_seq, tt, D_MODEL)


def _mixer_kernel(*refs, n_seq, tt, c, has_state, n_alias):
    n_state = N_MIXER_OUTS if has_state else 0
    k = 1 + n_state
    x_ref, state_refs = refs[0], refs[1:k]
    weight_refs = refs[k:k + N_MIXER_WEIGHTS]
    k += N_MIXER_WEIGHTS + n_alias
    x1_ref = refs[k]
    out_refs = refs[k + 1:k + 1 + N_MIXER_OUTS]
    scratch_refs = refs[k + 1 + N_MIXER_OUTS:]
    t_idx = pl.program_id(1)

    def write_x1(val):
        x1_ref[...] = val.reshape(n_seq, tt, D_MODEL)

    stage = functools.partial(_mixer_stage, x_ref=x_ref, state_refs=state_refs, weight_refs=weight_refs,
                              out_refs=out_refs, scratch_refs=scratch_refs, write_x1=write_x1,
                              n_seq=n_seq, tt=tt, c=c)
    pl.when(t_idx == 0)(lambda: stage('init'))
    stage('body')
    pl.when(t_idx == pl.num_programs(1) - 1)(lambda: stage('final'))


def _ffn_kernel(*refs, n_seq, tt, has_state, n_alias):
    k = 2 if has_state else 1
    x_ref = refs[0]
    state_ref = refs[1] if has_state else None
    weight_refs = refs[k:k + 3]
    k += 3 + n_alias
    y_ref, tail_ref = refs[k:k + 2]
    scratch_refs = refs[k + 2:]
    t_idx = pl.program_id(1)
    stage = functools.partial(_ffn_stage, state_ref=state_ref, weight_refs=weight_refs, y_ref=y_ref,
                              tail_ref=tail_ref, scratch_refs=scratch_refs, n_seq=n_seq, tt=tt)
    pl.when(t_idx == 0)(lambda: stage('init', None))
    stage('body', x_ref[...].reshape(n_seq * tt, D_MODEL))
    pl.when(t_idx == pl.num_programs(1) - 1)(lambda: stage('final', None))


def _block_call(kern, name, x, states, state_shapes, weights, scratch, layer, prev_outs, *, n_seq, tt):
    nb, t_total, _ = x.shape
    assert nb % n_seq == 0 and t_total % tt == 0
    x_spec = pl.BlockSpec((n_seq, tt, D_MODEL), lambda b, t: (b, t, 0))
    in_specs, operands = [x_spec], [x]
    if states is not None:
        for arr, tail in zip(states, state_shapes):
            assert arr.shape[2:] == tail
            in_specs.append(pl.BlockSpec((None, n_seq) + tail, lambda b, t, z=(0,) * len(tail): (layer, b) + z))
            operands.append(arr)
    for a, per_layer in weights:
        if per_layer:
            in_specs.append(pl.BlockSpec((None,) + a.shape[1:], lambda b, t, z=(0,) * (a.ndim - 1): (layer,) + z))
        else:
            in_specs.append(pl.BlockSpec(a.shape, lambda b, t, z=(0,) * a.ndim: z))
        operands.append(a)
    if states is not None:
        assert prev_outs is None
        aliases = {1 + i: 1 + i for i in range(len(states))}
        n_alias = 0
    else:
        aliases = {}
        for i, arr in enumerate(prev_outs):
            aliases[len(operands)] = 1 + i
            in_specs.append(pl.BlockSpec(memory_space=pl.ANY))
            operands.append(arr)
        n_alias = len(prev_outs)
    out_shape = [jax.ShapeDtypeStruct((nb, t_total, D_MODEL), F32)]
    out_specs = [x_spec]
    for tail in state_shapes:
        out_shape.append(jax.ShapeDtypeStruct((DEPTH, nb) + tail, F32))
        out_specs.append(pl.BlockSpec((None, n_seq) + tail, lambda b, t, z=(0,) * len(tail): (layer, b) + z))
    return pl.pallas_call(
        functools.partial(kern, n_alias=n_alias),
        grid=(nb // n_seq, t_total // tt),
        in_specs=in_specs,
        out_specs=out_specs,
        out_shape=out_shape,
        input_output_aliases=aliases,
        scratch_shapes=scratch,
        compiler_params=pltpu.CompilerParams(
            dimension_semantics=("arbitrary", "arbitrary"), vmem_limit_bytes=VMEM_LIMIT_BYTES),
        name=f"{name}_nseq{n_seq}_tt{tt}",
    )(*operands)


def _mixer_call(x, states, w, layer, prev_outs, *, n_seq, tt, c):
    assert (tt % MASK_ROWS == 0 and c == GROUP_ROWS) or (tt == HIST and c == HIST and n_seq * tt <= MASK_ROWS)
    assert (n_seq * tt) % GROUP_ROWS == 0
    state_shapes = MIXER_STATE_SHAPES
    weights = [(_tile_masks(min(n_seq * tt, MASK_ROWS), c), False)] + [
        (w[k], True) for k in ('w_in', 'mix_vec', 'wr', 'wi', 'w2', 'w_out')]
    kern = functools.partial(_mixer_kernel, n_seq=n_seq, tt=tt, c=c, has_state=states is not None)
    return _block_call(kern, 'mixer', x, states, state_shapes, weights, _mixer_scratch(n_seq, tt, c), layer,
                       prev_outs, n_seq=n_seq, tt=tt)


def _ffn_call(x, state, w, layer, prev_out, *, n_seq, tt):
    weights = [(w[k], True) for k in ('w_up', 'ffn_vec', 'w_down')]
    scratch = [pltpu.VMEM((n_seq, HIST + tt, D_FF), F32),
               pltpu.VMEM((n_seq * tt, D_FF), BF16),
               pltpu.VMEM((n_seq * tt, D_MODEL), BF16)]
    kern = functools.partial(_ffn_kernel, n_seq=n_seq, tt=tt, has_state=state is not None)
    return _block_call(kern, 'ffn', x, None if state is None else (state,), (FFN_STATE_SHAPE,), weights,
                       scratch, layer, None if prev_out is None else (prev_out,), n_seq=n_seq, tt=tt)


def _vec_table(pieces, width, n_rows):
    rows = []
    used = 0
    for arr, lane0 in pieces:
        arr = arr.astype(F32)
        rows.append(jnp.pad(arr, ((0, 0), (0, 0), (lane0, width - lane0 - arr.shape[2]))))
        used += arr.shape[1]
    depth = pieces[0][0].shape[0]
    rows.append(jnp.zeros((depth, n_rows - used, width), F32))
    return jnp.concatenate(rows, axis=1)


def _prep_weights(w_in, conv_a_w, a_log, dt_bias, norm_a_w, conv_b_w, conv_b_b, lru_w_r, lru_b_r, lru_w_i,
                  lru_b_i, lru_lambda, gla_w2, gla_b2, norm_c_w, w_out, ln1_g, ln1_b, ffn_w_up, ffn_conv_w,
                  ffn_conv_b, ffn_w_down, ln2_g, ln2_b):
    depth, d, _ = w_in.shape
    pts = [0]
    for s in (QK_A, QK_A, W_A, W_A, H_A, H_A, W_B, W_B, QK_C, QK_C, W_C, W_C, GLA_RANK):
        pts.append(pts[-1] + s)
    (qa, ka, va, za, ba, aa, xb, gb, qc, kc, vc, zc, lc) = [w_in[:, :, pts[i]:pts[i + 1]] for i in range(13)]
    z = lambda n: jnp.zeros((depth, d, n), w_in.dtype)
    small = jnp.concatenate([ba, z(LANE_DECAY - H_A), aa, z(LANE_LOWRANK - LANE_DECAY - H_A), lc,
                             z(LANES - LANE_LOWRANK - GLA_RANK)], axis=2)
    w_in_p = jnp.concatenate([qa, ka, va, za, xb, gb, qc, z(QKC_PAD - QK_C), kc, z(QKC_PAD - QK_C), vc, zc, small],
                             axis=2).astype(BF16)
    assert w_in_p.shape[2] == D_IN_PAD

    eye_blocks = jnp.eye(LRU_BLOCKS, dtype=F32)[None, :, None, :, None]

    def block_diag(wb):
        return (wb[:, :, :, None, :] * eye_blocks).reshape(depth, W_B, W_B).astype(BF16)

    w2 = jnp.pad(gla_w2, ((0, 0), (LANE_LOWRANK, LANES - LANE_LOWRANK - GLA_RANK), (0, QKC_PAD - QK_C))).astype(BF16)
    r1 = lambda v: v[:, None, :]
    dup = lambda v: jnp.concatenate([v, v], axis=1)[:, None, :]
    mix_vec = _vec_table(
        [(conv_a_w, 0), (conv_b_w, 0), (r1(conv_b_b), 0), (r1(a_log), LANE_DECAY), (r1(dt_bias), LANE_DECAY),
         (dup(norm_a_w), 0), (r1(lru_b_r), 0), (r1(lru_b_i), 0), (r1(lru_lambda), 0), (r1(gla_b2), 0),
         (dup(norm_c_w), 0), (r1(ln1_g), 0), (r1(ln1_b), 0)], VEC_W, MIX_VEC_ROWS)
    ffn_vec = _vec_table([(ffn_conv_w, 0), (r1(ffn_conv_b), 0), (r1(ln2_g), 0), (r1(ln2_b), 0)], D_FF, FFN_VEC_ROWS)
    return {
        'w_in': w_in_p, 'mix_vec': mix_vec, 'wr': block_diag(lru_w_r), 'wi': block_diag(lru_w_i), 'w2': w2,
        'w_out': w_out.astype(BF16), 'w_up': ffn_w_up.astype(BF16), 'ffn_vec': ffn_vec,
        'w_down': ffn_w_down.astype(BF16),
    }


def _trunk(x, states, w, *, n_seq, tt, c):
    nb = x.shape[0]
    if states is not None:
        st_dconv, st_delta, st_lconv, st_lru, st_gla, st_fconv = (s.astype(F32) for s in states)
        states = (st_dconv, st_delta.reshape(DEPTH, nb, N_PAIRS, 2 * DK_A, DV_A), st_lconv,
                  st_lru.reshape(DEPTH, nb, 1, W_B), st_gla.reshape(DEPTH, nb, QK_C, DV_C), st_fconv)
    has_state = states is not None
    if has_state:
        mix_outs, ffn_out = list(states[:5]), states[5]
    else:
        mix_outs = [jnp.zeros((DEPTH, nb) + tail, F32) for tail in MIXER_STATE_SHAPES]
        ffn_out = jnp.zeros((DEPTH, nb) + FFN_STATE_SHAPE, F32)
    for l in range(DEPTH):
        x1, *mix_outs = _mixer_call(x, mix_outs if has_state else None, w, l, None if has_state else mix_outs,
                                    n_seq=n_seq, tt=tt, c=c)
        if n_seq > 1:
            ffn_tile = dict(n_seq=math.gcd(nb, max(n_seq, FFN_TILE_ROWS // tt)), tt=tt)
        else:
            ffn_tile = dict(n_seq=1, tt=min(FFN_TILE_ROWS, x.shape[1]))
        x, ffn_out = _ffn_call(x1, ffn_out if has_state else None, w, l, None if has_state else ffn_out,
                               **ffn_tile)
    dconv, sd, lconv, h, sg = mix_outs
    return x, (dconv, sd.reshape(DEPTH, nb, H_A, DK_A, DV_A), lconv, h.reshape(DEPTH, nb, W_B),
               sg.reshape(DEPTH, nb, H_C, DK_C, DV_C), ffn_out)


def kernel(x_prompt, x_sample, state_delta_conv, state_delta, state_lru_conv, state_lru, state_gla, state_ffn_conv,
           w_in, conv_a_w, a_log, dt_bias, norm_a_w, conv_b_w, conv_b_b, lru_w_r, lru_b_r, lru_w_i, lru_b_i,
           lru_lambda, gla_w2, gla_b2, norm_c_w, w_out, ln1_g, ln1_b, ffn_w_up, ffn_conv_w, ffn_conv_b, ffn_w_down,
           ln2_g, ln2_b):
    assert w_in.shape[0] == DEPTH
    w = _prep_weights(w_in, conv_a_w, a_log, dt_bias, norm_a_w, conv_b_w, conv_b_b, lru_w_r, lru_b_r, lru_w_i,
                      lru_b_i, lru_lambda, gla_w2, gla_b2, norm_c_w, w_out, ln1_g, ln1_b, ffn_w_up, ffn_conv_w,
                      ffn_conv_b, ffn_w_down, ln2_g, ln2_b)
    sample_states = (state_delta_conv, state_delta, state_lru_conv, state_lru, state_gla, state_ffn_conv)
    t_p = x_prompt.shape[1]
    t_s = x_sample.shape[1]
    n_p = PROMPT_NSEQ if x_prompt.shape[0] % PROMPT_NSEQ == 0 else 1
    y_p, p_st = _trunk(x_prompt, None, w, n_seq=n_p, tt=min(PROMPT_TT, t_p), c=CHUNK)
    y_s, s_st = _trunk(x_sample, sample_states, w, n_seq=min(SAMPLE_NSEQ, x_sample.shape[0]), tt=t_s, c=t_s)
    return (y_p, y_s) + p_st + s_st
```

```python
import functools
import math

import numpy as np
import jax
import jax.numpy as jnp
from jax import lax
from jax.experimental import pallas as pl
from jax.experimental.pallas import tpu as pltpu

F32 = jnp.float32
BF16 = jnp.bfloat16

D_MODEL = 1024
H_A, DK_A, DV_A = 6, 64, 64
QK_A = H_A * DK_A
W_A = H_A * DV_A
CONV_A = 4
W_B = 256
LRU_BLOCKS = 4
LRU_BLOCK = W_B // LRU_BLOCKS
CONV_B = 4
LRU_C = 8.0
H_C, DK_C, DV_C = 6, 32, 64
QK_C = H_C * DK_C
W_C = H_C * DV_C
GLA_RANK = 16
GLA_TAU = 16.0
D_MIX = W_A + W_B + W_C
CHUNK = 64
D_FF = 2816
CONV_F = 3
EPS = 1e-6
CONV_A_WIDTH = 2 * QK_A + W_A
DEPTH = 2
ALPHA = (2.0 * DEPTH) ** 0.25

LANES = 128
SUBLANES = 8
HALF = LANES // 2
GROUP_ROWS = 64
STACK = 2 * GROUP_ROWS
N_PAIRS = 3
HIST = SUBLANES
FF_CHUNK = 256
N_FF_CHUNKS = D_FF // FF_CHUNK
VMEM_LIMIT_BYTES = 56 * 1024 * 1024
PROMPT_TT = 512
PROMPT_NSEQ = 1
MASK_ROWS = 256
FFN_TILE_ROWS = 512
SAMPLE_NSEQ = 16

OFF_QA = 0
OFF_KA = OFF_QA + QK_A
OFF_VA = OFF_KA + QK_A
OFF_ZA = OFF_VA + W_A
OFF_XB = OFF_ZA + W_A
OFF_GB = OFF_XB + W_B
OFF_QC = OFF_GB + W_B
QKC_PAD = 2 * LANES
OFF_KC = OFF_QC + QKC_PAD
OFF_VC = OFF_KC + QKC_PAD
OFF_ZC = OFF_VC + W_C
OFF_SMALL = OFF_ZC + W_C
D_IN_PAD = OFF_SMALL + LANES
LANE_BETA = 0
LANE_DECAY = 8
LANE_LOWRANK = 16

VEC_W = CONV_A_WIDTH
(ROW_CONV_A, ROW_CONV_B, ROW_CONV_B_BIAS, ROW_ALOG, ROW_DTB, ROW_NORM_A, ROW_BR, ROW_BI, ROW_LAM, ROW_B2,
 ROW_NORM_C, ROW_LN1_G, ROW_LN1_B) = (0, 4, 8, 9, 10, 11, 12, 13, 14, 15, 16, 17, 18)
MIX_VEC_ROWS = 24
(ROW_CONV_F, ROW_CONV_F_BIAS, ROW_LN2_G, ROW_LN2_B) = (0, 3, 4, 5)
FFN_VEC_ROWS = 8


def _bdot(a, b):
    return jnp.dot(a, b, preferred_element_type=F32)


def _mm(a, b):
    return _bdot(a.astype(BF16), b.astype(BF16))


def _mm_nt(a, b):
    return lax.dot_general(a.astype(BF16), b.astype(BF16), (((1,), (1,)), ((), ())), preferred_element_type=F32)


def _split(a, n):
    parts = []
    r = a
    for i in range(n):
        p = r.astype(BF16)
        parts.append(p)
        if i + 1 < n:
            r = r - p.astype(F32)
    return parts


def _mm_mask_lhs(mask_bf16, a, n):
    out = None
    for p in _split(a, n):
        t = _bdot(mask_bf16, p)
        out = t if out is None else out + t
    return out


def _softplus(x):
    return jnp.maximum(x, 0.0) + jnp.log1p(jnp.exp(-jnp.abs(x)))


def _sigmoid(x):
    return 0.5 * jnp.tanh(0.5 * x) + 0.5


def _silu(x):
    h = 0.5 * x
    return h * jnp.tanh(h) + h


def _gelu_tanh(x):
    return 0.5 * x * (1.0 + jnp.tanh(math.sqrt(2.0 / math.pi) * (x + 0.044715 * (x * x * x))))


def _layer_norm(x, g, b):
    mu = jnp.mean(x, axis=-1, keepdims=True)
    xc = x - mu
    var = jnp.mean(xc * xc, axis=-1, keepdims=True)
    return xc * lax.rsqrt(var + EPS) * g + b


def _iota(shape, axis):
    return lax.broadcasted_iota(jnp.int32, shape, axis)


def _log2(n):
    l = int(math.log2(n))
    assert (1 << l) == n
    return l


def _conv_from_buffer(buf_ref, vec_ref, row0, width, tt, cols):
    y = None
    for d in range(width):
        tap = vec_ref[row0 + width - 1 - d:row0 + width - d, cols]
        term = buf_ref[:, HIST - d:HIST - d + tt, cols] * tap
        y = term if y is None else y + term
    return y


def _stack_rows(n_sb, c, pieces):
    if n_sb == 1:
        return jnp.concatenate([pieces[0], pieces[1]], axis=0)
    out = []
    for s in range(n_sb):
        for a in range(2):
            out.append(pieces[a][s * c:(s + 1) * c])
    return jnp.concatenate(out, axis=0)


def _unstack_rows(o, n_sb, c):
    lo = _iota((c, LANES), 1) < HALF
    out = []
    for s in range(n_sb):
        base = s * 2 * c
        out.append(jnp.where(lo, o[base:base + c], o[base + c:base + 2 * c]))
    return out[0] if n_sb == 1 else jnp.concatenate(out, axis=0)


def _col_bcast_stack(vals, lane0, n_sb, c):
    pieces = [jnp.broadcast_to(vals[:, lane0 + a:lane0 + a + 1], (GROUP_ROWS, LANES)) for a in range(2)]
    return _stack_rows(n_sb, c, pieces)


def _dup_values(v):
    lo = _iota(v.shape, 1) < HALF
    vr = pltpu.roll(v, HALF, axis=1)
    return jnp.where(lo, v, vr), jnp.where(lo, vr, v)


def _tile_masks(rows, c):
    lc = _log2(c)
    i = np.arange(rows)[:, None]
    j = np.arange(rows)[None, :]
    same = (i >> lc) == (j >> lc)
    out = [same & (j <= i), same]
    for lvl in range(lc):
        lh = lc - 1 - lvl
        hi = (i >> lh) & 1
        start = (i >> lh) << lh
        nxt = ((i >> lh) + 1) << lh
        out.append((hi == 1) & (j >= start) & (j <= i))
        out.append((hi == 0) & (j > i) & (j < nxt))
    return jnp.asarray(np.stack(out).astype(np.float32), dtype=BF16)


MIXER_STATE_SHAPES = ((CONV_A - 1, CONV_A_WIDTH), (N_PAIRS, 2 * DK_A, DV_A), (CONV_B - 1, W_B), (1, W_B),
                      (QK_C, DV_C))
FFN_STATE_SHAPE = (CONV_F - 1, D_FF)
N_MIXER_WEIGHTS = 7
N_MIXER_OUTS = 5
N_MIXER_SCRATCH = 35


def _mixer_stage(stage, x_ref, state_refs, weight_refs, out_refs, scratch_refs, write_x1, *, n_seq, tt, c):
    has_state = len(state_refs) > 0
    (masks_ref, w_in_ref, vec_ref, wr_ref, wi_ref, w2_ref, w_out_ref) = weight_refs
    (dconv_out_ref, sd_out_ref, lconv_out_ref, h_out_ref, sg_out_ref) = out_refs
    (sd_s, sg_s, abuf_s, bbuf_s, h_s, heads_s, xb_s,
     qa_s, ka_s, va_s, beta_s, gcum_s, gtot_s,
     qc_s, kc_s, vc_s, qb_s, kdec_s, btot_s, qt_s, kt_s,
     t_s, p_s, rhs_s, qe_s, qkd_s, kdst_s, gtst_s, oprime_s, qprime_s, c_s, kw_s, av_s, qbst_s, gk_s,
     ) = scratch_refs

    rows = n_seq * tt
    n_groups = rows // GROUP_ROWS
    n_sb = GROUP_ROWS // c
    lc = _log2(c)
    n_levels = lc
    a_tail = slice(HIST - (CONV_A - 1), HIST)
    b_tail = slice(HIST - (CONV_B - 1), HIST)

    if stage == 'init':
        if has_state:
            dconv_ref, sd_ref, lconv_ref, h0_ref, sg_ref = state_refs
            abuf_s[:, a_tail, :] = dconv_ref[...]
            bbuf_s[:, b_tail, :] = lconv_ref[...]
            h_s[...] = h0_ref[...].reshape(n_seq, W_B)
            zeros = jnp.zeros((GROUP_ROWS, LANES), F32)
            for s in range(n_seq):
                for p in range(N_PAIRS):
                    sd = sd_ref[s, p]
                    sd_s[s, p] = jnp.concatenate([sd, sd], axis=-1)
                    sg = sg_ref[s, 2 * DK_C * p:2 * DK_C * (p + 1), :]
                    sg = jnp.concatenate([sg, sg], axis=-1)
                    sg_s[s, p] = jnp.concatenate([sg, zeros] if p % 2 == 0 else [zeros, sg], axis=0)
        else:
            abuf_s[:, a_tail, :] = jnp.zeros((n_seq, CONV_A - 1, CONV_A_WIDTH), F32)
            bbuf_s[:, b_tail, :] = jnp.zeros((n_seq, CONV_B - 1, W_B), F32)
            h_s[...] = jnp.zeros((n_seq, W_B), F32)
            sd_s[...] = jnp.zeros(sd_s.shape, F32)
            sg_s[...] = jnp.zeros(sg_s.shape, F32)
        return

    if stage == 'final':
        dconv_out_ref[...] = abuf_s[:, tt + HIST - (CONV_A - 1):tt + HIST, :]
        lconv_out_ref[...] = bbuf_s[:, tt + HIST - (CONV_B - 1):tt + HIST, :]
        h_out_ref[...] = h_s[...].reshape(n_seq, 1, W_B)
        for s in range(n_seq):
            for p in range(N_PAIRS):
                sd_out_ref[s, p] = sd_s[s, p][:, 0:HALF]
                r0 = GROUP_ROWS * (p % 2)
                sg_out_ref[s, 2 * DK_C * p:2 * DK_C * (p + 1), :] = sg_s[s, p][r0:r0 + GROUP_ROWS, 0:HALF]
        return

    def vec(row, width):
        return vec_ref[row:row + 1, 0:width]

    x = x_ref[...].reshape(rows, D_MODEL)
    xb_s[...] = x.astype(BF16)
    proj = _bdot(xb_s[...], w_in_ref[...])
    m_cum = masks_ref[0]
    m_tot = masks_ref[1]
    mask_rows = masks_ref.shape[1]

    def masked_sum(mask, a, n):
        out = [_mm_mask_lhs(mask, a[r:r + mask_rows], n) for r in range(0, rows, mask_rows)]
        return out[0] if len(out) == 1 else jnp.concatenate(out, axis=0)

    abuf_s[:, HIST:, :] = proj[:, OFF_QA:OFF_QA + CONV_A_WIDTH].reshape(n_seq, tt, CONV_A_WIDTH)
    conv_a = _conv_from_buffer(abuf_s, vec_ref, ROW_CONV_A, CONV_A, tt, slice(0, CONV_A_WIDTH))
    qkv = _silu(conv_a.reshape(rows, CONV_A_WIDTH))
    abuf_s[:, a_tail, :] = abuf_s[:, tt + HIST - (CONV_A - 1):tt + HIST, :]
    p_i = _iota((LANES, LANES), 0) >> _log2(DK_A)
    p_j = _iota((LANES, LANES), 1) >> _log2(DK_A)
    pair_ones = jnp.where(p_i == p_j, 1.0, 0.0).astype(BF16)
    for p in range(N_PAIRS):
        for off, dst, scale in ((0, qa_s, DK_A ** -0.5), (QK_A, ka_s, 1.0)):
            v = qkv[:, off + p * LANES:off + (p + 1) * LANES]
            ssq = _bdot((v * v).astype(BF16), pair_ones)
            dst[:, p * LANES:(p + 1) * LANES] = v * (lax.rsqrt(ssq + EPS) * scale)
    va_s[...] = qkv[:, 2 * QK_A:]

    small = proj[:, OFF_SMALL:OFF_SMALL + LANES]
    beta_s[...] = _sigmoid(small)
    g_full = -jnp.exp(vec(ROW_ALOG, LANES)) * _softplus(small + vec(ROW_DTB, LANES))
    gcum_s[...] = masked_sum(m_cum, g_full, 2)
    gtot_s[...] = masked_sum(m_tot, g_full, 2)

    def mixer_b():
        bbuf_s[:, HIST:, :] = proj[:, OFF_XB:OFF_XB + W_B].reshape(n_seq, tt, W_B)
        xc = (_conv_from_buffer(bbuf_s, vec_ref, ROW_CONV_B, CONV_B, tt, slice(0, W_B)).reshape(rows, W_B)
              + vec(ROW_CONV_B_BIAS, W_B))
        bbuf_s[:, b_tail, :] = bbuf_s[:, tt + HIST - (CONV_B - 1):tt + HIST, :]
        gate_r = _sigmoid(_mm(xc, wr_ref[...]) + vec(ROW_BR, W_B))
        gate_i = _sigmoid(_mm(xc, wi_ref[...]) + vec(ROW_BI, W_B))
        log_a = -LRU_C * gate_r * _softplus(-vec(ROW_LAM, W_B))
        a_t = jnp.exp(log_a)
        b_t = jnp.sqrt(-jnp.tanh(log_a) * (a_t * a_t + 1.0)) * (gate_i * xc)
        i_seq = _iota((rows, W_B), 0) & (tt - 1)
        h_prev = jnp.concatenate([jnp.broadcast_to(h_s[s:s + 1, :], (tt, W_B)) for s in range(n_seq)], axis=0)
        b_t = b_t + jnp.where(i_seq == 0, a_t * h_prev, 0.0)
        d = 1
        while d < tt:
            a_sh = pltpu.roll(a_t, d, axis=0)
            b_sh = pltpu.roll(b_t, d, axis=0)
            ok = i_seq >= d
            b_t = jnp.where(ok, a_t * b_sh + b_t, b_t)
            a_t = jnp.where(ok, a_t * a_sh, a_t)
            d *= 2
        h_s[...] = jnp.concatenate([b_t[(s + 1) * tt - 1:(s + 1) * tt] for s in range(n_seq)], axis=0)
        heads_s[:, W_A:W_A + W_B] = (b_t * _gelu_tanh(proj[:, OFF_GB:OFF_GB + W_B])).astype(BF16)

    qc = proj[:, OFF_QC:OFF_QC + QKC_PAD] * (DK_C ** -0.5)
    kc = proj[:, OFF_KC:OFF_KC + QKC_PAD]
    logf = -_softplus(-(_mm(small, w2_ref[...]) + vec(ROW_B2, QKC_PAD))) * (1.0 / GLA_TAU)
    b_cum = masked_sum(m_cum, logf, 2)
    b_tot = masked_sum(m_tot, logf, 2)
    qc_s[...] = qc
    kc_s[...] = kc
    vc_s[...] = proj[:, OFF_VC:OFF_VC + W_C]
    qb_s[...] = qc * jnp.exp(b_cum)
    kdec_s[...] = kc * jnp.exp(b_tot - b_cum)
    btot_s[...] = b_tot
    for lvl in range(n_levels):
        qt_s[lvl] = qc * jnp.exp(masked_sum(masks_ref[2 + 2 * lvl], logf, 1))
        kt_s[lvl] = kc * jnp.exp(masked_sum(masks_ref[3 + 2 * lvl], logf, 1))

    s_i = _iota((STACK, STACK), 0)
    s_j = _iota((STACK, STACK), 1)
    same_blk = (s_i >> lc) == (s_j >> lc)
    m_incl = same_blk & (s_j <= s_i)
    m_strict = same_blk & (s_j < s_i)
    eye = jnp.where(s_i == s_j, 1.0, 0.0)
    lane = _iota((GROUP_ROWS, LANES), 1)
    m0 = lane < HALF
    norm_a = vec(ROW_NORM_A, LANES)
    norm_c = vec(ROW_NORM_C, LANES)

    def delta_setup(gi):
        rs = pl.ds(gi * GROUP_ROWS, GROUP_ROWS)
        beta_g = beta_s[rs, :]
        gcum_g = gcum_s[rs, :]
        gtot_g = gtot_s[rs, :]
        for p in range(N_PAIRS):
            sl = pl.ds(p * LANES, LANES)
            q_p = qa_s[rs, sl]
            k_p = ka_s[rs, sl]
            v0, v1 = _dup_values(va_s[rs, sl])
            q_st = _stack_rows(n_sb, c, [jnp.where(m0, q_p, 0.0), jnp.where(m0, 0.0, q_p)])
            k_st = _stack_rows(n_sb, c, [jnp.where(m0, k_p, 0.0), jnp.where(m0, 0.0, k_p)])
            v_st = _stack_rows(n_sb, c, [v0, v1])
            beta_c = _col_bcast_stack(beta_g, LANE_BETA + 2 * p, n_sb, c)
            g_c = _col_bcast_stack(gcum_g, LANE_DECAY + 2 * p, n_sb, c)
            gt_c = _col_bcast_stack(gtot_g, LANE_DECAY + 2 * p, n_sb, c)
            dec = jnp.exp(jnp.where(m_incl, g_c - g_c.T, -1e30))
            kq = _mm_nt(jnp.concatenate([k_st, q_st], axis=0), k_st)
            kk, qk = kq[0:STACK], kq[STACK:]
            n_mat = -(beta_c * kk * jnp.where(m_strict, dec, 0.0))
            eg = jnp.exp(g_c)
            t_s[gi, p] = eye + n_mat
            p_s[gi, p] = n_mat.astype(BF16)
            rhs_s[gi, p] = jnp.concatenate([beta_c * v_st, beta_c * eg * k_st], axis=1).astype(BF16)
            qe_s[gi, p] = (eg * q_st).astype(BF16)
            qkd_s[gi, p] = (qk * dec).astype(BF16)
            kdst_s[gi, p] = k_st * jnp.exp(gt_c - g_c)
            gtst_s[gi, p] = gt_c

    def inverse_level(gi, j):
        for p in range(N_PAIRS):
            pw = p_s[gi, p]
            if j >= 2:
                t_old = t_s[gi, p]
                both = _bdot(jnp.concatenate([pw, t_old.astype(BF16)], axis=0), pw)
                t_s[gi, p] = t_old + both[STACK:]
                p_s[gi, p] = both[0:STACK].astype(BF16)
            else:
                p_s[gi, p] = _bdot(pw, pw).astype(BF16)

    def delta_solve(gi):
        for p in range(N_PAIRS):
            t_old = t_s[gi, p]
            t_fin = t_old + _bdot(t_old.astype(BF16), p_s[gi, p])
            uwk = _bdot(t_fin.astype(BF16), rhs_s[gi, p]).astype(BF16)
            k_dec = kdst_s[gi, p]
            if n_sb == 1:
                both = _bdot(jnp.concatenate([qkd_s[gi, p], k_dec.T.astype(BF16)], axis=0), uwk)
                o_qw, c_kw = both[0:STACK], [both[STACK:]]
            else:
                o_qw = _bdot(qkd_s[gi, p], uwk)
                c_kw = [_bdot(k_dec[sb * 2 * c:(sb + 1) * 2 * c].T.astype(BF16), uwk[sb * 2 * c:(sb + 1) * 2 * c])
                        for sb in range(n_sb)]
            oprime_s[gi, p] = o_qw[:, 0:LANES]
            qprime_s[gi, p] = (qe_s[gi, p].astype(F32) - o_qw[:, LANES:]).astype(BF16)
            for sb in range(n_sb):
                c_s[gi, p, sb] = c_kw[sb][:, 0:LANES]
                kw_s[gi, p, sb] = c_kw[sb][:, LANES:].astype(BF16)

    def gla_precompute(gi):
        rs = pl.ds(gi * GROUP_ROWS, GROUP_ROWS)
        for p in range(N_PAIRS):
            sl = pl.ds(p * LANES, LANES)
            slab = pl.ds((p // 2) * LANES, LANES)
            l0 = 2 * DK_C * (p % 2)
            mh0 = (lane >= l0) & (lane < l0 + DK_C)
            mh1 = (lane >= l0 + DK_C) & (lane < l0 + 2 * DK_C)

            def stack_c(ref, idx=None):
                val = ref[rs, slab] if idx is None else ref[idx, rs, slab]
                return _stack_rows(n_sb, c, [jnp.where(mh0, val, 0.0), jnp.where(mh1, val, 0.0)])

            def stack_k(ref, idx=None):
                val = ref[rs, slab] if idx is None else ref[idx, rs, slab]
                return _stack_rows(n_sb, c, [val, val])

            att = jnp.where(s_i == s_j, _mm_nt(stack_c(qc_s), stack_k(kc_s)), 0.0)
            for lvl in range(n_levels):
                lh = lc - 1 - lvl
                half = 1 << lh
                q_l = stack_c(qt_s, lvl)
                k_l = stack_k(kt_s, lvl)
                if half >= SUBLANES:
                    q_up = jnp.concatenate([q_l[b0 + half:b0 + 2 * half] for b0 in range(0, STACK, 2 * half)], axis=0)
                    prod = _mm_nt(q_up, k_l)
                    u_i = _iota(prod.shape, 0)
                    u_j = _iota(prod.shape, 1)
                    ok = ((u_i >> lh) == (u_j >> (lh + 1))) & (((u_j >> lh) & 1) == 0)
                    prod = jnp.where(ok, prod, 0.0)
                    zeros = jnp.zeros((half, STACK), F32)
                    pieces = []
                    for n_blk in range(STACK // (2 * half)):
                        pieces += [zeros, prod[n_blk * half:(n_blk + 1) * half]]
                    att = att + jnp.concatenate(pieces, axis=0)
                else:
                    valid = (((s_i >> (lh + 1)) == (s_j >> (lh + 1)))
                             & (((s_i >> lh) & 1) == 1) & (((s_j >> lh) & 1) == 0))
                    att = att + jnp.where(valid, _mm_nt(q_l, k_l), 0.0)
            vc0, vc1 = _dup_values(vc_s[rs, sl])
            vc_st = _stack_rows(n_sb, c, [vc0, vc1]).astype(BF16)
            qbst_s[gi, p] = stack_c(qb_s).astype(BF16)
            kd_st = stack_c(kdec_s)
            if n_sb == 1:
                both = _bdot(jnp.concatenate([att.astype(BF16), kd_st.T.astype(BF16)], axis=0), vc_st)
                av_s[gi, p] = both[0:STACK]
                gk_s[gi, p, 0] = both[STACK:]
            else:
                av_s[gi, p] = _bdot(att.astype(BF16), vc_st)
                for sb in range(n_sb):
                    blk = slice(sb * 2 * c, (sb + 1) * 2 * c)
                    gk_s[gi, p, sb] = _bdot(kd_st[blk].T.astype(BF16), vc_st[blk])

    for gi in range(n_groups):
        delta_setup(gi)
    for j in range(1, lc):
        for gi in range(n_groups):
            inverse_level(gi, j)
        if j <= n_groups:
            gla_precompute(j - 1)
        if j == 1:
            mixer_b()
    for gi in range(lc - 1, n_groups):
        gla_precompute(gi)
    for gi in range(n_groups):
        delta_solve(gi)

    def recurrence_body(gi):
        rs = pl.ds(gi * GROUP_ROWS, GROUP_ROWS)
        for p in range(N_PAIRS):
            sl = pl.ds(p * LANES, LANES)
            q_prime = qprime_s[gi, p]
            gt_c = gtst_s[gi, p]
            qs_parts = []
            for sb in range(n_sb):
                seq = (gi * GROUP_ROWS + sb * c) // tt
                blk = slice(sb * 2 * c, (sb + 1) * 2 * c)
                s_old = sd_s[seq, p]
                s_bf = s_old.astype(BF16)
                if n_sb == 1:
                    both = _bdot(jnp.concatenate([q_prime, kw_s[gi, p, 0]], axis=0), s_bf)
                    qs, kws = both[0:STACK], both[STACK:]
                else:
                    qs, kws = _bdot(q_prime[blk], s_bf), _bdot(kw_s[gi, p, sb], s_bf)
                qs_parts.append(qs)
                decay_rows = jnp.concatenate(
                    [jnp.broadcast_to(gt_c[sb * 2 * c + a * c:sb * 2 * c + a * c + 1, :], (GROUP_ROWS, LANES))
                     for a in range(2)], axis=0)
                sd_s[seq, p] = jnp.exp(decay_rows) * s_old + (c_s[gi, p, sb] - kws)
            qs_all = qs_parts[0] if n_sb == 1 else jnp.concatenate(qs_parts, axis=0)
            o_st = qs_all + oprime_s[gi, p]
            o_st = o_st * lax.rsqrt(jnp.mean(o_st * o_st, axis=-1, keepdims=True) + EPS) * norm_a
            z_a = proj[gi * GROUP_ROWS:(gi + 1) * GROUP_ROWS, OFF_ZA + p * LANES:OFF_ZA + (p + 1) * LANES]
            heads_s[rs, sl] = (_unstack_rows(o_st, n_sb, c) * _silu(z_a)).astype(BF16)
            slab = pl.ds((p // 2) * LANES, LANES)
            qb_st = qbst_s[gi, p]
            btot_t = btot_s[rs, slab].T
            oi_parts = []
            for sb in range(n_sb):
                seq = (gi * GROUP_ROWS + sb * c) // tt
                blk = slice(sb * 2 * c, (sb + 1) * 2 * c)
                s_old = sg_s[seq, p]
                oi_parts.append(_bdot(qb_st[blk], s_old.astype(BF16)))
                decay_col = jnp.broadcast_to(btot_t[:, sb * c:sb * c + 1], (LANES, LANES))
                sg_s[seq, p] = jnp.exp(decay_col) * s_old + gk_s[gi, p, sb]
            oi_all = oi_parts[0] if n_sb == 1 else jnp.concatenate(oi_parts, axis=0)
            oc_st = oi_all + av_s[gi, p]
            oc_st = oc_st * lax.rsqrt(jnp.mean(oc_st * oc_st, axis=-1, keepdims=True) + EPS) * norm_c
            z_c = proj[gi * GROUP_ROWS:(gi + 1) * GROUP_ROWS, OFF_ZC + p * LANES:OFF_ZC + (p + 1) * LANES]
            heads_s[rs, pl.ds(W_A + W_B + p * LANES, LANES)] = (
                _unstack_rows(oc_st, n_sb, c) * _silu(z_c)).astype(BF16)

    groups_per_seq = max(tt // GROUP_ROWS, 1)
    for k in range(groups_per_seq):
        for gi in range(k, n_groups, groups_per_seq):
            recurrence_body(gi)

    y = ALPHA * x + _bdot(heads_s[...], w_out_ref[...])
    write_x1(_layer_norm(y, vec(ROW_LN1_G, D_MODEL), vec(ROW_LN1_B, D_MODEL)))


def _mixer_scratch(n_seq, tt, c):
    rows = n_seq * tt
    n_levels = _log2(c)
    n_groups = rows // GROUP_ROWS
    scratch = [
        pltpu.VMEM((n_seq, N_PAIRS, STACK, LANES), F32),
        pltpu.VMEM((n_seq, N_PAIRS, LANES, LANES), F32),
        pltpu.VMEM((n_seq, HIST + tt, CONV_A_WIDTH), F32),
        pltpu.VMEM((n_seq, HIST + tt, W_B), F32),
        pltpu.VMEM((n_seq, W_B), F32),
        pltpu.VMEM((rows, D_MIX), BF16),
        pltpu.VMEM((rows, D_MODEL), BF16),
        pltpu.VMEM((rows, QK_A), F32),
        pltpu.VMEM((rows, QK_A), F32),
        pltpu.VMEM((rows, W_A), F32),
        pltpu.VMEM((rows, LANES), F32),
        pltpu.VMEM((rows, LANES), F32),
        pltpu.VMEM((rows, LANES), F32),
        pltpu.VMEM((rows, QKC_PAD), F32),
        pltpu.VMEM((rows, QKC_PAD), F32),
        pltpu.VMEM((rows, W_C), F32),
        pltpu.VMEM((rows, QKC_PAD), F32),
        pltpu.VMEM((rows, QKC_PAD), F32),
        pltpu.VMEM((rows, QKC_PAD), F32),
        pltpu.VMEM((n_levels, rows, QKC_PAD), F32),
        pltpu.VMEM((n_levels, rows, QKC_PAD), F32),
        pltpu.VMEM((n_groups, N_PAIRS, STACK, STACK), F32),
        pltpu.VMEM((n_groups, N_PAIRS, STACK, STACK), BF16),
        pltpu.VMEM((n_groups, N_PAIRS, STACK, 2 * LANES), BF16),
        pltpu.VMEM((n_groups, N_PAIRS, STACK, LANES), BF16),
        pltpu.VMEM((n_groups, N_PAIRS, STACK, STACK), BF16),
        pltpu.VMEM((n_groups, N_PAIRS, STACK, LANES), F32),
        pltpu.VMEM((n_groups, N_PAIRS, STACK, LANES), F32),
        pltpu.VMEM((n_groups, N_PAIRS, STACK, LANES), F32),
        pltpu.VMEM((n_groups, N_PAIRS, STACK, LANES), BF16),
        pltpu.VMEM((n_groups, N_PAIRS, GROUP_ROWS // c, LANES, LANES), F32),
        pltpu.VMEM((n_groups, N_PAIRS, GROUP_ROWS // c, LANES, LANES), BF16),
        pltpu.VMEM((n_groups, N_PAIRS, STACK, LANES), F32),
        pltpu.VMEM((n_groups, N_PAIRS, STACK, LANES), BF16),
        pltpu.VMEM((n_groups, N_PAIRS, GROUP_ROWS // c, LANES, LANES), F32),
    ]
    assert len(scratch) == N_MIXER_SCRATCH
    return scratch


def _ffn_stage(stage, x, state_ref, weight_refs, y_ref, tail_ref, scratch_refs, *, n_seq, tt):
    (w_up_ref, vec_ref, wd_ref) = weight_refs
    (gbuf_s, h_s, xb_s) = scratch_refs
    rows = n_seq * tt
    tail = slice(HIST - (CONV_F - 1), HIST)

    if stage == 'init':
        if state_ref is not None:
            gbuf_s[:, tail, :] = state_ref[...]
        else:
            gbuf_s[:, tail, :] = jnp.zeros((n_seq, CONV_F - 1, D_FF), F32)
        return
    if stage == 'final':
        tail_ref[...] = gbuf_s[:, tt + HIST - (CONV_F - 1):tt + HIST, :]
        return

    xb_s[...] = x.astype(BF16)
    for j in range(N_FF_CHUNKS):
        cols = slice(j * FF_CHUNK, (j + 1) * FF_CHUNK)
        gate = _bdot(xb_s[...], w_up_ref[:, cols])
        val = _bdot(xb_s[...], w_up_ref[:, D_FF + j * FF_CHUNK:D_FF + (j + 1) * FF_CHUNK])
        gbuf_s[:, HIST:, cols] = gate.reshape(n_seq, tt, FF_CHUNK)
        conv = _conv_from_buffer(gbuf_s, vec_ref, ROW_CONV_F, CONV_F, tt, cols).reshape(rows, FF_CHUNK)
        h = _gelu_tanh(conv + vec_ref[ROW_CONV_F_BIAS:ROW_CONV_F_BIAS + 1, cols]) * val
        h_s[:, cols] = h.astype(BF16)
    gbuf_s[:, tail, :] = gbuf_s[:, tt + HIST - (CONV_F - 1):tt + HIST, :]
    y = ALPHA * x + _bdot(h_s[...], wd_ref[...])
    ln_g = vec_ref[ROW_LN2_G:ROW_LN2_G + 1, 0:D_MODEL]
    ln_b = vec_ref[ROW_LN2_B:ROW_LN2_B + 1, 0:D_MODEL]
    y_ref[...] = _layer_norm(y, ln_g, ln_b).reshape(n_seq, tt, D_MODEL)


def _mixer_kernel(*refs, n_seq, tt, c, has_state, n_alias):
    n_state = N_MIXER_OUTS if has_state else 0
    k = 1 + n_state
    x_ref, state_refs = refs[0], refs[1:k]
    weight_refs = refs[k:k + N_MIXER_WEIGHTS]
    k += N_MIXER_WEIGHTS + n_alias
    x1_ref = refs[k]
    out_refs = refs[k + 1:k + 1 + N_MIXER_OUTS]
    scratch_refs = refs[k + 1 + N_MIXER_OUTS:]
    t_idx = pl.program_id(1)

    def write_x1(val):
        x1_ref[...] = val.reshape(n_seq, tt, D_MODEL)

    stage = functools.partial(_mixer_stage, x_ref=x_ref, state_refs=state_refs, weight_refs=weight_refs,
                              out_refs=out_refs, scratch_refs=scratch_refs, write_x1=write_x1,
                              n_seq=n_seq, tt=tt, c=c)
    pl.when(t_idx == 0)(lambda: stage('init'))
    stage('body')
    pl.when(t_idx == pl.num_programs(1) - 1)(lambda: stage('final'))


def _ffn_kernel(*refs, n_seq, tt, has_state, n_alias):
    k = 2 if has_state else 1
    x_ref = refs[0]
    state_ref = refs[1] if has_state else None
    weight_refs = refs[k:k + 3]
    k += 3 + n_alias
    y_ref, tail_ref = refs[k:k + 2]
    scratch_refs = refs[k + 2:]
    t_idx = pl.program_id(1)
    stage = functools.partial(_ffn_stage, state_ref=state_ref, weight_refs=weight_refs, y_ref=y_ref,
                              tail_ref=tail_ref, scratch_refs=scratch_refs, n_seq=n_seq, tt=tt)
    pl.when(t_idx == 0)(lambda: stage('init', None))
    stage('body', x_ref[...].reshape(n_seq * tt, D_MODEL))
    pl.when(t_idx == pl.num_programs(1) - 1)(lambda: stage('final', None))


def _block_call(kern, name, x, states, state_shapes, weights, scratch, layer, prev_outs, *, n_seq, tt):
    nb, t_total, _ = x.shape
    assert nb % n_seq == 0 and t_total % tt == 0
    x_spec = pl.BlockSpec((n_seq, tt, D_MODEL), lambda b, t: (b, t, 0))
    in_specs, operands = [x_spec], [x]
    if states is not None:
        for arr, tail in zip(states, state_shapes):
            assert arr.shape[2:] == tail
            in_specs.append(pl.BlockSpec((None, n_seq) + tail, lambda b, t, z=(0,) * len(tail): (layer, b) + z))
            operands.append(arr)
    for a, per_layer in weights:
        if per_layer:
            in_specs.append(pl.BlockSpec((None,) + a.shape[1:], lambda b, t, z=(0,) * (a.ndim - 1): (layer,) + z))
        else:
            in_specs.append(pl.BlockSpec(a.shape, lambda b, t, z=(0,) * a.ndim: z))
        operands.append(a)
    if states is not None:
        assert prev_outs is None
        aliases = {1 + i: 1 + i for i in range(len(states))}
        n_alias = 0
    else:
        aliases = {}
        for i, arr in enumerate(prev_outs):
            aliases[len(operands)] = 1 + i
            in_specs.append(pl.BlockSpec(memory_space=pl.ANY))
            operands.append(arr)
        n_alias = len(prev_outs)
    out_shape = [jax.ShapeDtypeStruct((nb, t_total, D_MODEL), F32)]
    out_specs = [x_spec]
    for tail in state_shapes:
        out_shape.append(jax.ShapeDtypeStruct((DEPTH, nb) + tail, F32))
        out_specs.append(pl.BlockSpec((None, n_seq) + tail, lambda b, t, z=(0,) * len(tail): (layer, b) + z))
    return pl.pallas_call(
        functools.partial(kern, n_alias=n_alias),
        grid=(nb // n_seq, t_total // tt),
        in_specs=in_specs,
        out_specs=out_specs,
        out_shape=out_shape,
        input_output_aliases=aliases,
        scratch_shapes=scratch,
        compiler_params=pltpu.CompilerParams(
            dimension_semantics=("arbitrary", "arbitrary"), vmem_limit_bytes=VMEM_LIMIT_BYTES),
        name=f"{name}_nseq{n_seq}_tt{tt}",
    )(*operands)


def _mixer_call(x, states, w, layer, prev_outs, *, n_seq, tt, c):
    assert (tt % MASK_ROWS == 0 and c == GROUP_ROWS) or (tt == HIST and c == HIST and n_seq * tt <= MASK_ROWS)
    assert (n_seq * tt) % GROUP_ROWS == 0
    state_shapes = MIXER_STATE_SHAPES
    weights = [(_tile_masks(min(n_seq * tt, MASK_ROWS), c), False)] + [
        (w[k], True) for k in ('w_in', 'mix_vec', 'wr', 'wi', 'w2', 'w_out')]
    kern = functools.partial(_mixer_kernel, n_seq=n_seq, tt=tt, c=c, has_state=states is not None)
    return _block_call(kern, 'mixer', x, states, state_shapes, weights, _mixer_scratch(n_seq, tt, c), layer,
                       prev_outs, n_seq=n_seq, tt=tt)


def _ffn_call(x, state, w, layer, prev_out, *, n_seq, tt):
    weights = [(w[k], True) for k in ('w_up', 'ffn_vec', 'w_down')]
    scratch = [pltpu.VMEM((n_seq, HIST + tt, D_FF), F32),
               pltpu.VMEM((n_seq * tt, D_FF), BF16),
               pltpu.VMEM((n_seq * tt, D_MODEL), BF16)]
    kern = functools.partial(_ffn_kernel, n_seq=n_seq, tt=tt, has_state=state is not None)
    return _block_call(kern, 'ffn', x, None if state is None else (state,), (FFN_STATE_SHAPE,), weights,
                       scratch, layer, None if prev_out is None else (prev_out,), n_seq=n_seq, tt=tt)


def _vec_table(pieces, width, n_rows):
    rows = []
    used = 0
    for arr, lane0 in pieces:
        arr = arr.astype(F32)
        rows.append(jnp.pad(arr, ((0, 0), (0, 0), (lane0, width - lane0 - arr.shape[2]))))
        used += arr.shape[1]
    depth = pieces[0][0].shape[0]
    rows.append(jnp.zeros((depth, n_rows - used, width), F32))
    return jnp.concatenate(rows, axis=1)


def _prep_weights(w_in, conv_a_w, a_log, dt_bias, norm_a_w, conv_b_w, conv_b_b, lru_w_r, lru_b_r, lru_w_i,
                  lru_b_i, lru_lambda, gla_w2, gla_b2, norm_c_w, w_out, ln1_g, ln1_b, ffn_w_up, ffn_conv_w,
                  ffn_conv_b, ffn_w_down, ln2_g, ln2_b):
    depth, d, _ = w_in.shape
    pts = [0]
    for s in (QK_A, QK_A, W_A, W_A, H_A, H_A, W_B, W_B, QK_C, QK_C, W_C, W_C, GLA_RANK):
        pts.append(pts[-1] + s)
    cols = lambda i, j: w_in[:, :, pts[i]:pts[j]]
    z = lambda n: jnp.zeros((depth, d, n), w_in.dtype)
    small = jnp.concatenate([cols(4, 5), z(LANE_DECAY - H_A), cols(5, 6), z(LANE_LOWRANK - LANE_DECAY - H_A),
                             cols(12, 13), z(LANES - LANE_LOWRANK - GLA_RANK)], axis=2)
    w_in_p = jnp.concatenate([cols(0, 4), cols(6, 8), cols(8, 9), z(QKC_PAD - QK_C), cols(9, 10),
                              z(QKC_PAD - QK_C), cols(10, 12), small], axis=2).astype(BF16)
    assert w_in_p.shape[2] == D_IN_PAD

    eye_blocks = jnp.eye(LRU_BLOCKS, dtype=F32)[None, :, None, :, None]

    def block_diag(wb):
        return (wb[:, :, :, None, :] * eye_blocks).reshape(depth, W_B, W_B).astype(BF16)

    w2 = jnp.pad(gla_w2, ((0, 0), (LANE_LOWRANK, LANES - LANE_LOWRANK - GLA_RANK), (0, QKC_PAD - QK_C))).astype(BF16)
    r1 = lambda v: v[:, None, :]
    dup = lambda v: jnp.concatenate([v, v], axis=1)[:, None, :]
    mix_vec = _vec_table(
        [(conv_a_w, 0), (conv_b_w, 0), (r1(conv_b_b), 0), (r1(a_log), LANE_DECAY), (r1(dt_bias), LANE_DECAY),
         (dup(norm_a_w), 0), (r1(lru_b_r), 0), (r1(lru_b_i), 0), (r1(lru_lambda), 0), (r1(gla_b2), 0),
         (dup(norm_c_w), 0), (r1(ln1_g), 0), (r1(ln1_b), 0)], VEC_W, MIX_VEC_ROWS)
    ffn_vec = _vec_table([(ffn_conv_w, 0), (r1(ffn_conv_b), 0), (r1(ln2_g), 0), (r1(ln2_b), 0)], D_FF, FFN_VEC_ROWS)
    return {
        'w_in': w_in_p, 'mix_vec': mix_vec, 'wr': block_diag(lru_w_r), 'wi': block_diag(lru_w_i), 'w2': w2,
        'w_out': w_out.astype(BF16), 'w_up': ffn_w_up.astype(BF16), 'ffn_vec': ffn_vec,
        'w_down': ffn_w_down.astype(BF16),
    }


def _trunk(x, states, w, *, n_seq, tt, c):
    nb = x.shape[0]
    if states is not None:
        st_dconv, st_delta, st_lconv, st_lru, st_gla, st_fconv = (s.astype(F32) for s in states)
        states = (st_dconv, st_delta.reshape(DEPTH, nb, N_PAIRS, 2 * DK_A, DV_A), st_lconv,
                  st_lru.reshape(DEPTH, nb, 1, W_B), st_gla.reshape(DEPTH, nb, QK_C, DV_C), st_fconv)
    has_state = states is not None
    if has_state:
        mix_outs, ffn_out = list(states[:5]), states[5]
    else:
        mix_outs = [jnp.zeros((DEPTH, nb) + tail, F32) for tail in MIXER_STATE_SHAPES]
        ffn_out = jnp.zeros((DEPTH, nb) + FFN_STATE_SHAPE, F32)
    for l in range(DEPTH):
        x1, *mix_outs = _mixer_call(x, mix_outs if has_state else None, w, l, None if has_state else mix_outs,
                                    n_seq=n_seq, tt=tt, c=c)
        if n_seq > 1:
            ffn_tile = dict(n_seq=math.gcd(nb, max(n_seq, FFN_TILE_ROWS // tt)), tt=tt)
        else:
            ffn_tile = dict(n_seq=1, tt=min(FFN_TILE_ROWS, x.shape[1]))
        x, ffn_out = _ffn_call(x1, ffn_out if has_state else None, w, l, None if has_state else ffn_out,
                               **ffn_tile)
    dconv, sd, lconv, h, sg = mix_outs
    return x, (dconv, sd.reshape(DEPTH, nb, H_A, DK_A, DV_A), lconv, h.reshape(DEPTH, nb, W_B),
               sg.reshape(DEPTH, nb, H_C, DK_C, DV_C), ffn_out)


def kernel(x_prompt, x_sample, state_delta_conv, state_delta, state_lru_conv, state_lru, state_gla, state_ffn_conv,
           w_in, conv_a_w, a_log, dt_bias, norm_a_w, conv_b_w, conv_b_b, lru_w_r, lru_b_r, lru_w_i, lru_b_i,
           lru_lambda, gla_w2, gla_b2, norm_c_w, w_out, ln1_g, ln1_b, ffn_w_up, ffn_conv_w, ffn_conv_b, ffn_w_down,
           ln2_g, ln2_b):
    assert w_in.shape[0] == DEPTH
    w = _prep_weights(w_in, conv_a_w, a_log, dt_bias, norm_a_w, conv_b_w, conv_b_b, lru_w_r, lru_b_r, lru_w_i,
                      lru_b_i, lru_lambda, gla_w2, gla_b2, norm_c_w, w_out, ln1_g, ln1_b, ffn_w_up, ffn_conv_w,
                      ffn_conv_b, ffn_w_down, ln2_g, ln2_b)
    sample_states = (state_delta_conv, state_delta, state_lru_conv, state_lru, state_gla, state_ffn_conv)
    t_p = x_prompt.shape[1]
    t_s = x_sample.shape[1]
    n_p = PROMPT_NSEQ if x_prompt.shape[0] % PROMPT_NSEQ == 0 else 1
    y_p, p_st = _trunk(x_prompt, None, w, n_seq=n_p, tt=min(PROMPT_TT, t_p), c=CHUNK)
    y_s, s_st = _trunk(x_sample, sample_states, w, n_seq=min(SAMPLE_NSEQ, x_sample.shape[0]), tt=t_s, c=t_s)
    return (y_p, y_s) + p_st + s_st
```

```python
import functools
import math

import numpy as np
import jax
import jax.numpy as jnp
from jax import lax
from jax.experimental import pallas as pl
from jax.experimental.pallas import tpu as pltpu

F32 = jnp.float32
BF16 = jnp.bfloat16

D_MODEL = 1024
H_A, DK_A, DV_A = 6, 64, 64
QK_A = H_A * DK_A
W_A = H_A * DV_A
CONV_A = 4
W_B = 256
LRU_BLOCKS = 4
LRU_BLOCK = W_B // LRU_BLOCKS
CONV_B = 4
LRU_C = 8.0
H_C, DK_C, DV_C = 6, 32, 64
QK_C = H_C * DK_C
W_C = H_C * DV_C
GLA_RANK = 16
GLA_TAU = 16.0
D_MIX = W_A + W_B + W_C
CHUNK = 64
D_FF = 2816
CONV_F = 3
EPS = 1e-6
CONV_A_WIDTH = 2 * QK_A + W_A
DEPTH = 2
ALPHA = (2.0 * DEPTH) ** 0.25

LANES = 128
SUBLANES = 8
HALF = LANES // 2
GROUP_ROWS = 64
STACK = 2 * GROUP_ROWS
N_PAIRS = 3
HIST = SUBLANES
FF_CHUNK = 256
N_FF_CHUNKS = D_FF // FF_CHUNK
VMEM_LIMIT_BYTES = 56 * 1024 * 1024
PROMPT_TT = 512
PROMPT_NSEQ = 1
MASK_ROWS = 256
FFN_TILE_ROWS = 512
SAMPLE_NSEQ = 16

OFF_QA = 0
OFF_KA = OFF_QA + QK_A
OFF_VA = OFF_KA + QK_A
OFF_ZA = OFF_VA + W_A
OFF_XB = OFF_ZA + W_A
OFF_GB = OFF_XB + W_B
OFF_QC = OFF_GB + W_B
QKC_PAD = 2 * LANES
OFF_KC = OFF_QC + QKC_PAD
OFF_VC = OFF_KC + QKC_PAD
OFF_ZC = OFF_VC + W_C
OFF_SMALL = OFF_ZC + W_C
D_IN_PAD = OFF_SMALL + LANES
LANE_BETA = 0
LANE_DECAY = 8
LANE_LOWRANK = 16

VEC_W = CONV_A_WIDTH
(ROW_CONV_A, ROW_CONV_B, ROW_CONV_B_BIAS, ROW_ALOG, ROW_DTB, ROW_NORM_A, ROW_BR, ROW_BI, ROW_LAM, ROW_B2,
 ROW_NORM_C, ROW_LN1_G, ROW_LN1_B) = (0, 4, 8, 9, 10, 11, 12, 13, 14, 15, 16, 17, 18)
MIX_VEC_ROWS = 24
(ROW_CONV_F, ROW_CONV_F_BIAS, ROW_LN2_G, ROW_LN2_B) = (0, 3, 4, 5)
FFN_VEC_ROWS = 8


def _bdot(a, b):
    return jnp.dot(a, b, preferred_element_type=F32)


def _mm(a, b):
    return _bdot(a.astype(BF16), b.astype(BF16))


def _mm_nt(a, b):
    return lax.dot_general(a.astype(BF16), b.astype(BF16), (((1,), (1,)), ((), ())), preferred_element_type=F32)


def _split(a, n):
    parts = []
    r = a
    for i in range(n):
        p = r.astype(BF16)
        parts.append(p)
        if i + 1 < n:
            r = r - p.astype(F32)
    return parts


def _mm_mask_lhs(mask_bf16, a, n):
    out = None
    for p in _split(a, n):
        t = _bdot(mask_bf16, p)
        out = t if out is None else out + t
    return out


def _softplus(x):
    return jnp.maximum(x, 0.0) + jnp.log1p(jnp.exp(-jnp.abs(x)))


def _sigmoid(x):
    return 0.5 * jnp.tanh(0.5 * x) + 0.5


def _silu(x):
    h = 0.5 * x
    return h * jnp.tanh(h) + h


def _gelu_tanh(x):
    return 0.5 * x * (1.0 + jnp.tanh(math.sqrt(2.0 / math.pi) * (x + 0.044715 * (x * x * x))))


def _layer_norm(x, g, b):
    mu = jnp.mean(x, axis=-1, keepdims=True)
    xc = x - mu
    var = jnp.mean(xc * xc, axis=-1, keepdims=True)
    return xc * lax.rsqrt(var + EPS) * g + b


def _iota(shape, axis):
    return lax.broadcasted_iota(jnp.int32, shape, axis)


def _log2(n):
    l = int(math.log2(n))
    assert (1 << l) == n
    return l


def _conv_from_buffer(buf_ref, vec_ref, row0, width, tt, cols):
    y = None
    for d in range(width):
        tap = vec_ref[row0 + width - 1 - d:row0 + width - d, cols]
        term = buf_ref[:, HIST - d:HIST - d + tt, cols] * tap
        y = term if y is None else y + term
    return y


def _stack_rows(n_sb, c, pieces):
    if n_sb == 1:
        return jnp.concatenate([pieces[0], pieces[1]], axis=0)
    out = []
    for s in range(n_sb):
        for a in range(2):
            out.append(pieces[a][s * c:(s + 1) * c])
    return jnp.concatenate(out, axis=0)


def _unstack_rows(o, n_sb, c):
    lo = _iota((c, LANES), 1) < HALF
    out = []
    for s in range(n_sb):
        base = s * 2 * c
        out.append(jnp.where(lo, o[base:base + c], o[base + c:base + 2 * c]))
    return out[0] if n_sb == 1 else jnp.concatenate(out, axis=0)


def _col_bcast_stack(vals, lane0, n_sb, c):
    pieces = [jnp.broadcast_to(vals[:, lane0 + a:lane0 + a + 1], (GROUP_ROWS, LANES)) for a in range(2)]
    return _stack_rows(n_sb, c, pieces)


def _dup_values(v):
    lo = _iota(v.shape, 1) < HALF
    vr = pltpu.roll(v, HALF, axis=1)
    return jnp.where(lo, v, vr), jnp.where(lo, vr, v)


def _tile_masks(rows, c):
    lc = _log2(c)
    i = np.arange(rows)[:, None]
    j = np.arange(rows)[None, :]
    same = (i >> lc) == (j >> lc)
    out = [same & (j <= i), same]
    for lvl in range(lc):
        lh = lc - 1 - lvl
        hi = (i >> lh) & 1
        start = (i >> lh) << lh
        nxt = ((i >> lh) + 1) << lh
        out.append((hi == 1) & (j >= start) & (j <= i))
        out.append((hi == 0) & (j > i) & (j < nxt))
    return jnp.asarray(np.stack(out).astype(np.float32), dtype=BF16)


MIXER_STATE_SHAPES = ((CONV_A - 1, CONV_A_WIDTH), (N_PAIRS, 2 * DK_A, DV_A), (CONV_B - 1, W_B), (1, W_B),
                      (QK_C, DV_C))
FFN_STATE_SHAPE = (CONV_F - 1, D_FF)
N_MIXER_WEIGHTS = 7
N_MIXER_OUTS = 5
N_MIXER_SCRATCH = 36
PROJ_SEGMENTS = ((OFF_QA, OFF_ZA), (OFF_ZA, OFF_XB), (OFF_XB, OFF_QC), (OFF_QC, OFF_VC), (OFF_VC, OFF_SMALL),
                 (OFF_SMALL, D_IN_PAD))


def _mixer_stage(stage, x_ref, state_refs, weight_refs, out_refs, scratch_refs, write_x1, *, n_seq, tt, c):
    has_state = len(state_refs) > 0
    (masks_ref, w_in_ref, vec_ref, wr_ref, wi_ref, w2_ref, w_out_ref) = weight_refs
    (dconv_out_ref, sd_out_ref, lconv_out_ref, h_out_ref, sg_out_ref) = out_refs
    (sd_s, sg_s, abuf_s, bbuf_s, h_s, heads_s, xb_s, proj_s,
     qa_s, ka_s, va_s, beta_s, gcum_s, gtot_s,
     qc_s, kc_s, vc_s, qb_s, kdec_s, btot_s, qt_s, kt_s,
     t_s, p_s, rhs_s, qe_s, qkd_s, kdst_s, gtst_s, oprime_s, qprime_s, c_s, kw_s, av_s, qbst_s, gk_s,
     ) = scratch_refs

    rows = n_seq * tt
    n_groups = rows // GROUP_ROWS
    n_sb = GROUP_ROWS // c
    lc = _log2(c)
    n_levels = lc
    a_tail = slice(HIST - (CONV_A - 1), HIST)
    b_tail = slice(HIST - (CONV_B - 1), HIST)

    if stage == 'init':
        if has_state:
            dconv_ref, sd_ref, lconv_ref, h0_ref, sg_ref = state_refs
            abuf_s[:, a_tail, :] = dconv_ref[...]
            bbuf_s[:, b_tail, :] = lconv_ref[...]
            h_s[...] = h0_ref[...].reshape(n_seq, W_B)
            zeros = jnp.zeros((GROUP_ROWS, LANES), F32)
            for s in range(n_seq):
                for p in range(N_PAIRS):
                    sd = sd_ref[s, p]
                    sd_s[s, p] = jnp.concatenate([sd, sd], axis=-1)
                    sg = sg_ref[s, 2 * DK_C * p:2 * DK_C * (p + 1), :]
                    sg = jnp.concatenate([sg, sg], axis=-1)
                    sg_s[s, p] = jnp.concatenate([sg, zeros] if p % 2 == 0 else [zeros, sg], axis=0)
        else:
            abuf_s[:, a_tail, :] = jnp.zeros((n_seq, CONV_A - 1, CONV_A_WIDTH), F32)
            bbuf_s[:, b_tail, :] = jnp.zeros((n_seq, CONV_B - 1, W_B), F32)
            h_s[...] = jnp.zeros((n_seq, W_B), F32)
            sd_s[...] = jnp.zeros(sd_s.shape, F32)
            sg_s[...] = jnp.zeros(sg_s.shape, F32)
        return

    if stage == 'final':
        dconv_out_ref[...] = abuf_s[:, tt + HIST - (CONV_A - 1):tt + HIST, :]
        lconv_out_ref[...] = bbuf_s[:, tt + HIST - (CONV_B - 1):tt + HIST, :]
        h_out_ref[...] = h_s[...].reshape(n_seq, 1, W_B)
        for s in range(n_seq):
            for p in range(N_PAIRS):
                sd_out_ref[s, p] = sd_s[s, p][:, 0:HALF]
                r0 = GROUP_ROWS * (p % 2)
                sg_out_ref[s, 2 * DK_C * p:2 * DK_C * (p + 1), :] = sg_s[s, p][r0:r0 + GROUP_ROWS, 0:HALF]
        return

    def vec(row, width):
        return vec_ref[row:row + 1, 0:width]

    x = x_ref[...].reshape(rows, D_MODEL)
    xb_s[...] = x.astype(BF16)
    proj = proj_s
    for a, b in PROJ_SEGMENTS:
        proj_s[:, a:b] = _bdot(xb_s[...], w_in_ref[:, a:b])
    m_cum = masks_ref[0]
    m_tot = masks_ref[1]
    mask_rows = masks_ref.shape[1]

    def masked_sum(mask, a, n):
        out = [_mm_mask_lhs(mask, a[r:r + mask_rows], n) for r in range(0, rows, mask_rows)]
        return out[0] if len(out) == 1 else jnp.concatenate(out, axis=0)

    abuf_s[:, HIST:, :] = proj[:, OFF_QA:OFF_QA + CONV_A_WIDTH].reshape(n_seq, tt, CONV_A_WIDTH)
    conv_a = _conv_from_buffer(abuf_s, vec_ref, ROW_CONV_A, CONV_A, tt, slice(0, CONV_A_WIDTH))
    qkv = _silu(conv_a.reshape(rows, CONV_A_WIDTH))
    abuf_s[:, a_tail, :] = abuf_s[:, tt + HIST - (CONV_A - 1):tt + HIST, :]
    p_i = _iota((LANES, LANES), 0) >> _log2(DK_A)
    p_j = _iota((LANES, LANES), 1) >> _log2(DK_A)
    pair_ones = jnp.where(p_i == p_j, 1.0, 0.0).astype(BF16)
    for p in range(N_PAIRS):
        for off, dst, scale in ((0, qa_s, DK_A ** -0.5), (QK_A, ka_s, 1.0)):
            v = qkv[:, off + p * LANES:off + (p + 1) * LANES]
            ssq = _bdot((v * v).astype(BF16), pair_ones)
            dst[:, p * LANES:(p + 1) * LANES] = v * (lax.rsqrt(ssq + EPS) * scale)
    va_s[...] = qkv[:, 2 * QK_A:]

    small = proj[:, OFF_SMALL:OFF_SMALL + LANES]
    beta_s[...] = _sigmoid(small)
    g_full = -jnp.exp(vec(ROW_ALOG, LANES)) * _softplus(small + vec(ROW_DTB, LANES))
    gcum_s[...] = masked_sum(m_cum, g_full, 2)
    gtot_s[...] = masked_sum(m_tot, g_full, 2)

    def mixer_b():
        bbuf_s[:, HIST:, :] = proj[:, OFF_XB:OFF_XB + W_B].reshape(n_seq, tt, W_B)
        xc = (_conv_from_buffer(bbuf_s, vec_ref, ROW_CONV_B, CONV_B, tt, slice(0, W_B)).reshape(rows, W_B)
              + vec(ROW_CONV_B_BIAS, W_B))
        bbuf_s[:, b_tail, :] = bbuf_s[:, tt + HIST - (CONV_B - 1):tt + HIST, :]
        gate_r = _sigmoid(_mm(xc, wr_ref[...]) + vec(ROW_BR, W_B))
        gate_i = _sigmoid(_mm(xc, wi_ref[...]) + vec(ROW_BI, W_B))
        log_a = -LRU_C * gate_r * _softplus(-vec(ROW_LAM, W_B))
        a_t = jnp.exp(log_a)
        b_t = jnp.sqrt(-jnp.tanh(log_a) * (a_t * a_t + 1.0)) * (gate_i * xc)
        i_seq = _iota((rows, W_B), 0) & (tt - 1)
        h_prev = jnp.concatenate([jnp.broadcast_to(h_s[s:s + 1, :], (tt, W_B)) for s in range(n_seq)], axis=0)
        b_t = b_t + jnp.where(i_seq == 0, a_t * h_prev, 0.0)
        d = 1
        while d < tt:
            a_sh = pltpu.roll(a_t, d, axis=0)
            b_sh = pltpu.roll(b_t, d, axis=0)
            ok = i_seq >= d
            b_t = jnp.where(ok, a_t * b_sh + b_t, b_t)
            a_t = jnp.where(ok, a_t * a_sh, a_t)
            d *= 2
        h_s[...] = jnp.concatenate([b_t[(s + 1) * tt - 1:(s + 1) * tt] for s in range(n_seq)], axis=0)
        heads_s[:, W_A:W_A + W_B] = (b_t * _gelu_tanh(proj[:, OFF_GB:OFF_GB + W_B])).astype(BF16)

    qc = proj[:, OFF_QC:OFF_QC + QKC_PAD] * (DK_C ** -0.5)
    kc = proj[:, OFF_KC:OFF_KC + QKC_PAD]
    logf = -_softplus(-(_mm(small, w2_ref[...]) + vec(ROW_B2, QKC_PAD))) * (1.0 / GLA_TAU)
    b_cum = masked_sum(m_cum, logf, 2)
    b_tot = masked_sum(m_tot, logf, 2)
    qc_s[...] = qc
    kc_s[...] = kc
    vc_s[...] = proj[:, OFF_VC:OFF_VC + W_C]
    qb_s[...] = qc * jnp.exp(b_cum)
    kdec_s[...] = kc * jnp.exp(b_tot - b_cum)
    btot_s[...] = b_tot
    for lvl in range(n_levels):
        qt_s[lvl] = qc * jnp.exp(masked_sum(masks_ref[2 + 2 * lvl], logf, 1))
        kt_s[lvl] = kc * jnp.exp(masked_sum(masks_ref[3 + 2 * lvl], logf, 1))

    s_i = _iota((STACK, STACK), 0)
    s_j = _iota((STACK, STACK), 1)
    same_blk = (s_i >> lc) == (s_j >> lc)
    m_incl = same_blk & (s_j <= s_i)
    m_strict = same_blk & (s_j < s_i)
    eye = jnp.where(s_i == s_j, 1.0, 0.0)
    lane = _iota((GROUP_ROWS, LANES), 1)
    m0 = lane < HALF
    norm_a = vec(ROW_NORM_A, LANES)
    norm_c = vec(ROW_NORM_C, LANES)

    def delta_setup(gi):
        rs = pl.ds(gi * GROUP_ROWS, GROUP_ROWS)
        beta_g = beta_s[rs, :]
        gcum_g = gcum_s[rs, :]
        gtot_g = gtot_s[rs, :]
        for p in range(N_PAIRS):
            sl = pl.ds(p * LANES, LANES)
            q_p = qa_s[rs, sl]
            k_p = ka_s[rs, sl]
            v0, v1 = _dup_values(va_s[rs, sl])
            q_st = _stack_rows(n_sb, c, [jnp.where(m0, q_p, 0.0), jnp.where(m0, 0.0, q_p)])
            k_st = _stack_rows(n_sb, c, [jnp.where(m0, k_p, 0.0), jnp.where(m0, 0.0, k_p)])
            v_st = _stack_rows(n_sb, c, [v0, v1])
            beta_c = _col_bcast_stack(beta_g, LANE_BETA + 2 * p, n_sb, c)
            g_c = _col_bcast_stack(gcum_g, LANE_DECAY + 2 * p, n_sb, c)
            gt_c = _col_bcast_stack(gtot_g, LANE_DECAY + 2 * p, n_sb, c)
            dec = jnp.exp(jnp.where(m_incl, g_c - g_c.T, -1e30))
            kq = _mm_nt(jnp.concatenate([k_st, q_st], axis=0), k_st)
            kk, qk = kq[0:STACK], kq[STACK:]
            n_mat = -(beta_c * kk * jnp.where(m_strict, dec, 0.0))
            eg = jnp.exp(g_c)
            t_s[gi, p] = eye + n_mat
            p_s[gi, p] = n_mat.astype(BF16)
            rhs_s[gi, p] = jnp.concatenate([beta_c * v_st, beta_c * eg * k_st], axis=1).astype(BF16)
            qe_s[gi, p] = (eg * q_st).astype(BF16)
            qkd_s[gi, p] = (qk * dec).astype(BF16)
            kdst_s[gi, p] = k_st * jnp.exp(gt_c - g_c)
            gtst_s[gi, p] = gt_c

    def inverse_level(gi, j):
        for p in range(N_PAIRS):
            pw = p_s[gi, p]
            if j >= 2:
                t_old = t_s[gi, p]
                both = _bdot(jnp.concatenate([pw, t_old.astype(BF16)], axis=0), pw)
                t_s[gi, p] = t_old + both[STACK:]
                p_s[gi, p] = both[0:STACK].astype(BF16)
            else:
                p_s[gi, p] = _bdot(pw, pw).astype(BF16)

    def delta_solve(gi):
        for p in range(N_PAIRS):
            t_old = t_s[gi, p]
            t_fin = t_old + _bdot(t_old.astype(BF16), p_s[gi, p])
            uwk = _bdot(t_fin.astype(BF16), rhs_s[gi, p]).astype(BF16)
            k_dec = kdst_s[gi, p]
            if n_sb == 1:
                both = _bdot(jnp.concatenate([qkd_s[gi, p], k_dec.T.astype(BF16)], axis=0), uwk)
                o_qw, c_kw = both[0:STACK], [both[STACK:]]
            else:
                o_qw = _bdot(qkd_s[gi, p], uwk)
                c_kw = [_bdot(k_dec[sb * 2 * c:(sb + 1) * 2 * c].T.astype(BF16), uwk[sb * 2 * c:(sb + 1) * 2 * c])
                        for sb in range(n_sb)]
            oprime_s[gi, p] = o_qw[:, 0:LANES]
            qprime_s[gi, p] = (qe_s[gi, p].astype(F32) - o_qw[:, LANES:]).astype(BF16)
            for sb in range(n_sb):
                c_s[gi, p, sb] = c_kw[sb][:, 0:LANES]
                kw_s[gi, p, sb] = c_kw[sb][:, LANES:].astype(BF16)

    def gla_precompute(gi):
        rs = pl.ds(gi * GROUP_ROWS, GROUP_ROWS)
        for p in range(N_PAIRS):
            sl = pl.ds(p * LANES, LANES)
            slab = pl.ds((p // 2) * LANES, LANES)
            l0 = 2 * DK_C * (p % 2)
            mh0 = (lane >= l0) & (lane < l0 + DK_C)
            mh1 = (lane >= l0 + DK_C) & (lane < l0 + 2 * DK_C)

            def stack_c(ref, idx=None):
                val = ref[rs, slab] if idx is None else ref[idx, rs, slab]
                return _stack_rows(n_sb, c, [jnp.where(mh0, val, 0.0), jnp.where(mh1, val, 0.0)])

            def stack_k(ref, idx=None):
                val = ref[rs, slab] if idx is None else ref[idx, rs, slab]
                return _stack_rows(n_sb, c, [val, val])

            att = jnp.where(s_i == s_j, _mm_nt(stack_c(qc_s), stack_k(kc_s)), 0.0)
            for lvl in range(n_levels):
                lh = lc - 1 - lvl
                half = 1 << lh
                q_l = stack_c(qt_s, lvl)
                k_l = stack_k(kt_s, lvl)
                if half >= SUBLANES:
                    q_up = jnp.concatenate([q_l[b0 + half:b0 + 2 * half] for b0 in range(0, STACK, 2 * half)], axis=0)
                    prod = _mm_nt(q_up, k_l)
                    u_i = _iota(prod.shape, 0)
                    u_j = _iota(prod.shape, 1)
                    ok = ((u_i >> lh) == (u_j >> (lh + 1))) & (((u_j >> lh) & 1) == 0)
                    prod = jnp.where(ok, prod, 0.0)
                    zeros = jnp.zeros((half, STACK), F32)
                    pieces = []
                    for n_blk in range(STACK // (2 * half)):
                        pieces += [zeros, prod[n_blk * half:(n_blk + 1) * half]]
                    att = att + jnp.concatenate(pieces, axis=0)
                else:
                    valid = (((s_i >> (lh + 1)) == (s_j >> (lh + 1)))
                             & (((s_i >> lh) & 1) == 1) & (((s_j >> lh) & 1) == 0))
                    att = att + jnp.where(valid, _mm_nt(q_l, k_l), 0.0)
            vc0, vc1 = _dup_values(vc_s[rs, sl])
            vc_st = _stack_rows(n_sb, c, [vc0, vc1]).astype(BF16)
            qbst_s[gi, p] = stack_c(qb_s).astype(BF16)
            kd_st = stack_c(kdec_s)
            if n_sb == 1:
                both = _bdot(jnp.concatenate([att.astype(BF16), kd_st.T.astype(BF16)], axis=0), vc_st)
                av_s[gi, p] = both[0:STACK]
                gk_s[gi, p, 0] = both[STACK:]
            else:
                av_s[gi, p] = _bdot(att.astype(BF16), vc_st)
                for sb in range(n_sb):
                    blk = slice(sb * 2 * c, (sb + 1) * 2 * c)
                    gk_s[gi, p, sb] = _bdot(kd_st[blk].T.astype(BF16), vc_st[blk])

    for gi in range(n_groups):
        delta_setup(gi)
    for j in range(1, lc):
        for gi in range(n_groups):
            inverse_level(gi, j)
        if j <= n_groups:
            gla_precompute(j - 1)
        if j == 1:
            mixer_b()
    for gi in range(lc - 1, n_groups):
        gla_precompute(gi)
    for gi in range(n_groups):
        delta_solve(gi)

    def recurrence_body(gi):
        rs = pl.ds(gi * GROUP_ROWS, GROUP_ROWS)
        for p in range(N_PAIRS):
            sl = pl.ds(p * LANES, LANES)
            q_prime = qprime_s[gi, p]
            gt_c = gtst_s[gi, p]
            qs_parts = []
            for sb in range(n_sb):
                seq = (gi * GROUP_ROWS + sb * c) // tt
                blk = slice(sb * 2 * c, (sb + 1) * 2 * c)
                s_old = sd_s[seq, p]
                s_bf = s_old.astype(BF16)
                if n_sb == 1:
                    both = _bdot(jnp.concatenate([q_prime, kw_s[gi, p, 0]], axis=0), s_bf)
                    qs, kws = both[0:STACK], both[STACK:]
                else:
                    qs, kws = _bdot(q_prime[blk], s_bf), _bdot(kw_s[gi, p, sb], s_bf)
                qs_parts.append(qs)
                decay_rows = jnp.concatenate(
                    [jnp.broadcast_to(gt_c[sb * 2 * c + a * c:sb * 2 * c + a * c + 1, :], (GROUP_ROWS, LANES))
                     for a in range(2)], axis=0)
                sd_s[seq, p] = jnp.exp(decay_rows) * s_old + (c_s[gi, p, sb] - kws)
            qs_all = qs_parts[0] if n_sb == 1 else jnp.concatenate(qs_parts, axis=0)
            o_st = qs_all + oprime_s[gi, p]
            o_st = o_st * lax.rsqrt(jnp.mean(o_st * o_st, axis=-1, keepdims=True) + EPS) * norm_a
            z_a = proj[gi * GROUP_ROWS:(gi + 1) * GROUP_ROWS, OFF_ZA + p * LANES:OFF_ZA + (p + 1) * LANES]
            heads_s[rs, sl] = (_unstack_rows(o_st, n_sb, c) * _silu(z_a)).astype(BF16)
            slab = pl.ds((p // 2) * LANES, LANES)
            qb_st = qbst_s[gi, p]
            btot_t = btot_s[rs, slab].T
            oi_parts = []
            for sb in range(n_sb):
                seq = (gi * GROUP_ROWS + sb * c) // tt
                blk = slice(sb * 2 * c, (sb + 1) * 2 * c)
                s_old = sg_s[seq, p]
                oi_parts.append(_bdot(qb_st[blk], s_old.astype(BF16)))
                decay_col = jnp.broadcast_to(btot_t[:, sb * c:sb * c + 1], (LANES, LANES))
                sg_s[seq, p] = jnp.exp(decay_col) * s_old + gk_s[gi, p, sb]
            oi_all = oi_parts[0] if n_sb == 1 else jnp.concatenate(oi_parts, axis=0)
            oc_st = oi_all + av_s[gi, p]
            oc_st = oc_st * lax.rsqrt(jnp.mean(oc_st * oc_st, axis=-1, keepdims=True) + EPS) * norm_c
            z_c = proj[gi * GROUP_ROWS:(gi + 1) * GROUP_ROWS, OFF_ZC + p * LANES:OFF_ZC + (p + 1) * LANES]
            heads_s[rs, pl.ds(W_A + W_B + p * LANES, LANES)] = (
                _unstack_rows(oc_st, n_sb, c) * _silu(z_c)).astype(BF16)

    groups_per_seq = max(tt // GROUP_ROWS, 1)
    for k in range(groups_per_seq):
        for gi in range(k, n_groups, groups_per_seq):
            recurrence_body(gi)

    y = ALPHA * x + _bdot(heads_s[...], w_out_ref[...])
    write_x1(_layer_norm(y, vec(ROW_LN1_G, D_MODEL), vec(ROW_LN1_B, D_MODEL)))


def _mixer_scratch(n_seq, tt, c):
    rows = n_seq * tt
    n_levels = _log2(c)
    n_groups = rows // GROUP_ROWS
    scratch = [
        pltpu.VMEM((n_seq, N_PAIRS, STACK, LANES), F32),
        pltpu.VMEM((n_seq, N_PAIRS, LANES, LANES), F32),
        pltpu.VMEM((n_seq, HIST + tt, CONV_A_WIDTH), F32),
        pltpu.VMEM((n_seq, HIST + tt, W_B), F32),
        pltpu.VMEM((n_seq, W_B), F32),
        pltpu.VMEM((rows, D_MIX), BF16),
        pltpu.VMEM((rows, D_MODEL), BF16),
        pltpu.VMEM((rows, D_IN_PAD), F32),
        pltpu.VMEM((rows, QK_A), F32),
        pltpu.VMEM((rows, QK_A), F32),
        pltpu.VMEM((rows, W_A), F32),
        pltpu.VMEM((rows, LANES), F32),
        pltpu.VMEM((rows, LANES), F32),
        pltpu.VMEM((rows, LANES), F32),
        pltpu.VMEM((rows, QKC_PAD), F32),
        pltpu.VMEM((rows, QKC_PAD), F32),
        pltpu.VMEM((rows, W_C), F32),
        pltpu.VMEM((rows, QKC_PAD), F32),
        pltpu.VMEM((rows, QKC_PAD), F32),
        pltpu.VMEM((rows, QKC_PAD), F32),
        pltpu.VMEM((n_levels, rows, QKC_PAD), F32),
        pltpu.VMEM((n_levels, rows, QKC_PAD), F32),
        pltpu.VMEM((n_groups, N_PAIRS, STACK, STACK), F32),
        pltpu.VMEM((n_groups, N_PAIRS, STACK, STACK), BF16),
        pltpu.VMEM((n_groups, N_PAIRS, STACK, 2 * LANES), BF16),
        pltpu.VMEM((n_groups, N_PAIRS, STACK, LANES), BF16),
        pltpu.VMEM((n_groups, N_PAIRS, STACK, STACK), BF16),
        pltpu.VMEM((n_groups, N_PAIRS, STACK, LANES), F32),
        pltpu.VMEM((n_groups, N_PAIRS, STACK, LANES), F32),
        pltpu.VMEM((n_groups, N_PAIRS, STACK, LANES), F32),
        pltpu.VMEM((n_groups, N_PAIRS, STACK, LANES), BF16),
        pltpu.VMEM((n_groups, N_PAIRS, GROUP_ROWS // c, LANES, LANES), F32),
        pltpu.VMEM((n_groups, N_PAIRS, GROUP_ROWS // c, LANES, LANES), BF16),
        pltpu.VMEM((n_groups, N_PAIRS, STACK, LANES), F32),
        pltpu.VMEM((n_groups, N_PAIRS, STACK, LANES), BF16),
        pltpu.VMEM((n_groups, N_PAIRS, GROUP_ROWS // c, LANES, LANES), F32),
    ]
    assert len(scratch) == N_MIXER_SCRATCH
    return scratch


def _ffn_stage(stage, x, state_ref, weight_refs, y_ref, tail_ref, scratch_refs, *, n_seq, tt):
    (w_up_ref, vec_ref, wd_ref) = weight_refs
    (gbuf_s, h_s, xb_s) = scratch_refs
    rows = n_seq * tt
    tail = slice(HIST - (CONV_F - 1), HIST)

    if stage == 'init':
        if state_ref is not None:
            gbuf_s[:, tail, :] = state_ref[...]
        else:
            gbuf_s[:, tail, :] = jnp.zeros((n_seq, CONV_F - 1, D_FF), F32)
        return
    if stage == 'final':
        tail_ref[...] = gbuf_s[:, tt + HIST - (CONV_F - 1):tt + HIST, :]
        return

    xb_s[...] = x.astype(BF16)
    for j in range(N_FF_CHUNKS):
        cols = slice(j * FF_CHUNK, (j + 1) * FF_CHUNK)
        gate = _bdot(xb_s[...], w_up_ref[:, cols])
        val = _bdot(xb_s[...], w_up_ref[:, D_FF + j * FF_CHUNK:D_FF + (j + 1) * FF_CHUNK])
        gbuf_s[:, HIST:, cols] = gate.reshape(n_seq, tt, FF_CHUNK)
        conv = _conv_from_buffer(gbuf_s, vec_ref, ROW_CONV_F, CONV_F, tt, cols).reshape(rows, FF_CHUNK)
        h = _gelu_tanh(conv + vec_ref[ROW_CONV_F_BIAS:ROW_CONV_F_BIAS + 1, cols]) * val
        h_s[:, cols] = h.astype(BF16)
    gbuf_s[:, tail, :] = gbuf_s[:, tt + HIST - (CONV_F - 1):tt + HIST, :]
    y = ALPHA * x + _bdot(h_s[...], wd_ref[...])
    ln_g = vec_ref[ROW_LN2_G:ROW_LN2_G + 1, 0:D_MODEL]
    ln_b = vec_ref[ROW_LN2_B:ROW_LN2_B + 1, 0:D_MODEL]
    y_ref[...] = _layer_norm(y, ln_g, ln_b).reshape(n_seq, tt, D_MODEL)


def _mixer_kernel(*refs, n_seq, tt, c, has_state, n_alias):
    n_state = N_MIXER_OUTS if has_state else 0
    k = 1 + n_state
    x_ref, state_refs = refs[0], refs[1:k]
    weight_refs = refs[k:k + N_MIXER_WEIGHTS]
    k += N_MIXER_WEIGHTS + n_alias
    x1_ref = refs[k]
    out_refs = refs[k + 1:k + 1 + N_MIXER_OUTS]
    scratch_refs = refs[k + 1 + N_MIXER_OUTS:]
    t_idx = pl.program_id(1)

    def write_x1(val):
        x1_ref[...] = val.reshape(n_seq, tt, D_MODEL)

    stage = functools.partial(_mixer_stage, x_ref=x_ref, state_refs=state_refs, weight_refs=weight_refs,
                              out_refs=out_refs, scratch_refs=scratch_refs, write_x1=write_x1,
                              n_seq=n_seq, tt=tt, c=c)
    pl.when(t_idx == 0)(lambda: stage('init'))
    stage('body')
    pl.when(t_idx == pl.num_programs(1) - 1)(lambda: stage('final'))


def _ffn_kernel(*refs, n_seq, tt, has_state, n_alias):
    k = 2 if has_state else 1
    x_ref = refs[0]
    state_ref = refs[1] if has_state else None
    weight_refs = refs[k:k + 3]
    k += 3 + n_alias
    y_ref, tail_ref = refs[k:k + 2]
    scratch_refs = refs[k + 2:]
    t_idx = pl.program_id(1)
    stage = functools.partial(_ffn_stage, state_ref=state_ref, weight_refs=weight_refs, y_ref=y_ref,
                              tail_ref=tail_ref, scratch_refs=scratch_refs, n_seq=n_seq, tt=tt)
    pl.when(t_idx == 0)(lambda: stage('init', None))
    stage('body', x_ref[...].reshape(n_seq * tt, D_MODEL))
    pl.when(t_idx == pl.num_programs(1) - 1)(lambda: stage('final', None))


def _block_call(kern, name, x, states, state_shapes, weights, scratch, layer, prev_outs, *, n_seq, tt):
    nb, t_total, _ = x.shape
    assert nb % n_seq == 0 and t_total % tt == 0
    x_spec = pl.BlockSpec((n_seq, tt, D_MODEL), lambda b, t: (b, t, 0))
    in_specs, operands = [x_spec], [x]
    if states is not None:
        for arr, tail in zip(states, state_shapes):
            assert arr.shape[2:] == tail
            in_specs.append(pl.BlockSpec((None, n_seq) + tail, lambda b, t, z=(0,) * len(tail): (layer, b) + z))
            operands.append(arr)
    for a, per_layer in weights:
        if per_layer:
            in_specs.append(pl.BlockSpec((None,) + a.shape[1:], lambda b, t, z=(0,) * (a.ndim - 1): (layer,) + z))
        else:
            in_specs.append(pl.BlockSpec(a.shape, lambda b, t, z=(0,) * a.ndim: z))
        operands.append(a)
    if states is not None:
        assert prev_outs is None
        aliases = {1 + i: 1 + i for i in range(len(states))}
        n_alias = 0
    else:
        aliases = {}
        for i, arr in enumerate(prev_outs):
            aliases[len(operands)] = 1 + i
            in_specs.append(pl.BlockSpec(memory_space=pl.ANY))
            operands.append(arr)
        n_alias = len(prev_outs)
    out_shape = [jax.ShapeDtypeStruct((nb, t_total, D_MODEL), F32)]
    out_specs = [x_spec]
    for tail in state_shapes:
        out_shape.append(jax.ShapeDtypeStruct((DEPTH, nb) + tail, F32))
        out_specs.append(pl.BlockSpec((None, n_seq) + tail, lambda b, t, z=(0,) * len(tail): (layer, b) + z))
    return pl.pallas_call(
        functools.partial(kern, n_alias=n_alias),
        grid=(nb // n_seq, t_total // tt),
        in_specs=in_specs,
        out_specs=out_specs,
        out_shape=out_shape,
        input_output_aliases=aliases,
        scratch_shapes=scratch,
        compiler_params=pltpu.CompilerParams(
            dimension_semantics=("arbitrary", "arbitrary"), vmem_limit_bytes=VMEM_LIMIT_BYTES),
        name=f"{name}_nseq{n_seq}_tt{tt}",
    )(*operands)


def _mixer_call(x, states, w, layer, prev_outs, *, n_seq, tt, c):
    assert (tt % MASK_ROWS == 0 and c == GROUP_ROWS) or (tt == HIST and c == HIST and n_seq * tt <= MASK_ROWS)
    assert (n_seq * tt) % GROUP_ROWS == 0
    state_shapes = MIXER_STATE_SHAPES
    weights = [(_tile_masks(min(n_seq * tt, MASK_ROWS), c), False)] + [
        (w[k], True) for k in ('w_in', 'mix_vec', 'wr', 'wi', 'w2', 'w_out')]
    kern = functools.partial(_mixer_kernel, n_seq=n_seq, tt=tt, c=c, has_state=states is not None)
    return _block_call(kern, 'mixer', x, states, state_shapes, weights, _mixer_scratch(n_seq, tt, c), layer,
                       prev_outs, n_seq=n_seq, tt=tt)


def _ffn_call(x, state, w, layer, prev_out, *, n_seq, tt):
    weights = [(w[k], True) for k in ('w_up', 'ffn_vec', 'w_down')]
    scratch = [pltpu.VMEM((n_seq, HIST + tt, D_FF), F32),
               pltpu.VMEM((n_seq * tt, D_FF), BF16),
               pltpu.VMEM((n_seq * tt, D_MODEL), BF16)]
    kern = functools.partial(_ffn_kernel, n_seq=n_seq, tt=tt, has_state=state is not None)
    return _block_call(kern, 'ffn', x, None if state is None else (state,), (FFN_STATE_SHAPE,), weights,
                       scratch, layer, None if prev_out is None else (prev_out,), n_seq=n_seq, tt=tt)


def _vec_table(pieces, width, n_rows):
    rows = []
    used = 0
    for arr, lane0 in pieces:
        arr = arr.astype(F32)
        rows.append(jnp.pad(arr, ((0, 0), (0, 0), (lane0, width - lane0 - arr.shape[2]))))
        used += arr.shape[1]
    depth = pieces[0][0].shape[0]
    rows.append(jnp.zeros((depth, n_rows - used, width), F32))
    return jnp.concatenate(rows, axis=1)


def _prep_weights(w_in, conv_a_w, a_log, dt_bias, norm_a_w, conv_b_w, conv_b_b, lru_w_r, lru_b_r, lru_w_i,
                  lru_b_i, lru_lambda, gla_w2, gla_b2, norm_c_w, w_out, ln1_g, ln1_b, ffn_w_up, ffn_conv_w,
                  ffn_conv_b, ffn_w_down, ln2_g, ln2_b):
    depth, d, _ = w_in.shape
    pts = [0]
    for s in (QK_A, QK_A, W_A, W_A, H_A, H_A, W_B, W_B, QK_C, QK_C, W_C, W_C, GLA_RANK):
        pts.append(pts[-1] + s)
    (qa, ka, va, za, ba, aa, xb, gb, qc, kc, vc, zc, lc) = [w_in[:, :, pts[i]:pts[i + 1]] for i in range(13)]
    z = lambda n: jnp.zeros((depth, d, n), w_in.dtype)
    small = jnp.concatenate([ba, z(LANE_DECAY - H_A), aa, z(LANE_LOWRANK - LANE_DECAY - H_A), lc,
                             z(LANES - LANE_LOWRANK - GLA_RANK)], axis=2)
    w_in_p = jnp.concatenate([qa, ka, va, za, xb, gb, qc, z(QKC_PAD - QK_C), kc, z(QKC_PAD - QK_C), vc, zc, small],
                             axis=2).astype(BF16)
    assert w_in_p.shape[2] == D_IN_PAD

    eye_blocks = jnp.eye(LRU_BLOCKS, dtype=F32)[None, :, None, :, None]

    def block_diag(wb):
        return (wb[:, :, :, None, :] * eye_blocks).reshape(depth, W_B, W_B).astype(BF16)

    w2 = jnp.pad(gla_w2, ((0, 0), (LANE_LOWRANK, LANES - LANE_LOWRANK - GLA_RANK), (0, QKC_PAD - QK_C))).astype(BF16)
    r1 = lambda v: v[:, None, :]
    dup = lambda v: jnp.concatenate([v, v], axis=1)[:, None, :]
    mix_vec = _vec_table(
        [(conv_a_w, 0), (conv_b_w, 0), (r1(conv_b_b), 0), (r1(a_log), LANE_DECAY), (r1(dt_bias), LANE_DECAY),
         (dup(norm_a_w), 0), (r1(lru_b_r), 0), (r1(lru_b_i), 0), (r1(lru_lambda), 0), (r1(gla_b2), 0),
         (dup(norm_c_w), 0), (r1(ln1_g), 0), (r1(ln1_b), 0)], VEC_W, MIX_VEC_ROWS)
    ffn_vec = _vec_table([(ffn_conv_w, 0), (r1(ffn_conv_b), 0), (r1(ln2_g), 0), (r1(ln2_b), 0)], D_FF, FFN_VEC_ROWS)
    return {
        'w_in': w_in_p, 'mix_vec': mix_vec, 'wr': block_diag(lru_w_r), 'wi': block_diag(lru_w_i), 'w2': w2,
        'w_out': w_out.astype(BF16), 'w_up': ffn_w_up.astype(BF16), 'ffn_vec': ffn_vec,
        'w_down': ffn_w_down.astype(BF16),
    }


def _trunk(x, states, w, *, n_seq, tt, c):
    nb = x.shape[0]
    if states is not None:
        st_dconv, st_delta, st_lconv, st_lru, st_gla, st_fconv = (s.astype(F32) for s in states)
        states = (st_dconv, st_delta.reshape(DEPTH, nb, N_PAIRS, 2 * DK_A, DV_A), st_lconv,
                  st_lru.reshape(DEPTH, nb, 1, W_B), st_gla.reshape(DEPTH, nb, QK_C, DV_C), st_fconv)
    has_state = states is not None
    if has_state:
        mix_outs, ffn_out = list(states[:5]), states[5]
    else:
        mix_outs = [jnp.zeros((DEPTH, nb) + tail, F32) for tail in MIXER_STATE_SHAPES]
        ffn_out = jnp.zeros((DEPTH, nb) + FFN_STATE_SHAPE, F32)
    for l in range(DEPTH):
        x1, *mix_outs = _mixer_call(x, mix_outs if has_state else None, w, l, None if has_state else mix_outs,
                                    n_seq=n_seq, tt=tt, c=c)
        if n_seq > 1:
            ffn_tile = dict(n_seq=math.gcd(nb, max(n_seq, FFN_TILE_ROWS // tt)), tt=tt)
        else:
            ffn_tile = dict(n_seq=1, tt=min(FFN_TILE_ROWS, x.shape[1]))
        x, ffn_out = _ffn_call(x1, ffn_out if has_state else None, w, l, None if has_state else ffn_out,
                               **ffn_tile)
    dconv, sd, lconv, h, sg = mix_outs
    return x, (dconv, sd.reshape(DEPTH, nb, H_A, DK_A, DV_A), lconv, h.reshape(DEPTH, nb, W_B),
               sg.reshape(DEPTH, nb, H_C, DK_C, DV_C), ffn_out)


def kernel(x_prompt, x_sample, state_delta_conv, state_delta, state_lru_conv, state_lru, state_gla, state_ffn_conv,
           w_in, conv_a_w, a_log, dt_bias, norm_a_w, conv_b_w, conv_b_b, lru_w_r, lru_b_r, lru_w_i, lru_b_i,
           lru_lambda, gla_w2, gla_b2, norm_c_w, w_out, ln1_g, ln1_b, ffn_w_up, ffn_conv_w, ffn_conv_b, ffn_w_down,
           ln2_g, ln2_b):
    assert w_in.shape[0] == DEPTH
    w = _prep_weights(w_in, conv_a_w, a_log, dt_bias, norm_a_w, conv_b_w, conv_b_b, lru_w_r, lru_b_r, lru_w_i,
                      lru_b_i, lru_lambda, gla_w2, gla_b2, norm_c_w, w_out, ln1_g, ln1_b, ffn_w_up, ffn_conv_w,
                      ffn_conv_b, ffn_w_down, ln2_g, ln2_b)
    sample_states = (state_delta_conv, state_delta, state_lru_conv, state_lru, state_gla, state_ffn_conv)
    t_p = x_prompt.shape[1]
    t_s = x_sample.shape[1]
    n_p = PROMPT_NSEQ if x_prompt.shape[0] % PROMPT_NSEQ == 0 else 1
    y_p, p_st = _trunk(x_prompt, None, w, n_seq=n_p, tt=min(PROMPT_TT, t_p), c=CHUNK)
    y_s, s_st = _trunk(x_sample, sample_states, w, n_seq=min(SAMPLE_NSEQ, x_sample.shape[0]), tt=t_s, c=t_s)
    return (y_p, y_s) + p_st + s_st
```
